```python
import jax, jax.numpy as jnp
from jax import lax
import numpy as np

D_MODEL = 1024
BATCH = 8
SEQ = 2048
DEPTH = 4
DEC_BATCH = 32
DEC_SEQ = 64
PAST_LEN = 1024

CHUNK = 64
N_MIXERS = 3
N_SB = (DEPTH + 2) // 3
N_CONV = (DEPTH + 1) // 3
N_GLA = DEPTH // 3
SB_HEADS = 16
SB_HEAD_DIM = D_MODEL // SB_HEADS
SB_Q_BLOCK = 128
CONV_WIDTH = 3
GLA_HEADS = 4
GLA_KEY_DIM = D_MODEL // 2
GLA_VALUE_DIM = D_MODEL
GLA_DK = GLA_KEY_DIM // GLA_HEADS
GLA_DV = GLA_VALUE_DIM // GLA_HEADS
GLA_GATE_RANK = 16
GLA_GATE_NORMALIZER = 16.0
GLA_IN_DIM = 2 * GLA_KEY_DIM + 2 * GLA_VALUE_DIM + GLA_GATE_RANK
MLP_HIDDEN = 4 * D_MODEL
NORM_EPS = 1e-6

kernel_name = 'hybrid_stickbreak_shortconv_gla_stream_step'


def rms_norm(x, gain):
    xf = x.astype(jnp.float32)
    y = xf * lax.rsqrt(jnp.mean(xf * xf, axis=-1, keepdims=True) + NORM_EPS)
    return (y * gain.astype(jnp.float32)).astype(x.dtype)


def sq_relu_mlp(x, w_up, w_down):
    h = jax.nn.relu(x @ w_up)
    return (h * h) @ w_down


def stick_breaking_block(q, k, v, q_pos, k_pos):
    z = jnp.einsum('bqhd,bkhd->bhqk', q.astype(jnp.float32), k.astype(jnp.float32)) * (SB_HEAD_DIM ** -0.5)
    mask = k_pos[None, :] < q_pos[:, None]
    log_not = jnp.where(mask, jax.nn.log_sigmoid(-z), 0.0)
    rev = lax.cumsum(log_not, axis=3, reverse=True)
    tail = jnp.pad(rev[..., 1:], ((0, 0), (0, 0), (0, 0), (0, 1)))
    a = jnp.where(mask, jnp.exp(jax.nn.log_sigmoid(z) + tail), 0.0)
    return jnp.einsum('bhqk,bkhd->bqhd', a, v.astype(jnp.float32))


def sb_project(h, w_qkv):
    b, t, _ = h.shape
    qkv = (h @ w_qkv).reshape(b, t, 3, SB_HEADS, SB_HEAD_DIM)
    return qkv[:, :, 0], qkv[:, :, 1], qkv[:, :, 2]


def sb_mix_prompt(h, w_qkv, w_o):
    b, t, _ = h.shape
    q, k, v = sb_project(h, w_qkv)
    nb = t // SB_Q_BLOCK
    q_blocks = q.reshape(b, nb, SB_Q_BLOCK, SB_HEADS, SB_HEAD_DIM).transpose(1, 0, 2, 3, 4)
    pos = jnp.arange(t, dtype=jnp.int32)
    pos_blocks = pos.reshape(nb, SB_Q_BLOCK)
    o = lax.map(lambda qp: stick_breaking_block(qp[0], k, v, qp[1], pos), (q_blocks, pos_blocks))
    o = o.transpose(1, 0, 2, 3, 4).reshape(b, t, D_MODEL).astype(h.dtype)
    return o @ w_o, k, v


def sb_mix_sample(h, cache_k, cache_v, w_qkv, w_o):
    b, t, _ = h.shape
    q, k, v = sb_project(h, w_qkv)
    past = cache_k.shape[1]
    k_all = jnp.concatenate([cache_k.astype(k.dtype), k], axis=1)
    v_all = jnp.concatenate([cache_v.astype(v.dtype), v], axis=1)
    q_pos = past + jnp.arange(t, dtype=jnp.int32)
    k_pos = jnp.arange(past + t, dtype=jnp.int32)
    o = stick_breaking_block(q, k_all, v_all, q_pos, k_pos).reshape(b, t, D_MODEL).astype(h.dtype)
    return o @ w_o, k, v


def short_conv_mix(h, conv_prefix, w_in, w_conv, w_out):
    gate_b, gate_c, xp = jnp.split(h @ w_in, 3, axis=-1)
    u = gate_c * xp
    u_ext = jnp.concatenate([conv_prefix.astype(u.dtype), u], axis=1)
    t = h.shape[1]
    y = w_conv[0] * u_ext[:, 0:t]
    for i in range(1, CONV_WIDTH):
        y = y + w_conv[i] * u_ext[:, i:i + t]
    return (gate_b * y) @ w_out, u_ext[:, -(CONV_WIDTH - 1):]


def gla_project(h, w_in, w_gk2, b_gk):
    b, t, _ = h.shape
    splits = [GLA_KEY_DIM, 2 * GLA_KEY_DIM, 2 * GLA_KEY_DIM + GLA_VALUE_DIM, 2 * GLA_KEY_DIM + 2 * GLA_VALUE_DIM]
    q, k, v, g_out, gk_low = jnp.split(h @ w_in, splits, axis=-1)
    gk = jax.nn.log_sigmoid((gk_low @ w_gk2 + b_gk).astype(jnp.float32)) / GLA_GATE_NORMALIZER

    def heads(a, dh):
        return a.reshape(b, t, GLA_HEADS, dh).transpose(0, 2, 1, 3).astype(jnp.float32)

    return heads(q, GLA_DK) * (GLA_DK ** -0.5), heads(k, GLA_DK), heads(v, GLA_DV), heads(gk, GLA_DK), g_out


def gla_chunk(state, q, k, v, g):
    L = q.shape[2]
    b = jnp.cumsum(g, axis=2)
    o_inter = jnp.einsum('bhld,bhde->bhle', q * jnp.exp(b), state)
    mask = jnp.tril(jnp.ones((L, L), dtype=bool))
    diff = b[:, :, :, None, :] - b[:, :, None, :, :]
    decay = jnp.exp(jnp.where(mask[:, :, None], diff, -jnp.inf))
    a = jnp.einsum('bhtd,bhsd,bhtsd->bhts', q, k, decay)
    o_intra = jnp.einsum('bhts,bhse->bhte', a, v)
    b_last = b[:, :, -1, :]
    new_state = jnp.exp(b_last)[..., None] * state + jnp.einsum('bhld,bhle->bhde', k * jnp.exp(b_last[:, :, None, :] - b), v)
    return new_state, o_inter + o_intra


def gla_output(o, g_out, norm_w, w_o, dtype):
    bsz, _, t, _ = o.shape
    o = o * lax.rsqrt(jnp.mean(o * o, axis=-1, keepdims=True) + NORM_EPS) * norm_w.astype(jnp.float32)
    o = o.transpose(0, 2, 1, 3).reshape(bsz, t, GLA_VALUE_DIM).astype(dtype)
    return (o * jax.nn.silu(g_out)) @ w_o


def gla_mix_prompt(h, w_in, w_gk2, b_gk, norm_w, w_o):
    bsz, t, _ = h.shape
    q, k, v, g, g_out = gla_project(h, w_in, w_gk2, b_gk)
    nc = t // CHUNK

    def to_chunks(a):
        return a.reshape(bsz, GLA_HEADS, nc, CHUNK, a.shape[-1]).transpose(2, 0, 1, 3, 4)

    def step(state, blk):
        return gla_chunk(state, *blk)

    s0 = jnp.zeros((bsz, GLA_HEADS, GLA_DK, GLA_DV), jnp.float32)
    s_final, o = lax.scan(step, s0, (to_chunks(q), to_chunks(k), to_chunks(v), to_chunks(g)))
    o = o.transpose(1, 2, 0, 3, 4).reshape(bsz, GLA_HEADS, t, GLA_DV)
    return gla_output(o, g_out, norm_w, w_o, h.dtype), s_final.astype(h.dtype)


def gla_mix_sample(h, state, w_in, w_gk2, b_gk, norm_w, w_o):
    q, k, v, g, g_out = gla_project(h, w_in, w_gk2, b_gk)
    s_new, o = gla_chunk(state.astype(jnp.float32), q, k, v, g)
    return gla_output(o, g_out, norm_w, w_o, h.dtype), s_new.astype(h.dtype)


def setup_inputs(seed: int = 0) -> dict:
    key = jax.random.key(seed)
    ks = jax.random.split(key, 24)
    f32 = jnp.float32

    def nrm(k, shape, scale=1.0):
        return jax.random.normal(k, shape, f32) * scale

    def w(k, shape, fan_in):
        return jax.random.normal(k, shape, f32) * (fan_in ** -0.5)

    def gain(k, shape):
        return 1.0 + 0.1 * jax.random.normal(k, shape, f32)

    return {
        'x_prompt': nrm(ks[0], (BATCH, SEQ, D_MODEL)),
        'x_sample': nrm(ks[1], (DEC_BATCH, DEC_SEQ, D_MODEL)),
        'cache_sb_k': nrm(ks[2], (N_SB, DEC_BATCH, PAST_LEN, SB_HEADS, SB_HEAD_DIM)),
        'cache_sb_v': nrm(ks[3], (N_SB, DEC_BATCH, PAST_LEN, SB_HEADS, SB_HEAD_DIM)),
        'state_conv': nrm(ks[4], (N_CONV, DEC_BATCH, CONV_WIDTH - 1, D_MODEL)),
        'state_gla': nrm(ks[5], (N_GLA, DEC_BATCH, GLA_HEADS, GLA_DK, GLA_DV), 0.5),
        'norm_mix': gain(ks[6], (DEPTH, D_MODEL)),
        'norm_mlp': gain(ks[7], (DEPTH, D_MODEL)),
        'sb_w_qkv': w(ks[8], (N_SB, D_MODEL, 3 * D_MODEL), D_MODEL),
        'sb_w_o': w(ks[9], (N_SB, D_MODEL, D_MODEL), D_MODEL),
        'conv_w_in': w(ks[10], (N_CONV, D_MODEL, 3 * D_MODEL), D_MODEL),
        'conv_w': w(ks[11], (N_CONV, CONV_WIDTH, D_MODEL), CONV_WIDTH),
        'conv_w_out': w(ks[12], (N_CONV, D_MODEL, D_MODEL), D_MODEL),
        'gla_w_in': w(ks[13], (N_GLA, D_MODEL, GLA_IN_DIM), D_MODEL),
        'gla_w_gk2': w(ks[14], (N_GLA, GLA_GATE_RANK, GLA_KEY_DIM), GLA_GATE_RANK),
        'gla_b_gk': nrm(ks[15], (N_GLA, GLA_KEY_DIM), 0.1),
        'gla_norm': gain(ks[16], (N_GLA, GLA_DV)),
        'gla_w_o': w(ks[17], (N_GLA, GLA_VALUE_DIM, D_MODEL), GLA_VALUE_DIM),
        'mlp_w_up': w(ks[18], (DEPTH, D_MODEL, MLP_HIDDEN), D_MODEL),
        'mlp_w_down': w(ks[19], (DEPTH, MLP_HIDDEN, D_MODEL), MLP_HIDDEN),
        'norm_final': gain(ks[20], (D_MODEL,)),
    }


def reference(x_prompt, x_sample, cache_sb_k, cache_sb_v, state_conv, state_gla,
              norm_mix, norm_mlp, sb_w_qkv, sb_w_o, conv_w_in, conv_w, conv_w_out,
              gla_w_in, gla_w_gk2, gla_b_gk, gla_norm, gla_w_o, mlp_w_up, mlp_w_down, norm_final):
    hp, hs = x_prompt, x_sample
    sb_k_p, sb_v_p, sb_k_s, sb_v_s = [], [], [], []
    conv_p, conv_s, gla_p, gla_s = [], [], [], []
    for i in range(DEPTH):
        j = i // N_MIXERS
        a_p = rms_norm(hp, norm_mix[i])
        a_s = rms_norm(hs, norm_mix[i])
        if i % N_MIXERS == 0:
            o_p, k_p, v_p = sb_mix_prompt(a_p, sb_w_qkv[j], sb_w_o[j])
            o_s, k_s, v_s = sb_mix_sample(a_s, cache_sb_k[j], cache_sb_v[j], sb_w_qkv[j], sb_w_o[j])
            sb_k_p.append(k_p); sb_v_p.append(v_p); sb_k_s.append(k_s); sb_v_s.append(v_s)
        elif i % N_MIXERS == 1:
            zero_prefix = jnp.zeros((a_p.shape[0], CONV_WIDTH - 1, D_MODEL), a_p.dtype)
            o_p, c_p = short_conv_mix(a_p, zero_prefix, conv_w_in[j], conv_w[j], conv_w_out[j])
            o_s, c_s = short_conv_mix(a_s, state_conv[j], conv_w_in[j], conv_w[j], conv_w_out[j])
            conv_p.append(c_p); conv_s.append(c_s)
        else:
            o_p, g_p = gla_mix_prompt(a_p, gla_w_in[j], gla_w_gk2[j], gla_b_gk[j], gla_norm[j], gla_w_o[j])
            o_s, g_s = gla_mix_sample(a_s, state_gla[j], gla_w_in[j], gla_w_gk2[j], gla_b_gk[j], gla_norm[j], gla_w_o[j])
            gla_p.append(g_p); gla_s.append(g_s)
        hp = hp + o_p
        hs = hs + o_s
        hp = hp + sq_relu_mlp(rms_norm(hp, norm_mlp[i]), mlp_w_up[i], mlp_w_down[i])
        hs = hs + sq_relu_mlp(rms_norm(hs, norm_mlp[i]), mlp_w_up[i], mlp_w_down[i])
    y_prompt = rms_norm(hp, norm_final)
    y_sample = rms_norm(hs, norm_final)
    return (y_prompt, y_sample,
            jnp.stack(sb_k_p), jnp.stack(sb_v_p), jnp.stack(sb_k_s), jnp.stack(sb_v_s),
            jnp.stack(conv_p), jnp.stack(conv_s), jnp.stack(gla_p), jnp.stack(gla_s))
```

```python
import functools

import jax
import jax.numpy as jnp
from jax import lax
from jax.experimental import pallas as pl
from jax.experimental.pallas import tpu as pltpu

F32 = jnp.float32
BF16 = jnp.bfloat16

D_MODEL = 1024
BATCH = 8
SEQ = 2048
DEPTH = 4
DEC_BATCH = 32
DEC_SEQ = 64
PAST_LEN = 1024
CHUNK = 64
N_MIXERS = 3
SB_HEADS = 16
SB_HEAD_DIM = D_MODEL // SB_HEADS
CONV_WIDTH = 3
GLA_HEADS = 4
GLA_KEY_DIM = D_MODEL // 2
GLA_VALUE_DIM = D_MODEL
GLA_DK = GLA_KEY_DIM // GLA_HEADS
GLA_DV = GLA_VALUE_DIM // GLA_HEADS
GLA_GATE_RANK = 16
GLA_GATE_NORMALIZER = 16.0
GLA_IN_DIM = 2 * GLA_KEY_DIM + 2 * GLA_VALUE_DIM + GLA_GATE_RANK
MLP_HIDDEN = 4 * D_MODEL
NORM_EPS = 1e-6

N_PROMPT = BATCH * SEQ
N_SAMPLE = DEC_BATCH * DEC_SEQ
N_TOK = N_PROMPT + N_SAMPLE

LANES = 128
GLA_IN_PAD = 25 * LANES
TOKEN_TILE = 512
COL_CHUNK = 512
SB_TILE = 256
SB_PAIR = 2 * SB_HEAD_DIM
GLA_SUB = 16
VMEM_LIMIT = 56 * 1024 * 1024


def _params(*sem):
    return pltpu.CompilerParams(dimension_semantics=sem, vmem_limit_bytes=VMEM_LIMIT)


def _rms(x, gain):
    ms = jnp.mean(x * x, axis=-1, keepdims=True)
    return x * lax.rsqrt(ms + NORM_EPS) * gain


def _softplus(z):
    return jnp.maximum(z, 0.0) + jnp.log1p(jnp.exp(-jnp.abs(z)))


def _split_bf16(x, passes):
    parts = []
    r = x
    for _ in range(passes):
        h = r.astype(BF16)
        parts.append(h)
        r = r - h.astype(F32)
    return parts


def _norm_proj_kernel(x_ref, g_ref, w_ref, o_ref):
    xn = _rms(x_ref[...], g_ref[...]).astype(BF16)
    dout = o_ref.shape[1]
    for lo in range(0, dout, COL_CHUNK):
        cols = slice(lo, min(lo + COL_CHUNK, dout))
        o_ref[:, cols] = jnp.dot(xn, w_ref[:, cols], preferred_element_type=F32)


def _norm_proj(x, gain, w):
    n, dout = x.shape[0], w.shape[1]
    return pl.pallas_call(
        _norm_proj_kernel,
        name="norm_proj",
        grid=(n // TOKEN_TILE,),
        in_specs=[
            pl.BlockSpec((TOKEN_TILE, D_MODEL), lambda i: (i, 0)),
            pl.BlockSpec((1, D_MODEL), lambda i: (0, 0)),
            pl.BlockSpec((D_MODEL, dout), lambda i: (0, 0)),
        ],
        out_specs=pl.BlockSpec((TOKEN_TILE, dout), lambda i: (i, 0)),
        out_shape=jax.ShapeDtypeStruct((n, dout), F32),
        compiler_params=_params("parallel"),
    )(x, gain.reshape(1, D_MODEL), w)


def _conv_in_kernel(x_ref, g_ref, w_ref, gb_ref, u_ref):
    xn = _rms(x_ref[...], g_ref[...]).astype(BF16)
    for j in range(D_MODEL // COL_CHUNK):
        cols = slice(j * COL_CHUNK, (j + 1) * COL_CHUNK)
        cols_c = slice(D_MODEL + j * COL_CHUNK, D_MODEL + (j + 1) * COL_CHUNK)
        cols_x = slice(2 * D_MODEL + j * COL_CHUNK, 2 * D_MODEL + (j + 1) * COL_CHUNK)
        gb_ref[:, cols] = jnp.dot(xn, w_ref[:, cols], preferred_element_type=F32)
        gc = jnp.dot(xn, w_ref[:, cols_c], preferred_element_type=F32)
        xp = jnp.dot(xn, w_ref[:, cols_x], preferred_element_type=F32)
        u_ref[:, cols] = gc * xp


def _conv_in(x, gain, w):
    n = x.shape[0]
    tok = pl.BlockSpec((TOKEN_TILE, D_MODEL), lambda i: (i, 0))
    return pl.pallas_call(
        _conv_in_kernel,
        name="conv_in",
        grid=(n // TOKEN_TILE,),
        in_specs=[tok, pl.BlockSpec((1, D_MODEL), lambda i: (0, 0)),
                  pl.BlockSpec((D_MODEL, 3 * D_MODEL), lambda i: (0, 0))],
        out_specs=[tok, tok],
        out_shape=[jax.ShapeDtypeStruct((n, D_MODEL), F32)] * 2,
        compiler_params=_params("parallel"),
    )(x, gain.reshape(1, D_MODEL), w)


def _proj_res_kernel(a_ref, w_ref, r_ref, o_ref):
    o_ref[...] = r_ref[...] + jnp.dot(a_ref[...], w_ref[...], preferred_element_type=F32)


def _proj_res(a, w, res):
    n = a.shape[0]
    tok = pl.BlockSpec((TOKEN_TILE, D_MODEL), lambda i: (i, 0))
    return pl.pallas_call(
        _proj_res_kernel,
        name="proj_res",
        grid=(n // TOKEN_TILE,),
        in_specs=[tok, pl.BlockSpec((D_MODEL, D_MODEL), lambda i: (0, 0)), tok],
        out_specs=tok,
        out_shape=jax.ShapeDtypeStruct((n, D_MODEL), F32),
        compiler_params=_params("parallel"),
    )(a, w, res)


def _conv_out_kernel(u_ref, prev_ref, gb_ref, cw_ref, w_ref, r_ref, o_ref):
    units = TOKEN_TILE // CHUNK
    u = u_ref[...]
    prev = prev_ref[...]
    p0 = jnp.broadcast_to(prev[:, 0:1, :], (units, CHUNK, D_MODEL)).reshape(TOKEN_TILE, D_MODEL)
    p1 = jnp.broadcast_to(prev[:, 1:2, :], (units, CHUNK, D_MODEL)).reshape(TOKEN_TILE, D_MODEL)
    t = lax.broadcasted_iota(jnp.int32, (TOKEN_TILE, 1), 0) % CHUNK
    s1 = jnp.where(t == 0, p1, pltpu.roll(u, 1, axis=0))
    s2 = jnp.where(t == 0, p0, jnp.where(t == 1, p1, pltpu.roll(u, 2, axis=0)))
    cw = cw_ref[...]
    y = cw[0:1, :] * s2 + cw[1:2, :] * s1 + cw[2:3, :] * u
    a = (gb_ref[...] * y).astype(BF16)
    o_ref[...] = r_ref[...] + jnp.dot(a, w_ref[...], preferred_element_type=F32)


def _conv_out(u, prev, gb, cw, w, res):
    n = u.shape[0]
    units = TOKEN_TILE // CHUNK
    tok = pl.BlockSpec((TOKEN_TILE, D_MODEL), lambda i: (i, 0))
    return pl.pallas_call(
        _conv_out_kernel,
        name="conv_out",
        grid=(n // TOKEN_TILE,),
        in_specs=[tok,
                  pl.BlockSpec((units, CONV_WIDTH - 1, D_MODEL), lambda i: (i, 0, 0)),
                  tok,
                  pl.BlockSpec((CONV_WIDTH, D_MODEL), lambda i: (0, 0)),
                  pl.BlockSpec((D_MODEL, D_MODEL), lambda i: (0, 0)),
                  tok],
        out_specs=tok,
        out_shape=jax.ShapeDtypeStruct((n, D_MODEL), F32),
        compiler_params=_params("parallel"),
    )(u, prev, gb, cw, w, res)


def _mlp_kernel(x_ref, g_ref, wu_ref, wd_ref, gf_ref, o_ref, *, final_norm):
    x = x_ref[...]
    xn = _rms(x, g_ref[...]).astype(BF16)
    acc = x
    for c in range(MLP_HIDDEN // COL_CHUNK):
        cols = slice(c * COL_CHUNK, (c + 1) * COL_CHUNK)
        h = jnp.maximum(jnp.dot(xn, wu_ref[:, cols], preferred_element_type=F32), 0.0)
        acc = acc + jnp.dot((h * h).astype(BF16), wd_ref[cols, :], preferred_element_type=F32)
    o_ref[...] = _rms(acc, gf_ref[...]) if final_norm else acc


def _mlp(x, gain, wu, wd, gain_final, final_norm):
    n = x.shape[0]
    tok = pl.BlockSpec((TOKEN_TILE, D_MODEL), lambda i: (i, 0))
    vec = pl.BlockSpec((1, D_MODEL), lambda i: (0, 0))
    return pl.pallas_call(
        functools.partial(_mlp_kernel, final_norm=final_norm),
        name="mlp",
        grid=(n // TOKEN_TILE,),
        in_specs=[tok, vec,
                  pl.BlockSpec((D_MODEL, MLP_HIDDEN), lambda i: (0, 0)),
                  pl.BlockSpec((MLP_HIDDEN, D_MODEL), lambda i: (0, 0)),
                  vec],
        out_specs=tok,
        out_shape=jax.ShapeDtypeStruct((n, D_MODEL), F32),
        compiler_params=_params("parallel"),
    )(x, gain.reshape(1, D_MODEL), wu, wd, gain_final.reshape(1, D_MODEL))


def _sb_tile(qh, kb, vb, strict_upper, mask, carry, acc):
    z = lax.dot_general(qh, kb, (((1,), (1,)), ((), ())), preferred_element_type=F32)
    sp = _softplus(z)
    log_not = -sp
    if mask is not None:
        log_not = jnp.where(mask, log_not, 0.0)
    tail = carry
    for part in _split_bf16(log_not, 2):
        tail = tail + jnp.dot(part, strict_upper, preferred_element_type=F32)
    a = jnp.exp(z - sp + tail)
    if mask is not None:
        a = jnp.where(mask, a, 0.0)
    acc = acc + jnp.dot(a.astype(BF16), vb, preferred_element_type=F32)
    carry = tail[:, 0:1] + log_not[:, 0:1]
    return carry, acc


def _suffix_matrix(n):
    s = lax.broadcasted_iota(jnp.int32, (n, n), 0)
    j = lax.broadcasted_iota(jnp.int32, (n, n), 1)
    return (s > j).astype(BF16)


def _head_lane_mask(hh):
    lane = lax.broadcasted_iota(jnp.int32, (1, SB_PAIR), 1)
    return (lane >= hh * SB_HEAD_DIM) & (lane < (hh + 1) * SB_HEAD_DIM)


def _sb_prompt_kernel(q_ref, k_ref, v_ref, o_ref):
    i = pl.program_id(2)
    row = lax.broadcasted_iota(jnp.int32, (SB_TILE, SB_TILE), 0)
    col = lax.broadcasted_iota(jnp.int32, (SB_TILE, SB_TILE), 1)
    causal = col < row
    upper = _suffix_matrix(SB_TILE)
    q = q_ref[...] * (SB_HEAD_DIM ** -0.5)
    zero_c = jnp.zeros((SB_TILE, 1), F32)
    zero_a = jnp.zeros((SB_TILE, SB_PAIR), F32)
    outs = []
    for hh in range(2):
        qh = jnp.where(_head_lane_mask(hh), q, 0.0).astype(BF16)

        def load(j):
            rows = pl.ds(pl.multiple_of(j * SB_TILE, SB_TILE), SB_TILE)
            return k_ref[rows, :].astype(BF16), v_ref[rows, :].astype(BF16)

        kb, vb = load(i)
        carry, acc = _sb_tile(qh, kb, vb, upper, causal, zero_c, zero_a)

        def body(jj, c, qh=qh):
            kb, vb = load(i - 1 - jj)
            return _sb_tile(qh, kb, vb, upper, None, c[0], c[1])

        carry, acc = lax.fori_loop(0, i, body, (carry, acc))
        outs.append(acc)
    o_ref[...] = jnp.where(_head_lane_mask(0), outs[0], outs[1]).astype(o_ref.dtype)


def _sb_prompt(qkv):
    nq = SEQ // SB_TILE
    pairs = D_MODEL // SB_PAIR
    return pl.pallas_call(
        _sb_prompt_kernel,
        name="sb_prompt",
        grid=(BATCH, pairs, nq),
        in_specs=[
            pl.BlockSpec((SB_TILE, SB_PAIR), lambda b, p, i: (b * nq + i, p)),
            pl.BlockSpec((SEQ, SB_PAIR), lambda b, p, i: (b, pairs + p)),
            pl.BlockSpec((SEQ, SB_PAIR), lambda b, p, i: (b, 2 * pairs + p)),
        ],
        out_specs=pl.BlockSpec((SB_TILE, SB_PAIR), lambda b, p, i: (b * nq + i, p)),
        out_shape=jax.ShapeDtypeStruct((N_PROMPT, D_MODEL), BF16),
        compiler_params=_params("parallel", "parallel", "arbitrary"),
    )(qkv, qkv, qkv)


def _sb_sample_kernel(q_ref, kn_ref, vn_ref, ck_ref, cv_ref, o_ref):
    row = lax.broadcasted_iota(jnp.int32, (DEC_SEQ, DEC_SEQ), 0)
    col = lax.broadcasted_iota(jnp.int32, (DEC_SEQ, DEC_SEQ), 1)
    causal = col < row
    upper_new = _suffix_matrix(DEC_SEQ)
    upper = _suffix_matrix(SB_TILE)
    q = q_ref[...] * (SB_HEAD_DIM ** -0.5)
    kn = kn_ref[...].astype(BF16)
    vn = vn_ref[...].astype(BF16)
    outs = []
    for hh in range(2):
        qh = jnp.where(_head_lane_mask(hh), q, 0.0).astype(BF16)
        carry = jnp.zeros((DEC_SEQ, 1), F32)
        acc = jnp.zeros((DEC_SEQ, SB_PAIR), F32)
        carry, acc = _sb_tile(qh, kn, vn, upper_new, causal, carry, acc)
        for j in reversed(range(PAST_LEN // SB_TILE)):
            rows = slice(j * SB_TILE, (j + 1) * SB_TILE)
            carry, acc = _sb_tile(qh, ck_ref[rows, :].astype(BF16), cv_ref[rows, :].astype(BF16),
                                  upper, None, carry, acc)
        outs.append(acc)
    o_ref[...] = jnp.where(_head_lane_mask(0), outs[0], outs[1]).astype(o_ref.dtype)


def _sb_sample(qkv, cache_k, cache_v, layer):
    pairs = D_MODEL // SB_PAIR
    row0 = N_PROMPT // DEC_SEQ
    cache_spec = pl.BlockSpec((None, None, PAST_LEN, SB_PAIR), lambda b, p: (layer, b, 0, p))
    return pl.pallas_call(
        _sb_sample_kernel,
        name="sb_sample",
        grid=(DEC_BATCH, pairs),
        in_specs=[
            pl.BlockSpec((DEC_SEQ, SB_PAIR), lambda b, p: (row0 + b, p)),
            pl.BlockSpec((DEC_SEQ, SB_PAIR), lambda b, p: (row0 + b, pairs + p)),
            pl.BlockSpec((DEC_SEQ, SB_PAIR), lambda b, p: (row0 + b, 2 * pairs + p)),
            cache_spec, cache_spec,
        ],
        out_specs=pl.BlockSpec((DEC_SEQ, SB_PAIR), lambda b, p: (b, p)),
        out_shape=jax.ShapeDtypeStruct((N_SAMPLE, D_MODEL), BF16),
        compiler_params=_params("parallel", "parallel"),
    )(qkv, qkv, qkv, cache_k, cache_v)


def _gla_kernel(q_ref, k_ref, v_ref, go_ref, gl_ref, s0_ref, wgk_ref, bgk_ref, nw_ref,
                o_ref, sout_ref, st_ref):
    c = pl.program_id(2)

    @pl.when(c == 0)
    def _():
        st_ref[...] = s0_ref[...].T

    q = q_ref[...] * (GLA_DK ** -0.5)
    k = k_ref[...]
    v = v_ref[...]
    gate_in = jnp.dot(gl_ref[...].astype(BF16), wgk_ref[...].astype(BF16),
                      preferred_element_type=F32) + bgk_ref[...]
    g = (jnp.minimum(gate_in, 0.0) - jnp.log1p(jnp.exp(-jnp.abs(gate_in)))) / GLA_GATE_NORMALIZER

    t_i = lax.broadcasted_iota(jnp.int32, (CHUNK, CHUNK), 0)
    s_i = lax.broadcasted_iota(jnp.int32, (CHUNK, CHUNK), 1)
    lower = (s_i <= t_i).astype(BF16)
    b = jnp.zeros((CHUNK, GLA_DK), F32)
    for part in _split_bf16(g, 3):
        b = b + jnp.dot(lower, part, preferred_element_type=F32)

    st = st_ref[...]
    o = lax.dot_general((q * jnp.exp(b)).astype(BF16), st.astype(BF16),
                        (((1,), (1,)), ((), ())), preferred_element_type=F32)

    o_rows = []
    sub_row = lax.broadcasted_iota(jnp.int32, (GLA_SUB, 1), 0)
    for blk in range(CHUNK // GLA_SUB):
        lo = blk * GLA_SUB
        rows = slice(lo, lo + GLA_SUB)
        qi, ki, vi, bi = q[rows], k[rows], v[rows], b[rows]
        oi = o[rows]
        if blk > 0:
            ref = b[lo - 1:lo]
            qt = (qi * jnp.exp(bi - ref)).astype(BF16)
            kt = (k[0:lo] * jnp.exp(ref - b[0:lo])).astype(BF16)
            a = lax.dot_general(qt, kt, (((1,), (1,)), ((), ())), preferred_element_type=F32)
            oi = oi + jnp.dot(a.astype(BF16), v[0:lo].astype(BF16), preferred_element_type=F32)
        for s in range(GLA_SUB):
            diff = jnp.where(sub_row >= s, bi - bi[s:s + 1], -jnp.inf)
            a_s = jnp.sum(qi * ki[s:s + 1] * jnp.exp(diff), axis=-1, keepdims=True)
            oi = oi + a_s * vi[s:s + 1]
        o_rows.append(oi)
    o = jnp.concatenate(o_rows, axis=0)

    b_last = b[CHUNK - 1:CHUNK]
    kd = (k * jnp.exp(b_last - b)).astype(BF16)
    st_new = st * jnp.exp(b_last) + lax.dot_general(
        v.astype(BF16), kd, (((0,), (0,)), ((), ())), preferred_element_type=F32)
    st_ref[...] = st_new

    @pl.when(c == pl.num_programs(2) - 1)
    def _():
        sout_ref[...] = st_new.T

    o = o * lax.rsqrt(jnp.mean(o * o, axis=-1, keepdims=True) + NORM_EPS) * nw_ref[...]
    go = go_ref[...]
    o_ref[...] = (o * (go * jax.nn.sigmoid(go))).astype(o_ref.dtype)


def _gla(proj, state, w_gk2, b_gk, norm_w, row0, nseq, nchunks):
    def rb(b, h, c):
        return row0 + b * nchunks + c

    kd_blocks = GLA_KEY_DIM // GLA_DK
    state_spec = pl.BlockSpec((None, None, GLA_DK, GLA_DV), lambda b, h, c: (b, h, 0, 0))
    return pl.pallas_call(
        _gla_kernel,
        name="gla",
        grid=(nseq, GLA_HEADS, nchunks),
        in_specs=[
            pl.BlockSpec((CHUNK, GLA_DK), lambda b, h, c: (rb(b, h, c), h)),
            pl.BlockSpec((CHUNK, GLA_DK), lambda b, h, c: (rb(b, h, c), kd_blocks + h)),
            pl.BlockSpec((CHUNK, GLA_DV), lambda b, h, c: (rb(b, h, c), GLA_HEADS + h)),
            pl.BlockSpec((CHUNK, GLA_DV), lambda b, h, c: (rb(b, h, c), 2 * GLA_HEADS + h)),
            pl.BlockSpec((CHUNK, LANES), lambda b, h, c: (rb(b, h, c), GLA_IN_PAD // LANES - 1)),
            state_spec,
            pl.BlockSpec((LANES, GLA_DK), lambda b, h, c: (0, h)),
            pl.BlockSpec((1, GLA_DK), lambda b, h, c: (0, h)),
            pl.BlockSpec((1, GLA_DV), lambda b, h, c: (0, 0)),
        ],
        out_specs=[
            pl.BlockSpec((CHUNK, GLA_DV), lambda b, h, c: (b * nchunks + c, h)),
            state_spec,
        ],
        out_shape=[
            jax.ShapeDtypeStruct((nseq * nchunks * CHUNK, GLA_VALUE_DIM), BF16),
            jax.ShapeDtypeStruct((nseq, GLA_HEADS, GLA_DK, GLA_DV), F32),
        ],
        scratch_shapes=[pltpu.VMEM((GLA_DV, GLA_DK), F32)],
        compiler_params=_params("parallel", "parallel", "arbitrary"),
    )(proj, proj, proj, proj, proj, state,
      jnp.pad(w_gk2, ((0, LANES - GLA_GATE_RANK), (0, 0))), b_gk.reshape(1, GLA_KEY_DIM),
      norm_w.reshape(1, GLA_DV))


def kernel(x_prompt, x_sample, cache_sb_k, cache_sb_v, state_conv, state_gla, norm_mix, norm_mlp,
           sb_w_qkv, sb_w_o, conv_w_in, conv_w, conv_w_out, gla_w_in, gla_w_gk2, gla_b_gk,
           gla_norm, gla_w_o, mlp_w_up, mlp_w_down, norm_final):
    x = jnp.concatenate([x_prompt.reshape(N_PROMPT, D_MODEL), x_sample.reshape(N_SAMPLE, D_MODEL)])
    n_sb = cache_sb_k.shape[0]
    cache_k = cache_sb_k.reshape(n_sb, DEC_BATCH, PAST_LEN, D_MODEL)
    cache_v = cache_sb_v.reshape(n_sb, DEC_BATCH, PAST_LEN, D_MODEL)
    sb_k, sb_v, conv_st, gla_st_p, gla_st_s = [], [], [], [], []

    for i in range(DEPTH):
        j = i // N_MIXERS
        if i % N_MIXERS == 0:
            qkv = _norm_proj(x, norm_mix[i], sb_w_qkv[j].astype(BF16))
            o = jnp.concatenate([_sb_prompt(qkv), _sb_sample(qkv, cache_k, cache_v, j)])
            x = _proj_res(o, sb_w_o[j].astype(BF16), x)
            sb_k.append(qkv[:, D_MODEL:2 * D_MODEL])
            sb_v.append(qkv[:, 2 * D_MODEL:])
        elif i % N_MIXERS == 1:
            gb, u = _conv_in(x, norm_mix[i], conv_w_in[j].astype(BF16))
            tails = u.reshape(N_TOK // CHUNK, CHUNK, D_MODEL)[:, CHUNK - (CONV_WIDTH - 1):]
            p_units = SEQ // CHUNK
            tails_p = tails[:N_PROMPT // CHUNK].reshape(BATCH, p_units, CONV_WIDTH - 1, D_MODEL)
            prev_p = jnp.concatenate(
                [jnp.zeros((BATCH, 1, CONV_WIDTH - 1, D_MODEL), F32), tails_p[:, :-1]], axis=1)
            prev = jnp.concatenate(
                [prev_p.reshape(N_PROMPT // CHUNK, CONV_WIDTH - 1, D_MODEL), state_conv[j]])
            x = _conv_out(u, prev, gb, conv_w[j], conv_w_out[j].astype(BF16), x)
            conv_st.append((tails_p[:, -1], tails[N_PROMPT // CHUNK:]))
        else:
            w_in = jnp.pad(gla_w_in[j], ((0, 0), (0, GLA_IN_PAD - GLA_IN_DIM))).astype(BF16)
            proj = _norm_proj(x, norm_mix[i], w_in)
            zero_state = jnp.zeros((BATCH, GLA_HEADS, GLA_DK, GLA_DV), F32)
            o_p, s_p = _gla(proj, zero_state, gla_w_gk2[j], gla_b_gk[j], gla_norm[j],
                            0, BATCH, SEQ // CHUNK)
            o_s, s_s = _gla(proj, state_gla[j], gla_w_gk2[j], gla_b_gk[j], gla_norm[j],
                            N_PROMPT // CHUNK, DEC_BATCH, DEC_SEQ // CHUNK)
            x = _proj_res(jnp.concatenate([o_p, o_s]), gla_w_o[j].astype(BF16), x)
            gla_st_p.append(s_p)
            gla_st_s.append(s_s)
        x = _mlp(x, norm_mlp[i], mlp_w_up[i].astype(BF16), mlp_w_down[i].astype(BF16),
                 norm_final, final_norm=(i == DEPTH - 1))

    def heads_p(a):
        return a[:N_PROMPT].reshape(BATCH, SEQ, SB_HEADS, SB_HEAD_DIM)

    def heads_s(a):
        return a[N_PROMPT:].reshape(DEC_BATCH, DEC_SEQ, SB_HEADS, SB_HEAD_DIM)

    return (x[:N_PROMPT].reshape(BATCH, SEQ, D_MODEL),
            x[N_PROMPT:].reshape(DEC_BATCH, DEC_SEQ, D_MODEL),
            jnp.stack([heads_p(a) for a in sb_k]), jnp.stack([heads_p(a) for a in sb_v]),
            jnp.stack([heads_s(a) for a in sb_k]), jnp.stack([heads_s(a) for a in sb_v]),
            jnp.stack([c[0] for c in conv_st]), jnp.stack([c[1] for c in conv_st]),
            jnp.stack(gla_st_p), jnp.stack(gla_st_s))
```

```python
import functools

import jax
import jax.numpy as jnp
from jax import lax
from jax.experimental import pallas as pl
from jax.experimental.pallas import tpu as pltpu

F32 = jnp.float32
BF16 = jnp.bfloat16

D_MODEL = 1024
BATCH = 8
SEQ = 2048
DEPTH = 4
DEC_BATCH = 32
DEC_SEQ = 64
PAST_LEN = 1024
CHUNK = 64
N_MIXERS = 3
SB_HEADS = 16
SB_HEAD_DIM = D_MODEL // SB_HEADS
CONV_WIDTH = 3
GLA_HEADS = 4
GLA_KEY_DIM = D_MODEL // 2
GLA_VALUE_DIM = D_MODEL
GLA_DK = GLA_KEY_DIM // GLA_HEADS
GLA_DV = GLA_VALUE_DIM // GLA_HEADS
GLA_GATE_RANK = 16
GLA_GATE_NORMALIZER = 16.0
GLA_IN_DIM = 2 * GLA_KEY_DIM + 2 * GLA_VALUE_DIM + GLA_GATE_RANK
MLP_HIDDEN = 4 * D_MODEL
NORM_EPS = 1e-6

N_PROMPT = BATCH * SEQ
N_SAMPLE = DEC_BATCH * DEC_SEQ

LANES = 128
GLA_IN_PAD = 25 * LANES
TOKEN_TILE = 512
P_TILES = N_PROMPT // TOKEN_TILE
S_TILES = N_SAMPLE // TOKEN_TILE
COL_CHUNK = 512
SB_TILE = 256
SB_PAIR = 2 * SB_HEAD_DIM
SB_PAIRS = D_MODEL // SB_PAIR
GLA_SUB = 16
VMEM_LIMIT = 56 * 1024 * 1024
LOG2E = 1.4426950408889634


def _params(*sem):
    return pltpu.CompilerParams(dimension_semantics=sem, vmem_limit_bytes=VMEM_LIMIT)


def _rms(x, gain):
    ms = jnp.mean(x * x, axis=-1, keepdims=True)
    return x * lax.rsqrt(ms + NORM_EPS) * gain


def _split_bf16(x, passes):
    parts = []
    r = x
    for _ in range(passes):
        h = r.astype(BF16)
        parts.append(h)
        r = r - h.astype(F32)
    return parts


def _resident(arr):
    nd = arr.ndim
    return pl.BlockSpec(arr.shape, lambda *_: (0,) * nd, pipeline_mode=pl.Buffered(1))


def _two_stream(body, name, ins_p, ins_s, consts, outs):
    n_in, n_c, n_out = len(ins_p), len(consts), len(outs)

    def kern(*refs):
        p_in, s_in = refs[:n_in], refs[n_in:2 * n_in]
        c = refs[2 * n_in:2 * n_in + n_c]
        p_out = refs[2 * n_in + n_c:2 * n_in + n_c + n_out]
        s_out = refs[2 * n_in + n_c + n_out:]
        i = pl.program_id(0)

        @pl.when(i < P_TILES)
        def _():
            body(p_in, c, p_out)

        @pl.when(i >= P_TILES)
        def _():
            body(s_in, c, s_out)

    def p_idx(i):
        return jnp.minimum(i, P_TILES - 1)

    def s_idx(i):
        return jnp.maximum(i - P_TILES, 0)

    def spec(arr, tiles, idx):
        block = (arr.shape[0] // tiles,) + arr.shape[1:]
        zeros = (0,) * (arr.ndim - 1)
        return pl.BlockSpec(block, lambda i: (idx(i),) + zeros)

    def out_spec(width, idx):
        return pl.BlockSpec((TOKEN_TILE, width), lambda i: (idx(i), 0))

    return pl.pallas_call(
        kern,
        name=name,
        grid=(P_TILES + S_TILES,),
        in_specs=([spec(a, P_TILES, p_idx) for a in ins_p] + [spec(a, S_TILES, s_idx) for a in ins_s]
                  + [_resident(a) for a in consts]),
        out_specs=([out_spec(w, p_idx) for w, _ in outs] + [out_spec(w, s_idx) for w, _ in outs]),
        out_shape=([jax.ShapeDtypeStruct((N_PROMPT, w), dt) for w, dt in outs]
                   + [jax.ShapeDtypeStruct((N_SAMPLE, w), dt) for w, dt in outs]),
        compiler_params=_params("arbitrary"),
    )(*ins_p, *ins_s, *consts)


def _norm_proj_body(ins, consts, outs):
    (x_ref,), (g_ref, w_ref), (o_ref,) = ins, consts, outs
    xn = _rms(x_ref[...], g_ref[...]).astype(BF16)
    dout = o_ref.shape[1]
    for lo in range(0, dout, COL_CHUNK):
        cols = slice(lo, min(lo + COL_CHUNK, dout))
        o_ref[:, cols] = jnp.dot(xn, w_ref[:, cols], preferred_element_type=F32)


def _conv_in_body(ins, consts, outs):
    (x_ref,), (g_ref, w_ref), (gb_ref, u_ref) = ins, consts, outs
    xn = _rms(x_ref[...], g_ref[...]).astype(BF16)
    for lo in range(0, D_MODEL, COL_CHUNK):
        cols = slice(lo, lo + COL_CHUNK)
        cols_c = slice(D_MODEL + lo, D_MODEL + lo + COL_CHUNK)
        cols_x = slice(2 * D_MODEL + lo, 2 * D_MODEL + lo + COL_CHUNK)
        gb_ref[:, cols] = jnp.dot(xn, w_ref[:, cols], preferred_element_type=F32)
        gc = jnp.dot(xn, w_ref[:, cols_c], preferred_element_type=F32)
        xp = jnp.dot(xn, w_ref[:, cols_x], preferred_element_type=F32)
        u_ref[:, cols] = gc * xp


def _proj_res_body(ins, consts, outs):
    (a_ref, r_ref), (w_ref,), (o_ref,) = ins, consts, outs
    o_ref[...] = r_ref[...] + jnp.dot(a_ref[...], w_ref[...], preferred_element_type=F32)


def _conv_out_body(ins, consts, outs):
    (u_ref, prev_ref, gb_ref, r_ref), (cw_ref, w_ref), (o_ref,) = ins, consts, outs
    units = TOKEN_TILE // CHUNK
    u = u_ref[...]
    prev = prev_ref[...]
    p0 = jnp.broadcast_to(prev[:, 0:1, :], (units, CHUNK, D_MODEL)).reshape(TOKEN_TILE, D_MODEL)
    p1 = jnp.broadcast_to(prev[:, 1:2, :], (units, CHUNK, D_MODEL)).reshape(TOKEN_TILE, D_MODEL)
    t = lax.broadcasted_iota(jnp.int32, (TOKEN_TILE, 1), 0) % CHUNK
    s1 = jnp.where(t == 0, p1, pltpu.roll(u, 1, axis=0))
    s2 = jnp.where(t == 0, p0, jnp.where(t == 1, p1, pltpu.roll(u, 2, axis=0)))
    cw = cw_ref[...]
    y = cw[0:1, :] * s2 + cw[1:2, :] * s1 + cw[2:3, :] * u
    a = (gb_ref[...] * y).astype(BF16)
    o_ref[...] = r_ref[...] + jnp.dot(a, w_ref[...], preferred_element_type=F32)


def _mlp_body(ins, consts, outs, *, final_norm):
    (x_ref,), (g_ref, wu_ref, wd_ref, gf_ref), (o_ref,) = ins, consts, outs
    x = x_ref[...]
    xn = _rms(x, g_ref[...]).astype(BF16)
    acc = x
    for lo in range(0, MLP_HIDDEN, COL_CHUNK):
        cols = slice(lo, lo + COL_CHUNK)
        h = jnp.maximum(jnp.dot(xn, wu_ref[:, cols], preferred_element_type=F32), 0.0)
        acc = acc + jnp.dot((h * h).astype(BF16), wd_ref[cols, :], preferred_element_type=F32)
    o_ref[...] = _rms(acc, gf_ref[...]) if final_norm else acc


def _sb_qkv_prompt_kernel(x_ref, g_ref, wq_ref, wkvt_ref, q_ref, kt_ref, vt_ref, ktb_ref, vtb_ref):
    xn = _rms(x_ref[...], g_ref[...]).astype(BF16)
    for lo in range(0, D_MODEL, COL_CHUNK):
        cols = slice(lo, lo + COL_CHUNK)
        q_ref[:, cols] = jnp.dot(xn, wq_ref[:, cols], preferred_element_type=F32).astype(BF16)
    for out_ref, outb_ref, base in ((kt_ref, ktb_ref, 0), (vt_ref, vtb_ref, D_MODEL)):
        for lo in range(0, D_MODEL, COL_CHUNK):
            rows = slice(lo, lo + COL_CHUNK)
            w_rows = slice(base + lo, base + lo + COL_CHUNK)
            t = lax.dot_general(wkvt_ref[w_rows, :], xn, (((1,), (1,)), ((), ())),
                                preferred_element_type=F32)
            out_ref[rows, :] = t
            for kt in range(TOKEN_TILE // SB_TILE):
                outb_ref[kt, rows, :] = t[:, kt * SB_TILE:(kt + 1) * SB_TILE].astype(BF16)


def _sb_qkv_prompt(x, gain, wq, wkvt):
    per_seq = SEQ // TOKEN_TILE
    key_tiles = TOKEN_TILE // SB_TILE
    kt_spec = pl.BlockSpec((None, D_MODEL, TOKEN_TILE), lambda i: (i // per_seq, 0, i % per_seq))
    ktb_spec = pl.BlockSpec((None, key_tiles, D_MODEL, SB_TILE),
                            lambda i: (i // per_seq, i % per_seq, 0, 0))
    kt_shape = jax.ShapeDtypeStruct((BATCH, D_MODEL, SEQ), F32)
    ktb_shape = jax.ShapeDtypeStruct((BATCH, SEQ // SB_TILE, D_MODEL, SB_TILE), BF16)
    gain = gain.reshape(1, D_MODEL)
    return pl.pallas_call(
        _sb_qkv_prompt_kernel,
        name="sb_qkv_prompt",
        grid=(P_TILES,),
        in_specs=[pl.BlockSpec((TOKEN_TILE, D_MODEL), lambda i: (i, 0)),
                  _resident(gain), _resident(wq), _resident(wkvt)],
        out_specs=[pl.BlockSpec((TOKEN_TILE, D_MODEL), lambda i: (i, 0)),
                   kt_spec, kt_spec, ktb_spec, ktb_spec],
        out_shape=[jax.ShapeDtypeStruct((N_PROMPT, D_MODEL), BF16),
                   kt_shape, kt_shape, ktb_shape, ktb_shape],
        compiler_params=_params("parallel"),
    )(x, gain, wq, wkvt)


def _sb_qkv_sample_kernel(x_ref, g_ref, wq_ref, wkv_ref, q_ref, k_ref, v_ref, kvb_ref):
    xn = _rms(x_ref[...], g_ref[...]).astype(BF16)
    for lo in range(0, D_MODEL, COL_CHUNK):
        cols = slice(lo, lo + COL_CHUNK)
        q_ref[:, cols] = jnp.dot(xn, wq_ref[:, cols], preferred_element_type=F32).astype(BF16)
    for out_ref, base in ((k_ref, 0), (v_ref, D_MODEL)):
        for lo in range(0, D_MODEL, COL_CHUNK):
            w_cols = slice(base + lo, base + lo + COL_CHUNK)
            t = jnp.dot(xn, wkv_ref[:, w_cols], preferred_element_type=F32)
            out_ref[:, lo:lo + COL_CHUNK] = t
            kvb_ref[:, w_cols] = t.astype(BF16)


def _sb_qkv_sample(x, gain, wq, wkv):
    tok = pl.BlockSpec((TOKEN_TILE, D_MODEL), lambda i: (i, 0))
    gain = gain.reshape(1, D_MODEL)
    return pl.pallas_call(
        _sb_qkv_sample_kernel,
        name="sb_qkv_sample",
        grid=(S_TILES,),
        in_specs=[tok, _resident(gain), _resident(wq), _resident(wkv)],
        out_specs=[tok, tok, tok, pl.BlockSpec((TOKEN_TILE, 2 * D_MODEL), lambda i: (i, 0))],
        out_shape=[jax.ShapeDtypeStruct((N_SAMPLE, D_MODEL), BF16),
                   jax.ShapeDtypeStruct((N_SAMPLE, D_MODEL), F32),
                   jax.ShapeDtypeStruct((N_SAMPLE, D_MODEL), F32),
                   jax.ShapeDtypeStruct((N_SAMPLE, 2 * D_MODEL), BF16)],
        compiler_params=_params("parallel"),
    )(x, gain, wq, wkv)


def _sb_tile(qst, k, v, upper2, mask, carry, acc, key_minor):
    nt = (((1,), (1,)), ((), ()))
    if key_minor:
        z = jnp.dot(qst, k, preferred_element_type=F32)
    else:
        z = lax.dot_general(qst, k, nt, preferred_element_type=F32)
    z2 = z * LOG2E
    sp = jnp.maximum(z2, 0.0) + jnp.log2(1.0 + jnp.exp2(-jnp.abs(z2)))
    spm = sp if mask is None else jnp.where(mask, sp, 0.0)
    hi, lo = _split_bf16(spm, 2)
    below = jnp.dot(jnp.concatenate([hi, lo], axis=1), upper2, preferred_element_type=F32) + carry
    a = jnp.exp2(z2 - sp - below)
    if mask is not None:
        a = jnp.where(mask, a, 0.0)
    a = a.astype(BF16)
    if key_minor:
        acc = acc + lax.dot_general(a, v, nt, preferred_element_type=F32)
    else:
        acc = acc + jnp.dot(a, v, preferred_element_type=F32)
    carry = below[:, 0:1] + spm[:, 0:1]
    return carry, acc


def _suffix_matrix2(n):
    s = lax.broadcasted_iota(jnp.int32, (2 * n, n), 0) % n
    j = lax.broadcasted_iota(jnp.int32, (2 * n, n), 1)
    return (s > j).astype(BF16)


def _stack_heads(q):
    lane = lax.broadcasted_iota(jnp.int32, (1, SB_PAIR), 1)
    zero = jnp.zeros_like(q)
    return jnp.concatenate([jnp.where(lane < SB_HEAD_DIM, q, zero),
                            jnp.where(lane >= SB_HEAD_DIM, q, zero)], axis=0)


def _unstack_heads(acc):
    lane = lax.broadcasted_iota(jnp.int32, (1, SB_PAIR), 1)
    m = acc.shape[0] // 2
    return jnp.where(lane < SB_HEAD_DIM, acc[:m], acc[m:])


def _causal_mask(tq, tk):
    t = lax.broadcasted_iota(jnp.int32, (2 * tq, tk), 0) % tq
    s = lax.broadcasted_iota(jnp.int32, (2 * tq, tk), 1)
    return s < t


def _sb_prompt_kernel(q_ref, kt_ref, vt_ref, o_ref):
    i = pl.program_id(2)
    qst = _stack_heads(q_ref[...])
    upper2 = _suffix_matrix2(SB_TILE)
    carry = jnp.zeros((2 * SB_TILE, 1), F32)
    acc = jnp.zeros((2 * SB_TILE, SB_PAIR), F32)
    carry, acc = _sb_tile(qst, kt_ref[i], vt_ref[i], upper2, _causal_mask(SB_TILE, SB_TILE),
                          carry, acc, True)

    def body(jj, c):
        j = i - 1 - jj
        return _sb_tile(qst, kt_ref[j], vt_ref[j], upper2, None, c[0], c[1], True)

    carry, acc = lax.fori_loop(0, i, body, (carry, acc))
    o_ref[...] = _unstack_heads(acc).astype(o_ref.dtype)


def _sb_prompt(q, ktb, vtb):
    nq = SEQ // SB_TILE
    kv_spec = pl.BlockSpec((None, nq, SB_PAIR, SB_TILE), lambda b, p, i: (b, 0, p, 0))
    return pl.pallas_call(
        _sb_prompt_kernel,
        name="sb_prompt",
        grid=(BATCH, SB_PAIRS, nq),
        in_specs=[pl.BlockSpec((SB_TILE, SB_PAIR), lambda b, p, i: (b * nq + i, p)),
                  kv_spec, kv_spec],
        out_specs=pl.BlockSpec((SB_TILE, SB_PAIR), lambda b, p, i: (b * nq + i, p)),
        out_shape=jax.ShapeDtypeStruct((N_PROMPT, D_MODEL), BF16),
        compiler_params=_params("parallel", "parallel", "arbitrary"),
    )(q, ktb, vtb)


def _sb_sample_kernel(q_ref, kn_ref, vn_ref, ck_ref, cv_ref, o_ref):
    qst = _stack_heads(q_ref[...])
    carry = jnp.zeros((2 * DEC_SEQ, 1), F32)
    acc = jnp.zeros((2 * DEC_SEQ, SB_PAIR), F32)
    carry, acc = _sb_tile(qst, kn_ref[...], vn_ref[...], _suffix_matrix2(DEC_SEQ),
                          _causal_mask(DEC_SEQ, DEC_SEQ), carry, acc, False)
    upper2 = _suffix_matrix2(SB_TILE)
    for j in reversed(range(PAST_LEN // SB_TILE)):
        keys = slice(j * SB_TILE, (j + 1) * SB_TILE)
        carry, acc = _sb_tile(qst, ck_ref[:, keys].astype(BF16), cv_ref[:, keys].astype(BF16),
                              upper2, None, carry, acc, True)
    o_ref[...] = _unstack_heads(acc).astype(o_ref.dtype)


def _sb_sample(q, kvb, cache_kt, cache_vt, layer):
    cache_spec = pl.BlockSpec((None, None, SB_PAIR, PAST_LEN), lambda b, p: (layer, b, p, 0))
    return pl.pallas_call(
        _sb_sample_kernel,
        name="sb_sample",
        grid=(DEC_BATCH, SB_PAIRS),
        in_specs=[pl.BlockSpec((DEC_SEQ, SB_PAIR), lambda b, p: (b, p)),
                  pl.BlockSpec((DEC_SEQ, SB_PAIR), lambda b, p: (b, p)),
                  pl.BlockSpec((DEC_SEQ, SB_PAIR), lambda b, p: (b, SB_PAIRS + p)),
                  cache_spec, cache_spec],
        out_specs=pl.BlockSpec((DEC_SEQ, SB_PAIR), lambda b, p: (b, p)),
        out_shape=jax.ShapeDtypeStruct((N_SAMPLE, D_MODEL), BF16),
        compiler_params=_params("parallel", "parallel"),
    )(q, kvb, kvb, cache_kt, cache_vt)


def _gla_kernel(q_ref, k_ref, v_ref, go_ref, gl_ref, s0_ref, wgk_ref, bgk_ref, nw_ref,
                o_ref, sout_ref, st_ref):
    c = pl.program_id(2)

    @pl.when(c == 0)
    def _():
        st_ref[...] = s0_ref[...].T

    q = q_ref[...] * (GLA_DK ** -0.5)
    k = k_ref[...]
    v = v_ref[...]
    gate_in = jnp.dot(gl_ref[...].astype(BF16), wgk_ref[...].astype(BF16),
                      preferred_element_type=F32) + bgk_ref[...]
    g = (jnp.minimum(gate_in, 0.0) - jnp.log1p(jnp.exp(-jnp.abs(gate_in)))) / GLA_GATE_NORMALIZER

    t_i = lax.broadcasted_iota(jnp.int32, (CHUNK, CHUNK), 0)
    s_i = lax.broadcasted_iota(jnp.int32, (CHUNK, CHUNK), 1)
    lower = (s_i <= t_i).astype(BF16)
    b = jnp.zeros((CHUNK, GLA_DK), F32)
    for part in _split_bf16(g, 3):
        b = b + jnp.dot(lower, part, preferred_element_type=F32)

    st = st_ref[...]
    o = lax.dot_general((q * jnp.exp(b)).astype(BF16), st.astype(BF16),
                        (((1,), (1,)), ((), ())), preferred_element_type=F32)

    o_rows = []
    sub_row = lax.broadcasted_iota(jnp.int32, (GLA_SUB, 1), 0)
    for blk in range(CHUNK // GLA_SUB):
        lo = blk * GLA_SUB
        rows = slice(lo, lo + GLA_SUB)
        qi, ki, vi, bi = q[rows], k[rows], v[rows], b[rows]
        oi = o[rows]
        if blk > 0:
            ref = b[lo - 1:lo]
            qt = (qi * jnp.exp(bi - ref)).astype(BF16)
            kt = (k[0:lo] * jnp.exp(ref - b[0:lo])).astype(BF16)
            a = lax.dot_general(qt, kt, (((1,), (1,)), ((), ())), preferred_element_type=F32)
            oi = oi + jnp.dot(a.astype(BF16), v[0:lo].astype(BF16), preferred_element_type=F32)
        for s in range(GLA_SUB):
            diff = jnp.where(sub_row >= s, bi - bi[s:s + 1], -jnp.inf)
            a_s = jnp.sum(qi * ki[s:s + 1] * jnp.exp(diff), axis=-1, keepdims=True)
            oi = oi + a_s * vi[s:s + 1]
        o_rows.append(oi)
    o = jnp.concatenate(o_rows, axis=0)

    b_last = b[CHUNK - 1:CHUNK]
    kd = (k * jnp.exp(b_last - b)).astype(BF16)
    st_new = st * jnp.exp(b_last) + lax.dot_general(
        v.astype(BF16), kd, (((0,), (0,)), ((), ())), preferred_element_type=F32)
    st_ref[...] = st_new

    @pl.when(c == pl.num_programs(2) - 1)
    def _():
        sout_ref[...] = st_new.T

    o = o * lax.rsqrt(jnp.mean(o * o, axis=-1, keepdims=True) + NORM_EPS) * nw_ref[...]
    go = go_ref[...]
    o_ref[...] = (o * (go * jax.nn.sigmoid(go))).astype(o_ref.dtype)


def _gla(proj, state, w_gk2, b_gk, norm_w, nseq, nchunks):
    def rb(b, h, c):
        return b * nchunks + c

    kd_blocks = GLA_KEY_DIM // GLA_DK
    state_spec = pl.BlockSpec((None, None, GLA_DK, GLA_DV), lambda b, h, c: (b, h, 0, 0))
    return pl.pallas_call(
        _gla_kernel,
        name="gla",
        grid=(nseq, GLA_HEADS, nchunks),
        in_specs=[
            pl.BlockSpec((CHUNK, GLA_DK), lambda b, h, c: (rb(b, h, c), h)),
            pl.BlockSpec((CHUNK, GLA_DK), lambda b, h, c: (rb(b, h, c), kd_blocks + h)),
            pl.BlockSpec((CHUNK, GLA_DV), lambda b, h, c: (rb(b, h, c), GLA_HEADS + h)),
            pl.BlockSpec((CHUNK, GLA_DV), lambda b, h, c: (rb(b, h, c), 2 * GLA_HEADS + h)),
            pl.BlockSpec((CHUNK, LANES), lambda b, h, c: (rb(b, h, c), GLA_IN_PAD // LANES - 1)),
            state_spec,
            pl.BlockSpec((LANES, GLA_DK), lambda b, h, c: (0, h)),
            pl.BlockSpec((1, GLA_DK), lambda b, h, c: (0, h)),
            pl.BlockSpec((1, GLA_DV), lambda b, h, c: (0, 0)),
        ],
        out_specs=[
            pl.BlockSpec((CHUNK, GLA_DV), lambda b, h, c: (rb(b, h, c), h)),
            state_spec,
        ],
        out_shape=[
            jax.ShapeDtypeStruct((nseq * nchunks * CHUNK, GLA_VALUE_DIM), BF16),
            jax.ShapeDtypeStruct((nseq, GLA_HEADS, GLA_DK, GLA_DV), F32),
        ],
        scratch_shapes=[pltpu.VMEM((GLA_DV, GLA_DK), F32)],
        compiler_params=_params("parallel", "parallel", "arbitrary"),
    )(proj, proj, proj, proj, proj, state,
      jnp.pad(w_gk2, ((0, LANES - GLA_GATE_RANK), (0, 0))), b_gk.reshape(1, GLA_KEY_DIM),
      norm_w.reshape(1, GLA_DV))


def kernel(x_prompt, x_sample, cache_sb_k, cache_sb_v, state_conv, state_gla, norm_mix, norm_mlp,
           sb_w_qkv, sb_w_o, conv_w_in, conv_w, conv_w_out, gla_w_in, gla_w_gk2, gla_b_gk,
           gla_norm, gla_w_o, mlp_w_up, mlp_w_down, norm_final):
    xp = x_prompt.reshape(N_PROMPT, D_MODEL)
    xs = x_sample.reshape(N_SAMPLE, D_MODEL)
    n_sb = cache_sb_k.shape[0]
    cache_kt = cache_sb_k.transpose(0, 1, 3, 4, 2).reshape(n_sb, DEC_BATCH, D_MODEL, PAST_LEN)
    cache_vt = cache_sb_v.transpose(0, 1, 3, 4, 2).reshape(n_sb, DEC_BATCH, D_MODEL, PAST_LEN)
    sb_kt_p, sb_vt_p, sb_k_s, sb_v_s = [], [], [], []
    conv_p, conv_s, gla_p, gla_s = [], [], [], []

    def vec(a):
        return a.reshape(1, -1)

    for i in range(DEPTH):
        j = i // N_MIXERS
        if i % N_MIXERS == 0:
            w = sb_w_qkv[j]
            wq = (w[:, :D_MODEL] * (SB_HEAD_DIM ** -0.5)).astype(BF16)
            wkv = w[:, D_MODEL:].astype(BF16)
            q_p, kt, vt, ktb, vtb = _sb_qkv_prompt(xp, norm_mix[i], wq, wkv.T)
            q_s, k_s, v_s, kvb_s = _sb_qkv_sample(xs, norm_mix[i], wq, wkv)
            o_p = _sb_prompt(q_p, ktb, vtb)
            o_s = _sb_sample(q_s, kvb_s, cache_kt, cache_vt, j)
            xp, xs = _two_stream(_proj_res_body, "sb_out", [o_p, xp], [o_s, xs],
                                 [sb_w_o[j].astype(BF16)], [(D_MODEL, F32)])
            sb_kt_p.append(kt)
            sb_vt_p.append(vt)
            sb_k_s.append(k_s)
            sb_v_s.append(v_s)
        elif i % N_MIXERS == 1:
            gb_p, u_p, gb_s, u_s = _two_stream(
                _conv_in_body, "conv_in", [xp], [xs],
                [vec(norm_mix[i]), conv_w_in[j].astype(BF16)], [(D_MODEL, F32), (D_MODEL, F32)])
            keep = CONV_WIDTH - 1
            tails_p = u_p.reshape(BATCH, SEQ // CHUNK, CHUNK, D_MODEL)[:, :, CHUNK - keep:]
            prev_p = jnp.concatenate(
                [jnp.zeros((BATCH, 1, keep, D_MODEL), F32), tails_p[:, :-1]], axis=1)
            prev_p = prev_p.reshape(N_PROMPT // CHUNK, keep, D_MODEL)
            xp, xs = _two_stream(_conv_out_body, "conv_out",
                                 [u_p, prev_p, gb_p, xp], [u_s, state_conv[j], gb_s, xs],
                                 [conv_w[j], conv_w_out[j].astype(BF16)], [(D_MODEL, F32)])
            conv_p.append(tails_p[:, -1])
            conv_s.append(u_s.reshape(DEC_BATCH, DEC_SEQ, D_MODEL)[:, DEC_SEQ - keep:])
        else:
            w_in = jnp.pad(gla_w_in[j], ((0, 0), (0, GLA_IN_PAD - GLA_IN_DIM))).astype(BF16)
            proj_p, proj_s = _two_stream(_norm_proj_body, "gla_in", [xp], [xs],
                                         [vec(norm_mix[i]), w_in], [(GLA_IN_PAD, F32)])
            zero_state = jnp.zeros((BATCH, GLA_HEADS, GLA_DK, GLA_DV), F32)
            o_p, s_p = _gla(proj_p, zero_state, gla_w_gk2[j], gla_b_gk[j], gla_norm[j],
                            BATCH, SEQ // CHUNK)
            o_s, s_s = _gla(proj_s, state_gla[j], gla_w_gk2[j], gla_b_gk[j], gla_norm[j],
                            DEC_BATCH, DEC_SEQ // CHUNK)
            xp, xs = _two_stream(_proj_res_body, "gla_out", [o_p, xp], [o_s, xs],
                                 [gla_w_o[j].astype(BF16)], [(D_MODEL, F32)])
            gla_p.append(s_p)
            gla_s.append(s_s)
        xp, xs = _two_stream(
            functools.partial(_mlp_body, final_norm=(i == DEPTH - 1)), "mlp", [xp], [xs],
            [vec(norm_mlp[i]), mlp_w_up[i].astype(BF16), mlp_w_down[i].astype(BF16),
             vec(norm_final)], [(D_MODEL, F32)])

    def prompt_heads(a):
        a = jnp.stack(a).reshape(n_sb, BATCH, SB_HEADS, SB_HEAD_DIM, SEQ)
        return a.transpose(0, 1, 4, 2, 3)

    def sample_heads(a):
        return jnp.stack(a).reshape(n_sb, DEC_BATCH, DEC_SEQ, SB_HEADS, SB_HEAD_DIM)

    return (xp.reshape(BATCH, SEQ, D_MODEL), xs.reshape(DEC_BATCH, DEC_SEQ, D_MODEL),
            prompt_heads(sb_kt_p), prompt_heads(sb_vt_p), sample_heads(sb_k_s), sample_heads(sb_v_s),
            jnp.stack(conv_p), jnp.stack(conv_s), jnp.stack(gla_p), jnp.stack(gla_s))
```

```python
import functools

import jax
import jax.numpy as jnp
from jax import lax
from jax.experimental import pallas as pl
from jax.experimental.pallas import tpu as pltpu

F32 = jnp.float32
BF16 = jnp.bfloat16

D_MODEL = 1024
BATCH = 8
SEQ = 2048
DEPTH = 4
DEC_BATCH = 32
DEC_SEQ = 64
PAST_LEN = 1024
CHUNK = 64
N_MIXERS = 3
SB_HEADS = 16
SB_HEAD_DIM = D_MODEL // SB_HEADS
CONV_WIDTH = 3
GLA_HEADS = 4
GLA_KEY_DIM = D_MODEL // 2
GLA_VALUE_DIM = D_MODEL
GLA_DK = GLA_KEY_DIM // GLA_HEADS
GLA_DV = GLA_VALUE_DIM // GLA_HEADS
GLA_GATE_RANK = 16
GLA_GATE_NORMALIZER = 16.0
GLA_IN_DIM = 2 * GLA_KEY_DIM + 2 * GLA_VALUE_DIM + GLA_GATE_RANK
MLP_HIDDEN = 4 * D_MODEL
NORM_EPS = 1e-6

N_PROMPT = BATCH * SEQ
N_SAMPLE = DEC_BATCH * DEC_SEQ

LANES = 128
GLA_IN_PAD = 25 * LANES
TOKEN_TILE = 512
P_TILES = N_PROMPT // TOKEN_TILE
S_TILES = N_SAMPLE // TOKEN_TILE
COL_CHUNK = 512
SB_TILE = 256
SB_PAIR = 2 * SB_HEAD_DIM
SB_PAIRS = D_MODEL // SB_PAIR
GLA_SUB = 16
VMEM_LIMIT = 56 * 1024 * 1024
LOG2E = 1.4426950408889634
SB_DEAD_LOG2 = 160.0


def _params(*sem):
    return pltpu.CompilerParams(dimension_semantics=sem, vmem_limit_bytes=VMEM_LIMIT)


def _rms(x, gain):
    ms = jnp.mean(x * x, axis=-1, keepdims=True)
    return x * lax.rsqrt(ms + NORM_EPS) * gain


def _split_bf16(x, passes):
    parts = []
    r = x
    for _ in range(passes):
        h = r.astype(BF16)
        parts.append(h)
        r = r - h.astype(F32)
    return parts


def _resident(arr):
    nd = arr.ndim
    return pl.BlockSpec(arr.shape, lambda *_: (0,) * nd, pipeline_mode=pl.Buffered(1))


def _two_stream(body, name, ins_p, ins_s, consts, outs):
    n_in, n_c, n_out = len(ins_p), len(consts), len(outs)

    def kern(*refs):
        p_in, s_in = refs[:n_in], refs[n_in:2 * n_in]
        c = refs[2 * n_in:2 * n_in + n_c]
        p_out = refs[2 * n_in + n_c:2 * n_in + n_c + n_out]
        s_out = refs[2 * n_in + n_c + n_out:]
        i = pl.program_id(0)

        @pl.when(i < P_TILES)
        def _():
            body(p_in, c, p_out)

        @pl.when(i >= P_TILES)
        def _():
            body(s_in, c, s_out)

    def p_idx(i):
        return jnp.minimum(i, P_TILES - 1)

    def s_idx(i):
        return jnp.maximum(i - P_TILES, 0)

    def spec(arr, tiles, idx):
        block = (arr.shape[0] // tiles,) + arr.shape[1:]
        zeros = (0,) * (arr.ndim - 1)
        return pl.BlockSpec(block, lambda i: (idx(i),) + zeros)

    def out_spec(width, idx):
        return pl.BlockSpec((TOKEN_TILE, width), lambda i: (idx(i), 0))

    return pl.pallas_call(
        kern,
        name=name,
        grid=(P_TILES + S_TILES,),
        in_specs=([spec(a, P_TILES, p_idx) for a in ins_p] + [spec(a, S_TILES, s_idx) for a in ins_s]
                  + [_resident(a) for a in consts]),
        out_specs=([out_spec(w, p_idx) for w, _ in outs] + [out_spec(w, s_idx) for w, _ in outs]),
        out_shape=([jax.ShapeDtypeStruct((N_PROMPT, w), dt) for w, dt in outs]
                   + [jax.ShapeDtypeStruct((N_SAMPLE, w), dt) for w, dt in outs]),
        compiler_params=_params("arbitrary"),
    )(*ins_p, *ins_s, *consts)


def _norm_proj_body(ins, consts, outs):
    (x_ref,), (g_ref, w_ref), (o_ref,) = ins, consts, outs
    xn = _rms(x_ref[...], g_ref[...]).astype(BF16)
    dout = o_ref.shape[1]
    for lo in range(0, dout, COL_CHUNK):
        cols = slice(lo, min(lo + COL_CHUNK, dout))
        o_ref[:, cols] = jnp.dot(xn, w_ref[:, cols], preferred_element_type=F32)


def _conv_in_body(ins, consts, outs):
    (x_ref,), (g_ref, w_ref), (gb_ref, u_ref) = ins, consts, outs
    xn = _rms(x_ref[...], g_ref[...]).astype(BF16)
    for lo in range(0, D_MODEL, COL_CHUNK):
        cols = slice(lo, lo + COL_CHUNK)
        cols_c = slice(D_MODEL + lo, D_MODEL + lo + COL_CHUNK)
        cols_x = slice(2 * D_MODEL + lo, 2 * D_MODEL + lo + COL_CHUNK)
        gb_ref[:, cols] = jnp.dot(xn, w_ref[:, cols], preferred_element_type=F32)
        gc = jnp.dot(xn, w_ref[:, cols_c], preferred_element_type=F32)
        xp = jnp.dot(xn, w_ref[:, cols_x], preferred_element_type=F32)
        u_ref[:, cols] = gc * xp


def _proj_res_body(ins, consts, outs):
    (a_ref, r_ref), (w_ref,), (o_ref,) = ins, consts, outs
    o_ref[...] = r_ref[...] + jnp.dot(a_ref[...], w_ref[...], preferred_element_type=F32)


def _conv_out_body(ins, consts, outs):
    (u_ref, prev_ref, gb_ref, r_ref), (cw_ref, w_ref), (o_ref,) = ins, consts, outs
    units = TOKEN_TILE // CHUNK
    u = u_ref[...]
    prev = prev_ref[...]
    p0 = jnp.broadcast_to(prev[:, 0:1, :], (units, CHUNK, D_MODEL)).reshape(TOKEN_TILE, D_MODEL)
    p1 = jnp.broadcast_to(prev[:, 1:2, :], (units, CHUNK, D_MODEL)).reshape(TOKEN_TILE, D_MODEL)
    t = lax.broadcasted_iota(jnp.int32, (TOKEN_TILE, 1), 0) % CHUNK
    s1 = jnp.where(t == 0, p1, pltpu.roll(u, 1, axis=0))
    s2 = jnp.where(t == 0, p0, jnp.where(t == 1, p1, pltpu.roll(u, 2, axis=0)))
    cw = cw_ref[...]
    y = cw[0:1, :] * s2 + cw[1:2, :] * s1 + cw[2:3, :] * u
    a = (gb_ref[...] * y).astype(BF16)
    o_ref[...] = r_ref[...] + jnp.dot(a, w_ref[...], preferred_element_type=F32)


def _mlp_body(ins, consts, outs, *, final_norm):
    (x_ref,), (g_ref, wu_ref, wd_ref, gf_ref), (o_ref,) = ins, consts, outs
    x = x_ref[...]
    xn = _rms(x, g_ref[...]).astype(BF16)
    acc = x
    for lo in range(0, MLP_HIDDEN, COL_CHUNK):
        cols = slice(lo, lo + COL_CHUNK)
        h = jnp.maximum(jnp.dot(xn, wu_ref[:, cols], preferred_element_type=F32), 0.0)
        acc = acc + jnp.dot((h * h).astype(BF16), wd_ref[cols, :], preferred_element_type=F32)
    o_ref[...] = _rms(acc, gf_ref[...]) if final_norm else acc


def _sb_qkv_prompt_kernel(x_ref, g_ref, wq_ref, wkvt_ref, q_ref, kt_ref, vt_ref, ktb_ref, vtb_ref):
    xn = _rms(x_ref[...], g_ref[...]).astype(BF16)
    for lo in range(0, D_MODEL, COL_CHUNK):
        cols = slice(lo, lo + COL_CHUNK)
        q_ref[:, cols] = jnp.dot(xn, wq_ref[:, cols], preferred_element_type=F32).astype(BF16)
    for out_ref, outb_ref, base in ((kt_ref, ktb_ref, 0), (vt_ref, vtb_ref, D_MODEL)):
        for lo in range(0, D_MODEL, COL_CHUNK):
            rows = slice(lo, lo + COL_CHUNK)
            w_rows = slice(base + lo, base + lo + COL_CHUNK)
            t = lax.dot_general(wkvt_ref[w_rows, :], xn, (((1,), (1,)), ((), ())),
                                preferred_element_type=F32)
            out_ref[rows, :] = t
            for kt in range(TOKEN_TILE // SB_TILE):
                outb_ref[kt, rows, :] = t[:, kt * SB_TILE:(kt + 1) * SB_TILE].astype(BF16)


def _sb_qkv_prompt(x, gain, wq, wkvt):
    per_seq = SEQ // TOKEN_TILE
    key_tiles = TOKEN_TILE // SB_TILE
    kt_spec = pl.BlockSpec((None, D_MODEL, TOKEN_TILE), lambda i: (i // per_seq, 0, i % per_seq))
    ktb_spec = pl.BlockSpec((None, key_tiles, D_MODEL, SB_TILE),
                            lambda i: (i // per_seq, i % per_seq, 0, 0))
    kt_shape = jax.ShapeDtypeStruct((BATCH, D_MODEL, SEQ), F32)
    ktb_shape = jax.ShapeDtypeStruct((BATCH, SEQ // SB_TILE, D_MODEL, SB_TILE), BF16)
    gain = gain.reshape(1, D_MODEL)
    return pl.pallas_call(
        _sb_qkv_prompt_kernel,
        name="sb_qkv_prompt",
        grid=(P_TILES,),
        in_specs=[pl.BlockSpec((TOKEN_TILE, D_MODEL), lambda i: (i, 0)),
                  _resident(gain), _resident(wq), _resident(wkvt)],
        out_specs=[pl.BlockSpec((TOKEN_TILE, D_MODEL), lambda i: (i, 0)),
                   kt_spec, kt_spec, ktb_spec, ktb_spec],
        out_shape=[jax.ShapeDtypeStruct((N_PROMPT, D_MODEL), BF16),
                   kt_shape, kt_shape, ktb_shape, ktb_shape],
        compiler_params=_params("parallel"),
    )(x, gain, wq, wkvt)


def _sb_qkv_sample_kernel(x_ref, g_ref, wq_ref, wkv_ref, q_ref, k_ref, v_ref, kvb_ref):
    xn = _rms(x_ref[...], g_ref[...]).astype(BF16)
    for lo in range(0, D_MODEL, COL_CHUNK):
        cols = slice(lo, lo + COL_CHUNK)
        q_ref[:, cols] = jnp.dot(xn, wq_ref[:, cols], preferred_element_type=F32).astype(BF16)
    for out_ref, base in ((k_ref, 0), (v_ref, D_MODEL)):
        for lo in range(0, D_MODEL, COL_CHUNK):
            w_cols = slice(base + lo, base + lo + COL_CHUNK)
            t = jnp.dot(xn, wkv_ref[:, w_cols], preferred_element_type=F32)
            out_ref[:, lo:lo + COL_CHUNK] = t
            kvb_ref[:, w_cols] = t.astype(BF16)


def _sb_qkv_sample(x, gain, wq, wkv):
    tok = pl.BlockSpec((TOKEN_TILE, D_MODEL), lambda i: (i, 0))
    gain = gain.reshape(1, D_MODEL)
    return pl.pallas_call(
        _sb_qkv_sample_kernel,
        name="sb_qkv_sample",
        grid=(S_TILES,),
        in_specs=[tok, _resident(gain), _resident(wq), _resident(wkv)],
        out_specs=[tok, tok, tok, pl.BlockSpec((TOKEN_TILE, 2 * D_MODEL), lambda i: (i, 0))],
        out_shape=[jax.ShapeDtypeStruct((N_SAMPLE, D_MODEL), BF16),
                   jax.ShapeDtypeStruct((N_SAMPLE, D_MODEL), F32),
                   jax.ShapeDtypeStruct((N_SAMPLE, D_MODEL), F32),
                   jax.ShapeDtypeStruct((N_SAMPLE, 2 * D_MODEL), BF16)],
        compiler_params=_params("parallel"),
    )(x, gain, wq, wkv)


def _sb_tile(qst, k, v, upper2, mask, carry, acc, key_minor):
    nt = (((1,), (1,)), ((), ()))
    if key_minor:
        z = jnp.dot(qst, k, preferred_element_type=F32)
    else:
        z = lax.dot_general(qst, k, nt, preferred_element_type=F32)
    z2 = z * LOG2E
    sp = jnp.maximum(z2, 0.0) + jnp.log2(1.0 + jnp.exp2(-jnp.abs(z2)))
    spm = sp if mask is None else jnp.where(mask, sp, 0.0)
    hi, lo = _split_bf16(spm, 2)
    below = jnp.dot(jnp.concatenate([hi, lo], axis=1), upper2, preferred_element_type=F32) + carry
    a = jnp.exp2(z2 - sp - below)
    if mask is not None:
        a = jnp.where(mask, a, 0.0)
    a = a.astype(BF16)
    if key_minor:
        acc = acc + lax.dot_general(a, v, nt, preferred_element_type=F32)
    else:
        acc = acc + jnp.dot(a, v, preferred_element_type=F32)
    carry = below[:, 0:1] + spm[:, 0:1]
    return carry, acc


def _suffix_matrix2(n):
    s = lax.broadcasted_iota(jnp.int32, (2 * n, n), 0) % n
    j = lax.broadcasted_iota(jnp.int32, (2 * n, n), 1)
    return (s > j).astype(BF16)


def _stack_heads(q):
    lane = lax.broadcasted_iota(jnp.int32, (1, SB_PAIR), 1)
    zero = jnp.zeros_like(q)
    return jnp.concatenate([jnp.where(lane < SB_HEAD_DIM, q, zero),
                            jnp.where(lane >= SB_HEAD_DIM, q, zero)], axis=0)


def _unstack_heads(acc):
    lane = lax.broadcasted_iota(jnp.int32, (1, SB_PAIR), 1)
    m = acc.shape[0] // 2
    return jnp.where(lane < SB_HEAD_DIM, acc[:m], acc[m:])


def _causal_mask(tq, tk):
    t = lax.broadcasted_iota(jnp.int32, (2 * tq, tk), 0) % tq
    s = lax.broadcasted_iota(jnp.int32, (2 * tq, tk), 1)
    return s < t


def _sb_prompt_kernel(q_ref, kt_ref, vt_ref, o_ref):
    i = pl.program_id(2)
    qst = _stack_heads(q_ref[...])
    upper2 = _suffix_matrix2(SB_TILE)
    carry = jnp.zeros((2 * SB_TILE, 1), F32)
    acc = jnp.zeros((2 * SB_TILE, SB_PAIR), F32)
    carry, acc = _sb_tile(qst, kt_ref[i], vt_ref[i], upper2, _causal_mask(SB_TILE, SB_TILE),
                          carry, acc, True)

    def live(c):
        return (c[0] < i) & (jnp.min(c[1]) < SB_DEAD_LOG2)

    def body(c):
        jj = c[0]
        j = i - 1 - jj
        return (jj + 1,) + _sb_tile(qst, kt_ref[j], vt_ref[j], upper2, None, c[1], c[2], True)

    _, carry, acc = lax.while_loop(live, body, (jnp.int32(0), carry, acc))
    o_ref[...] = _unstack_heads(acc).astype(o_ref.dtype)


def _sb_prompt(q, ktb, vtb):
    nq = SEQ // SB_TILE
    kv_spec = pl.BlockSpec((None, nq, SB_PAIR, SB_TILE), lambda b, p, i: (b, 0, p, 0))
    return pl.pallas_call(
        _sb_prompt_kernel,
        name="sb_prompt",
        grid=(BATCH, SB_PAIRS, nq),
        in_specs=[pl.BlockSpec((SB_TILE, SB_PAIR), lambda b, p, i: (b * nq + i, p)),
                  kv_spec, kv_spec],
        out_specs=pl.BlockSpec((SB_TILE, SB_PAIR), lambda b, p, i: (b * nq + i, p)),
        out_shape=jax.ShapeDtypeStruct((N_PROMPT, D_MODEL), BF16),
        compiler_params=_params("parallel", "parallel", "arbitrary"),
    )(q, ktb, vtb)


def _sb_sample_kernel(q_ref, kn_ref, vn_ref, ck_ref, cv_ref, o_ref, carry_ref, acc_ref):
    qst = _stack_heads(q_ref[...])
    carry = jnp.zeros((2 * DEC_SEQ, 1), F32)
    acc = jnp.zeros((2 * DEC_SEQ, SB_PAIR), F32)
    carry_ref[...], acc_ref[...] = _sb_tile(
        qst, kn_ref[...], vn_ref[...], _suffix_matrix2(DEC_SEQ),
        _causal_mask(DEC_SEQ, DEC_SEQ), carry, acc, False)
    upper2 = _suffix_matrix2(SB_TILE)
    for j in reversed(range(PAST_LEN // SB_TILE)):
        keys = slice(j * SB_TILE, (j + 1) * SB_TILE)

        @pl.when(jnp.min(carry_ref[...]) < SB_DEAD_LOG2)
        def _():
            carry_ref[...], acc_ref[...] = _sb_tile(
                qst, ck_ref[:, keys].astype(BF16), cv_ref[:, keys].astype(BF16),
                upper2, None, carry_ref[...], acc_ref[...], True)

    o_ref[...] = _unstack_heads(acc_ref[...]).astype(o_ref.dtype)


def _sb_sample(q, kvb, cache_kt, cache_vt, layer):
    cache_spec = pl.BlockSpec((None, None, SB_PAIR, PAST_LEN), lambda b, p: (layer, b, p, 0))
    return pl.pallas_call(
        _sb_sample_kernel,
        name="sb_sample",
        grid=(DEC_BATCH, SB_PAIRS),
        in_specs=[pl.BlockSpec((DEC_SEQ, SB_PAIR), lambda b, p: (b, p)),
                  pl.BlockSpec((DEC_SEQ, SB_PAIR), lambda b, p: (b, p)),
                  pl.BlockSpec((DEC_SEQ, SB_PAIR), lambda b, p: (b, SB_PAIRS + p)),
                  cache_spec, cache_spec],
        out_specs=pl.BlockSpec((DEC_SEQ, SB_PAIR), lambda b, p: (b, p)),
        out_shape=jax.ShapeDtypeStruct((N_SAMPLE, D_MODEL), BF16),
        scratch_shapes=[pltpu.VMEM((2 * DEC_SEQ, 1), F32), pltpu.VMEM((2 * DEC_SEQ, SB_PAIR), F32)],
        compiler_params=_params("parallel", "parallel"),
    )(q, kvb, kvb, cache_kt, cache_vt)


def _gla_kernel(q_ref, k_ref, v_ref, go_ref, gl_ref, s0_ref, wgk_ref, bgk_ref, nw_ref,
                o_ref, sout_ref, st_ref):
    c = pl.program_id(1)

    @pl.when(c == 0)
    def _():
        for h in range(GLA_HEADS):
            st_ref[h] = s0_ref[h].T

    nt = (((1,), (1,)), ((), ()))
    q = q_ref[...] * (GLA_DK ** -0.5)
    k = k_ref[...]
    gate_in = jnp.dot(gl_ref[...].astype(BF16), wgk_ref[...].astype(BF16),
                      preferred_element_type=F32) + bgk_ref[...]
    g = (jnp.minimum(gate_in, 0.0) - jnp.log1p(jnp.exp(-jnp.abs(gate_in)))) / GLA_GATE_NORMALIZER

    t_i = lax.broadcasted_iota(jnp.int32, (CHUNK, CHUNK), 0)
    s_i = lax.broadcasted_iota(jnp.int32, (CHUNK, CHUNK), 1)
    lower = (s_i <= t_i).astype(BF16)
    b = jnp.zeros((CHUNK, GLA_KEY_DIM), F32)
    for part in _split_bf16(g, 3):
        b = b + jnp.dot(lower, part, preferred_element_type=F32)

    qe = (q * jnp.exp(b)).astype(BF16)
    b_last = b[CHUNK - 1:CHUNK]
    kd = (k * jnp.exp(b_last - b)).astype(BF16)
    decay_last = jnp.exp(b_last)

    key_row = lax.broadcasted_iota(jnp.int32, (CHUNK, 1), 0)
    sub_row = lax.broadcasted_iota(jnp.int32, (GLA_SUB, 1), 0)
    key_lane = lax.broadcasted_iota(jnp.int32, (1, CHUNK), 1)
    n_sub = CHUNK // GLA_SUB
    qt, kt = [None] * n_sub, [None] * n_sub
    for blk in range(1, n_sub):
        lo = blk * GLA_SUB
        ref = b[lo - 1:lo]
        qt[blk] = (q[lo:lo + GLA_SUB] * jnp.exp(b[lo:lo + GLA_SUB] - ref)).astype(BF16)
        kt[blk] = (k * jnp.exp(jnp.where(key_row < lo, ref - b, -jnp.inf))).astype(BF16)

    diag = [[None] * GLA_SUB for _ in range(n_sub)]
    for blk in range(n_sub):
        rows = slice(blk * GLA_SUB, (blk + 1) * GLA_SUB)
        qi, ki, bi = q[rows], k[rows], b[rows]
        for s in range(GLA_SUB):
            diff = jnp.where(sub_row >= s, bi - bi[s:s + 1], -jnp.inf)
            diag[blk][s] = qi * (ki[s:s + 1] * jnp.exp(diff))

    nw = nw_ref[...]
    for h in range(GLA_HEADS):
        kl = slice(h * GLA_DK, (h + 1) * GLA_DK)
        vl = slice(h * GLA_DV, (h + 1) * GLA_DV)
        v = v_ref[:, vl].astype(BF16)
        st = st_ref[h]
        a_rows = []
        for blk in range(n_sub):
            if blk == 0:
                a_blk = jnp.zeros((GLA_SUB, CHUNK), F32)
            else:
                a_blk = lax.dot_general(qt[blk][:, kl], kt[blk][:, kl], nt,
                                        preferred_element_type=F32)
            for s in range(GLA_SUB):
                a_s = jnp.sum(diag[blk][s][:, kl], axis=-1, keepdims=True)
                a_blk = jnp.where(key_lane == blk * GLA_SUB + s, a_s, a_blk)
            a_rows.append(a_blk)
        a = jnp.concatenate(a_rows, axis=0).astype(BF16)
        o = (lax.dot_general(qe[:, kl], st.astype(BF16), nt, preferred_element_type=F32)
             + jnp.dot(a, v, preferred_element_type=F32))
        st_new = st * decay_last[:, kl] + lax.dot_general(
            v, kd[:, kl], (((0,), (0,)), ((), ())), preferred_element_type=F32)
        st_ref[h] = st_new

        @pl.when(c == pl.num_programs(1) - 1)
        def _(h=h, st_new=st_new):
            sout_ref[h] = st_new.T

        o = o * lax.rsqrt(jnp.mean(o * o, axis=-1, keepdims=True) + NORM_EPS) * nw
        go = go_ref[:, vl]
        o_ref[:, vl] = (o * (go * jax.nn.sigmoid(go))).astype(o_ref.dtype)


def _gla(proj, state, w_gk2, b_gk, norm_w, nseq, nchunks):
    def rb(b, c):
        return b * nchunks + c

    state_spec = pl.BlockSpec((None, GLA_HEADS, GLA_DK, GLA_DV), lambda b, c: (b, 0, 0, 0))
    w_gk2 = jnp.pad(w_gk2, ((0, LANES - GLA_GATE_RANK), (0, 0)))
    b_gk = b_gk.reshape(1, GLA_KEY_DIM)
    norm_w = norm_w.reshape(1, GLA_DV)
    return pl.pallas_call(
        _gla_kernel,
        name="gla",
        grid=(nseq, nchunks),
        in_specs=[
            pl.BlockSpec((CHUNK, GLA_KEY_DIM), lambda b, c: (rb(b, c), 0)),
            pl.BlockSpec((CHUNK, GLA_KEY_DIM), lambda b, c: (rb(b, c), 1)),
            pl.BlockSpec((CHUNK, GLA_VALUE_DIM), lambda b, c: (rb(b, c), 1)),
            pl.BlockSpec((CHUNK, GLA_VALUE_DIM), lambda b, c: (rb(b, c), 2)),
            pl.BlockSpec((CHUNK, LANES), lambda b, c: (rb(b, c), GLA_IN_PAD // LANES - 1)),
            state_spec,
            _resident(w_gk2), _resident(b_gk), _resident(norm_w),
        ],
        out_specs=[
            pl.BlockSpec((CHUNK, GLA_VALUE_DIM), lambda b, c: (rb(b, c), 0)),
            state_spec,
        ],
        out_shape=[
            jax.ShapeDtypeStruct((nseq * nchunks * CHUNK, GLA_VALUE_DIM), BF16),
            jax.ShapeDtypeStruct((nseq, GLA_HEADS, GLA_DK, GLA_DV), F32),
        ],
        scratch_shapes=[pltpu.VMEM((GLA_HEADS, GLA_DV, GLA_DK), F32)],
        compiler_params=_params("parallel", "arbitrary"),
    )(proj, proj, proj, proj, proj, state, w_gk2, b_gk, norm_w)


def kernel(x_prompt, x_sample, cache_sb_k, cache_sb_v, state_conv, state_gla, norm_mix, norm_mlp,
           sb_w_qkv, sb_w_o, conv_w_in, conv_w, conv_w_out, gla_w_in, gla_w_gk2, gla_b_gk,
           gla_norm, gla_w_o, mlp_w_up, mlp_w_down, norm_final):
    xp = x_prompt.reshape(N_PROMPT, D_MODEL)
    xs = x_sample.reshape(N_SAMPLE, D_MODEL)
    n_sb = cache_sb_k.shape[0]
    cache_kt = cache_sb_k.transpose(0, 1, 3, 4, 2).reshape(n_sb, DEC_BATCH, D_MODEL, PAST_LEN)
    cache_vt = cache_sb_v.transpose(0, 1, 3, 4, 2).reshape(n_sb, DEC_BATCH, D_MODEL, PAST_LEN)
    sb_kt_p, sb_vt_p, sb_k_s, sb_v_s = [], [], [], []
    conv_p, conv_s, gla_p, gla_s = [], [], [], []

    def vec(a):
        return a.reshape(1, -1)

    for i in range(DEPTH):
        j = i // N_MIXERS
        if i % N_MIXERS == 0:
            w = sb_w_qkv[j]
            wq = (w[:, :D_MODEL] * (SB_HEAD_DIM ** -0.5)).astype(BF16)
            wkv = w[:, D_MODEL:].astype(BF16)
            q_p, kt, vt, ktb, vtb = _sb_qkv_prompt(xp, norm_mix[i], wq, wkv.T)
            q_s, k_s, v_s, kvb_s = _sb_qkv_sample(xs, norm_mix[i], wq, wkv)
            o_p = _sb_prompt(q_p, ktb, vtb)
            o_s = _sb_sample(q_s, kvb_s, cache_kt, cache_vt, j)
            xp, xs = _two_stream(_proj_res_body, "sb_out", [o_p, xp], [o_s, xs],
                                 [sb_w_o[j].astype(BF16)], [(D_MODEL, F32)])
            sb_kt_p.append(kt)
            sb_vt_p.append(vt)
            sb_k_s.append(k_s)
            sb_v_s.append(v_s)
        elif i % N_MIXERS == 1:
            gb_p, u_p, gb_s, u_s = _two_stream(
                _conv_in_body, "conv_in", [xp], [xs],
                [vec(norm_mix[i]), conv_w_in[j].astype(BF16)], [(D_MODEL, F32), (D_MODEL, F32)])
            keep = CONV_WIDTH - 1
            tails_p = u_p.reshape(BATCH, SEQ // CHUNK, CHUNK, D_MODEL)[:, :, CHUNK - keep:]
            prev_p = jnp.concatenate(
                [jnp.zeros((BATCH, 1, keep, D_MODEL), F32), tails_p[:, :-1]], axis=1)
            prev_p = prev_p.reshape(N_PROMPT // CHUNK, keep, D_MODEL)
            xp, xs = _two_stream(_conv_out_body, "conv_out",
                                 [u_p, prev_p, gb_p, xp], [u_s, state_conv[j], gb_s, xs],
                                 [conv_w[j], conv_w_out[j].astype(BF16)], [(D_MODEL, F32)])
            conv_p.append(tails_p[:, -1])
            conv_s.append(u_s.reshape(DEC_BATCH, DEC_SEQ, D_MODEL)[:, DEC_SEQ - keep:])
        else:
            w_in = jnp.pad(gla_w_in[j], ((0, 0), (0, GLA_IN_PAD - GLA_IN_DIM))).astype(BF16)
            proj_p, proj_s = _two_stream(_norm_proj_body, "gla_in", [xp], [xs],
                                         [vec(norm_mix[i]), w_in], [(GLA_IN_PAD, F32)])
            zero_state = jnp.zeros((BATCH, GLA_HEADS, GLA_DK, GLA_DV), F32)
            o_p, s_p = _gla(proj_p, zero_state, gla_w_gk2[j], gla_b_gk[j], gla_norm[j],
                            BATCH, SEQ // CHUNK)
            o_s, s_s = _gla(proj_s, state_gla[j], gla_w_gk2[j], gla_b_gk[j], gla_norm[j],
                            DEC_BATCH, DEC_SEQ // CHUNK)
            xp, xs = _two_stream(_proj_res_body, "gla_out", [o_p, xp], [o_s, xs],
                                 [gla_w_o[j].astype(BF16)], [(D_MODEL, F32)])
            gla_p.append(s_p)
            gla_s.append(s_s)
        xp, xs = _two_stream(
            functools.partial(_mlp_body, final_norm=(i == DEPTH - 1)), "mlp", [xp], [xs],
            [vec(norm_mlp[i]), mlp_w_up[i].astype(BF16), mlp_w_down[i].astype(BF16),
             vec(norm_final)], [(D_MODEL, F32)])

    def prompt_heads(a):
        a = jnp.stack(a).reshape(n_sb, BATCH, SB_HEADS, SB_HEAD_DIM, SEQ)
        return a.transpose(0, 1, 4, 2, 3)

    def sample_heads(a):
        return jnp.stack(a).reshape(n_sb, DEC_BATCH, DEC_SEQ, SB_HEADS, SB_HEAD_DIM)

    return (xp.reshape(BATCH, SEQ, D_MODEL), xs.reshape(DEC_BATCH, DEC_SEQ, D_MODEL),
            prompt_heads(sb_kt_p), prompt_heads(sb_vt_p), sample_heads(sb_k_s), sample_heads(sb_v_s),
            jnp.stack(conv_p), jnp.stack(conv_s), jnp.stack(gla_p), jnp.stack(gla_s))
```

```python
import functools

import jax
import jax.numpy as jnp
from jax import lax
from jax.experimental import pallas as pl
from jax.experimental.pallas import tpu as pltpu

F32 = jnp.float32
BF16 = jnp.bfloat16

D_MODEL = 1024
BATCH = 8
SEQ = 2048
DEPTH = 4
DEC_BATCH = 32
DEC_SEQ = 64
PAST_LEN = 1024
CHUNK = 64
N_MIXERS = 3
SB_HEADS = 16
SB_HEAD_DIM = D_MODEL // SB_HEADS
CONV_WIDTH = 3
GLA_HEADS = 4
GLA_KEY_DIM = D_MODEL // 2
GLA_VALUE_DIM = D_MODEL
GLA_DK = GLA_KEY_DIM // GLA_HEADS
GLA_DV = GLA_VALUE_DIM // GLA_HEADS
GLA_GATE_RANK = 16
GLA_GATE_NORMALIZER = 16.0
GLA_IN_DIM = 2 * GLA_KEY_DIM + 2 * GLA_VALUE_DIM + GLA_GATE_RANK
MLP_HIDDEN = 4 * D_MODEL
NORM_EPS = 1e-6

N_PROMPT = BATCH * SEQ
N_SAMPLE = DEC_BATCH * DEC_SEQ

LANES = 128
GLA_IN_PAD = 25 * LANES
TOKEN_TILE = 512
P_TILES = N_PROMPT // TOKEN_TILE
S_TILES = N_SAMPLE // TOKEN_TILE
COL_CHUNK = 512
SB_TILE = 256
SB_PAIR = 2 * SB_HEAD_DIM
SB_PAIRS = D_MODEL // SB_PAIR
SB_SAMPLE_PAIRS = 2
GLA_SEQS = 2
GLA_SUB = 16
VMEM_LIMIT = 56 * 1024 * 1024
LOG2E = 1.4426950408889634
SB_DEAD_LOG2 = 160.0


def _params(*sem):
    return pltpu.CompilerParams(dimension_semantics=sem, vmem_limit_bytes=VMEM_LIMIT)


def _rms(x, gain):
    ms = jnp.mean(x * x, axis=-1, keepdims=True)
    return x * lax.rsqrt(ms + NORM_EPS) * gain


def _split_bf16(x, passes):
    parts = []
    r = x
    for _ in range(passes):
        h = r.astype(BF16)
        parts.append(h)
        r = r - h.astype(F32)
    return parts


def _resident(arr):
    nd = arr.ndim
    return pl.BlockSpec(arr.shape, lambda *_: (0,) * nd, pipeline_mode=pl.Buffered(1))


def _two_stream(body, name, ins_p, ins_s, consts, outs):
    n_in, n_c, n_out = len(ins_p), len(consts), len(outs)

    def kern(*refs):
        p_in, s_in = refs[:n_in], refs[n_in:2 * n_in]
        c = refs[2 * n_in:2 * n_in + n_c]
        p_out = refs[2 * n_in + n_c:2 * n_in + n_c + n_out]
        s_out = refs[2 * n_in + n_c + n_out:]
        i = pl.program_id(0)

        @pl.when(i < P_TILES)
        def _():
            body(p_in, c, p_out)

        @pl.when(i >= P_TILES)
        def _():
            body(s_in, c, s_out)

    def p_idx(i):
        return jnp.minimum(i, P_TILES - 1)

    def s_idx(i):
        return jnp.maximum(i - P_TILES, 0)

    def spec(arr, tiles, idx):
        block = (arr.shape[0] // tiles,) + arr.shape[1:]
        zeros = (0,) * (arr.ndim - 1)
        return pl.BlockSpec(block, lambda i: (idx(i),) + zeros)

    def out_spec(width, idx):
        return pl.BlockSpec((TOKEN_TILE, width), lambda i: (idx(i), 0))

    return pl.pallas_call(
        kern,
        name=name,
        grid=(P_TILES + S_TILES,),
        in_specs=([spec(a, P_TILES, p_idx) for a in ins_p] + [spec(a, S_TILES, s_idx) for a in ins_s]
                  + [_resident(a) for a in consts]),
        out_specs=([out_spec(w, p_idx) for w, _ in outs] + [out_spec(w, s_idx) for w, _ in outs]),
        out_shape=([jax.ShapeDtypeStruct((N_PROMPT, w), dt) for w, dt in outs]
                   + [jax.ShapeDtypeStruct((N_SAMPLE, w), dt) for w, dt in outs]),
        compiler_params=_params("arbitrary"),
    )(*ins_p, *ins_s, *consts)


def _norm_proj_body(ins, consts, outs):
    (x_ref,), (g_ref, w_ref), (o_ref,) = ins, consts, outs
    xn = _rms(x_ref[...], g_ref[...]).astype(BF16)
    dout = o_ref.shape[1]
    for lo in range(0, dout, COL_CHUNK):
        cols = slice(lo, min(lo + COL_CHUNK, dout))
        o_ref[:, cols] = jnp.dot(xn, w_ref[:, cols], preferred_element_type=F32)


def _conv_in_body(ins, consts, outs):
    (x_ref,), (g_ref, w_ref), (gb_ref, u_ref) = ins, consts, outs
    xn = _rms(x_ref[...], g_ref[...]).astype(BF16)
    for lo in range(0, D_MODEL, COL_CHUNK):
        cols = slice(lo, lo + COL_CHUNK)
        cols_c = slice(D_MODEL + lo, D_MODEL + lo + COL_CHUNK)
        cols_x = slice(2 * D_MODEL + lo, 2 * D_MODEL + lo + COL_CHUNK)
        gb_ref[:, cols] = jnp.dot(xn, w_ref[:, cols], preferred_element_type=F32)
        gc = jnp.dot(xn, w_ref[:, cols_c], preferred_element_type=F32)
        xp = jnp.dot(xn, w_ref[:, cols_x], preferred_element_type=F32)
        u_ref[:, cols] = gc * xp


def _proj_res_body(ins, consts, outs):
    (a_ref, r_ref), (w_ref,), (o_ref,) = ins, consts, outs
    o_ref[...] = r_ref[...] + jnp.dot(a_ref[...], w_ref[...], preferred_element_type=F32)


def _conv_out_body(ins, consts, outs):
    (u_ref, prev_ref, gb_ref, r_ref), (cw_ref, w_ref), (o_ref,) = ins, consts, outs
    units = TOKEN_TILE // CHUNK
    u = u_ref[...]
    prev = prev_ref[...]
    p0 = jnp.broadcast_to(prev[:, 0:1, :], (units, CHUNK, D_MODEL)).reshape(TOKEN_TILE, D_MODEL)
    p1 = jnp.broadcast_to(prev[:, 1:2, :], (units, CHUNK, D_MODEL)).reshape(TOKEN_TILE, D_MODEL)
    t = lax.broadcasted_iota(jnp.int32, (TOKEN_TILE, 1), 0) % CHUNK
    s1 = jnp.where(t == 0, p1, pltpu.roll(u, 1, axis=0))
    s2 = jnp.where(t == 0, p0, jnp.where(t == 1, p1, pltpu.roll(u, 2, axis=0)))
    cw = cw_ref[...]
    y = cw[0:1, :] * s2 + cw[1:2, :] * s1 + cw[2:3, :] * u
    a = (gb_ref[...] * y).astype(BF16)
    o_ref[...] = r_ref[...] + jnp.dot(a, w_ref[...], preferred_element_type=F32)


def _mlp_body(ins, consts, outs, *, final_norm):
    (x_ref,), (g_ref, wu_ref, wd_ref, gf_ref), (o_ref,) = ins, consts, outs
    x = x_ref[...]
    xn = _rms(x, g_ref[...]).astype(BF16)
    acc = x
    for lo in range(0, MLP_HIDDEN, COL_CHUNK):
        cols = slice(lo, lo + COL_CHUNK)
        h = jnp.maximum(jnp.dot(xn, wu_ref[:, cols], preferred_element_type=F32), 0.0)
        acc = acc + jnp.dot((h * h).astype(BF16), wd_ref[cols, :], preferred_element_type=F32)
    o_ref[...] = _rms(acc, gf_ref[...]) if final_norm else acc


def _sb_qkv_prompt_kernel(x_ref, g_ref, wq_ref, wkvt_ref, q_ref, kt_ref, vt_ref, ktb_ref, vtb_ref):
    xn = _rms(x_ref[...], g_ref[...]).astype(BF16)
    for lo in range(0, D_MODEL, COL_CHUNK):
        cols = slice(lo, lo + COL_CHUNK)
        q_ref[:, cols] = jnp.dot(xn, wq_ref[:, cols], preferred_element_type=F32).astype(BF16)
    for out_ref, outb_ref, base in ((kt_ref, ktb_ref, 0), (vt_ref, vtb_ref, D_MODEL)):
        for lo in range(0, D_MODEL, COL_CHUNK):
            rows = slice(lo, lo + COL_CHUNK)
            w_rows = slice(base + lo, base + lo + COL_CHUNK)
            t = lax.dot_general(wkvt_ref[w_rows, :], xn, (((1,), (1,)), ((), ())),
                                preferred_element_type=F32)
            out_ref[rows, :] = t
            for kt in range(TOKEN_TILE // SB_TILE):
                outb_ref[kt, rows, :] = t[:, kt * SB_TILE:(kt + 1) * SB_TILE].astype(BF16)


def _sb_qkv_prompt(x, gain, wq, wkvt):
    per_seq = SEQ // TOKEN_TILE
    key_tiles = TOKEN_TILE // SB_TILE
    kt_spec = pl.BlockSpec((None, D_MODEL, TOKEN_TILE), lambda i: (i // per_seq, 0, i % per_seq))
    ktb_spec = pl.BlockSpec((None, key_tiles, D_MODEL, SB_TILE),
                            lambda i: (i // per_seq, i % per_seq, 0, 0))
    kt_shape = jax.ShapeDtypeStruct((BATCH, D_MODEL, SEQ), F32)
    ktb_shape = jax.ShapeDtypeStruct((BATCH, SEQ // SB_TILE, D_MODEL, SB_TILE), BF16)
    gain = gain.reshape(1, D_MODEL)
    return pl.pallas_call(
        _sb_qkv_prompt_kernel,
        name="sb_qkv_prompt",
        grid=(P_TILES,),
        in_specs=[pl.BlockSpec((TOKEN_TILE, D_MODEL), lambda i: (i, 0)),
                  _resident(gain), _resident(wq), _resident(wkvt)],
        out_specs=[pl.BlockSpec((TOKEN_TILE, D_MODEL), lambda i: (i, 0)),
                   kt_spec, kt_spec, ktb_spec, ktb_spec],
        out_shape=[jax.ShapeDtypeStruct((N_PROMPT, D_MODEL), BF16),
                   kt_shape, kt_shape, ktb_shape, ktb_shape],
        compiler_params=_params("parallel"),
    )(x, gain, wq, wkvt)


def _sb_qkv_sample_kernel(x_ref, g_ref, wq_ref, wkv_ref, q_ref, k_ref, v_ref, kvb_ref):
    xn = _rms(x_ref[...], g_ref[...]).astype(BF16)
    for lo in range(0, D_MODEL, COL_CHUNK):
        cols = slice(lo, lo + COL_CHUNK)
        q_ref[:, cols] = jnp.dot(xn, wq_ref[:, cols], preferred_element_type=F32).astype(BF16)
    for out_ref, base in ((k_ref, 0), (v_ref, D_MODEL)):
        for lo in range(0, D_MODEL, COL_CHUNK):
            w_cols = slice(base + lo, base + lo + COL_CHUNK)
            t = jnp.dot(xn, wkv_ref[:, w_cols], preferred_element_type=F32)
            out_ref[:, lo:lo + COL_CHUNK] = t
            kvb_ref[:, w_cols] = t.astype(BF16)


def _sb_qkv_sample(x, gain, wq, wkv):
    tok = pl.BlockSpec((TOKEN_TILE, D_MODEL), lambda i: (i, 0))
    gain = gain.reshape(1, D_MODEL)
    return pl.pallas_call(
        _sb_qkv_sample_kernel,
        name="sb_qkv_sample",
        grid=(S_TILES,),
        in_specs=[tok, _resident(gain), _resident(wq), _resident(wkv)],
        out_specs=[tok, tok, tok, pl.BlockSpec((TOKEN_TILE, 2 * D_MODEL), lambda i: (i, 0))],
        out_shape=[jax.ShapeDtypeStruct((N_SAMPLE, D_MODEL), BF16),
                   jax.ShapeDtypeStruct((N_SAMPLE, D_MODEL), F32),
                   jax.ShapeDtypeStruct((N_SAMPLE, D_MODEL), F32),
                   jax.ShapeDtypeStruct((N_SAMPLE, 2 * D_MODEL), BF16)],
        compiler_params=_params("parallel"),
    )(x, gain, wq, wkv)


def _sb_tile(qst, k, v, upper2, mask, carry, acc, key_minor):
    nt = (((1,), (1,)), ((), ()))
    if key_minor:
        z = jnp.dot(qst, k, preferred_element_type=F32)
    else:
        z = lax.dot_general(qst, k, nt, preferred_element_type=F32)
    z2 = z * LOG2E
    sp = jnp.maximum(z2, 0.0) + jnp.log2(1.0 + jnp.exp2(-jnp.abs(z2)))
    spm = sp if mask is None else jnp.where(mask, sp, 0.0)
    hi, lo = _split_bf16(spm, 2)
    below = jnp.dot(jnp.concatenate([hi, lo], axis=1), upper2, preferred_element_type=F32) + carry
    a = jnp.exp2(z2 - sp - below)
    if mask is not None:
        a = jnp.where(mask, a, 0.0)
    a = a.astype(BF16)
    if key_minor:
        acc = acc + lax.dot_general(a, v, nt, preferred_element_type=F32)
    else:
        acc = acc + jnp.dot(a, v, preferred_element_type=F32)
    carry = below[:, 0:1] + spm[:, 0:1]
    return carry, acc


def _suffix_matrix2(n):
    s = lax.broadcasted_iota(jnp.int32, (2 * n, n), 0) % n
    j = lax.broadcasted_iota(jnp.int32, (2 * n, n), 1)
    return (s > j).astype(BF16)


def _stack_heads(q):
    lane = lax.broadcasted_iota(jnp.int32, (1, SB_PAIR), 1)
    zero = jnp.zeros_like(q)
    return jnp.concatenate([jnp.where(lane < SB_HEAD_DIM, q, zero),
                            jnp.where(lane >= SB_HEAD_DIM, q, zero)], axis=0)


def _unstack_heads(acc):
    lane = lax.broadcasted_iota(jnp.int32, (1, SB_PAIR), 1)
    m = acc.shape[0] // 2
    return jnp.where(lane < SB_HEAD_DIM, acc[:m], acc[m:])


def _causal_mask(tq, tk):
    t = lax.broadcasted_iota(jnp.int32, (2 * tq, tk), 0) % tq
    s = lax.broadcasted_iota(jnp.int32, (2 * tq, tk), 1)
    return s < t


def _sb_prompt_kernel(q_ref, kt_ref, vt_ref, o_ref):
    i = pl.program_id(2)
    qst = _stack_heads(q_ref[...])
    upper2 = _suffix_matrix2(SB_TILE)
    causal = _causal_mask(SB_TILE, SB_TILE)
    zero_c = jnp.zeros((2 * SB_TILE, 1), F32)
    zero_a = jnp.zeros((2 * SB_TILE, SB_PAIR), F32)

    @pl.when(i == 0)
    def _():
        _, acc = _sb_tile(qst, kt_ref[0], vt_ref[0], upper2, causal, zero_c, zero_a, True)
        o_ref[...] = _unstack_heads(acc).astype(o_ref.dtype)

    @pl.when(i > 0)
    def _():
        carry, acc = _sb_tile(qst, kt_ref[i], vt_ref[i], upper2, causal, zero_c, zero_a, True)
        carry, acc = _sb_tile(qst, kt_ref[i - 1], vt_ref[i - 1], upper2, None, carry, acc, True)

        def live(c):
            return (c[0] < i) & (jnp.min(c[1]) < SB_DEAD_LOG2)

        def body(c):
            j = i - 1 - c[0]
            return (c[0] + 1,) + _sb_tile(qst, kt_ref[j], vt_ref[j], upper2, None, c[1], c[2], True)

        _, _, acc = lax.while_loop(live, body, (jnp.int32(1), carry, acc))
        o_ref[...] = _unstack_heads(acc).astype(o_ref.dtype)


def _sb_prompt(q, ktb, vtb):
    nq = SEQ // SB_TILE
    kv_spec = pl.BlockSpec((None, nq, SB_PAIR, SB_TILE), lambda b, p, i: (b, 0, p, 0))
    return pl.pallas_call(
        _sb_prompt_kernel,
        name="sb_prompt",
        grid=(BATCH, SB_PAIRS, nq),
        in_specs=[pl.BlockSpec((SB_TILE, SB_PAIR), lambda b, p, i: (b * nq + i, p)),
                  kv_spec, kv_spec],
        out_specs=pl.BlockSpec((SB_TILE, SB_PAIR), lambda b, p, i: (b * nq + i, p)),
        out_shape=jax.ShapeDtypeStruct((N_PROMPT, D_MODEL), BF16),
        compiler_params=_params("parallel", "parallel", "arbitrary"),
    )(q, ktb, vtb)


def _sb_sample_kernel(q_ref, kn_ref, vn_ref, ck_ref, cv_ref, o_ref, carry_ref, acc_ref):
    upper_new = _suffix_matrix2(DEC_SEQ)
    upper2 = _suffix_matrix2(SB_TILE)
    causal = _causal_mask(DEC_SEQ, DEC_SEQ)
    n_cache = PAST_LEN // SB_TILE

    def cache_tile(pp, j, qst, carry, acc):
        feats = slice(pp * SB_PAIR, (pp + 1) * SB_PAIR)
        keys = slice(j * SB_TILE, (j + 1) * SB_TILE)
        return _sb_tile(qst, ck_ref[feats, keys].astype(BF16), cv_ref[feats, keys].astype(BF16),
                        upper2, None, carry, acc, True)

    qsts = []
    for pp in range(SB_SAMPLE_PAIRS):
        lanes = slice(pp * SB_PAIR, (pp + 1) * SB_PAIR)
        qst = _stack_heads(q_ref[:, lanes])
        carry = jnp.zeros((2 * DEC_SEQ, 1), F32)
        acc = jnp.zeros((2 * DEC_SEQ, SB_PAIR), F32)
        carry, acc = _sb_tile(qst, kn_ref[:, lanes], vn_ref[:, lanes], upper_new, causal,
                              carry, acc, False)
        carry_ref[pp], acc_ref[pp] = cache_tile(pp, n_cache - 1, qst, carry, acc)
        qsts.append(qst)

    for j in reversed(range(n_cache - 1)):
        @pl.when(jnp.min(carry_ref[...]) < SB_DEAD_LOG2)
        def _():
            for pp in range(SB_SAMPLE_PAIRS):
                carry_ref[pp], acc_ref[pp] = cache_tile(pp, j, qsts[pp], carry_ref[pp], acc_ref[pp])

    for pp in range(SB_SAMPLE_PAIRS):
        lanes = slice(pp * SB_PAIR, (pp + 1) * SB_PAIR)
        o_ref[:, lanes] = _unstack_heads(acc_ref[pp]).astype(o_ref.dtype)


def _sb_sample(q, kvb, cache_kt, cache_vt, layer):
    width = SB_SAMPLE_PAIRS * SB_PAIR
    steps = D_MODEL // width
    cache_spec = pl.BlockSpec((None, None, width, PAST_LEN), lambda b, p: (layer, b, p, 0))
    return pl.pallas_call(
        _sb_sample_kernel,
        name="sb_sample",
        grid=(DEC_BATCH, steps),
        in_specs=[pl.BlockSpec((DEC_SEQ, width), lambda b, p: (b, p)),
                  pl.BlockSpec((DEC_SEQ, width), lambda b, p: (b, p)),
                  pl.BlockSpec((DEC_SEQ, width), lambda b, p: (b, steps + p)),
                  cache_spec, cache_spec],
        out_specs=pl.BlockSpec((DEC_SEQ, width), lambda b, p: (b, p)),
        out_shape=jax.ShapeDtypeStruct((N_SAMPLE, D_MODEL), BF16),
        scratch_shapes=[pltpu.VMEM((SB_SAMPLE_PAIRS, 2 * DEC_SEQ, 1), F32),
                        pltpu.VMEM((SB_SAMPLE_PAIRS, 2 * DEC_SEQ, SB_PAIR), F32)],
        compiler_params=_params("parallel", "parallel"),
    )(q, kvb, kvb, cache_kt, cache_vt)


def _gla_kernel(q_ref, k_ref, v_ref, go_ref, gl_ref, s0_ref, wgk_ref, bgk_ref, nw_ref,
                o_ref, sout_ref, st_ref, *, nchunks):
    for sq in range(GLA_SEQS):
        _gla_seq(q_ref.at[sq], k_ref.at[sq], v_ref.at[sq], go_ref.at[sq], gl_ref.at[sq],
                 s0_ref.at[sq], wgk_ref, bgk_ref, nw_ref, o_ref.at[sq], sout_ref.at[sq],
                 st_ref.at[sq], nchunks)


def _gla_seq(q_ref, k_ref, v_ref, go_ref, gl_ref, s0_ref, wgk_ref, bgk_ref, nw_ref,
             o_ref, sout_ref, st_ref, nchunks):
    c = pl.program_id(1)

    if nchunks > 1:
        @pl.when(c == 0)
        def _():
            for h in range(GLA_HEADS):
                st_ref[h] = s0_ref[h].T

    nt = (((1,), (1,)), ((), ()))
    q = q_ref[...] * (GLA_DK ** -0.5)
    k = k_ref[...]
    gate_in = jnp.dot(gl_ref[...].astype(BF16), wgk_ref[...].astype(BF16),
                      preferred_element_type=F32) + bgk_ref[...]
    g = (jnp.minimum(gate_in, 0.0) - jnp.log1p(jnp.exp(-jnp.abs(gate_in)))) / GLA_GATE_NORMALIZER

    t_i = lax.broadcasted_iota(jnp.int32, (CHUNK, CHUNK), 0)
    s_i = lax.broadcasted_iota(jnp.int32, (CHUNK, CHUNK), 1)
    lower = (s_i <= t_i).astype(BF16)
    b = jnp.zeros((CHUNK, GLA_KEY_DIM), F32)
    for part in _split_bf16(g, 3):
        b = b + jnp.dot(lower, part, preferred_element_type=F32)

    qe = (q * jnp.exp(b)).astype(BF16)
    b_last = b[CHUNK - 1:CHUNK]
    kd = (k * jnp.exp(b_last - b)).astype(BF16)
    decay_last = jnp.exp(b_last)

    key_row = lax.broadcasted_iota(jnp.int32, (CHUNK, 1), 0)
    sub_row = lax.broadcasted_iota(jnp.int32, (GLA_SUB, 1), 0)
    key_lane = lax.broadcasted_iota(jnp.int32, (1, CHUNK), 1)
    n_sub = CHUNK // GLA_SUB
    qt, kt = [None] * n_sub, [None] * n_sub
    for blk in range(1, n_sub):
        lo = blk * GLA_SUB
        ref = b[lo - 1:lo]
        qt[blk] = (q[lo:lo + GLA_SUB] * jnp.exp(b[lo:lo + GLA_SUB] - ref)).astype(BF16)
        kt[blk] = (k * jnp.exp(jnp.where(key_row < lo, ref - b, -jnp.inf))).astype(BF16)

    feat_head = lax.broadcasted_iota(jnp.int32, (GLA_KEY_DIM, GLA_HEADS * CHUNK), 0) // GLA_DK
    out_head = lax.broadcasted_iota(jnp.int32, (GLA_KEY_DIM, GLA_HEADS * CHUNK), 1) // CHUNK
    head_sum = (feat_head == out_head).astype(BF16)
    out_key = lax.broadcasted_iota(jnp.int32, (1, GLA_HEADS * CHUNK), 1) % CHUNK
    diag_rows = []
    for blk in range(n_sub):
        rows = slice(blk * GLA_SUB, (blk + 1) * GLA_SUB)
        qi, ki, bi = q[rows], k[rows], b[rows]
        prods = []
        for s in range(GLA_SUB):
            diff = jnp.where(sub_row >= s, bi - bi[s:s + 1], -jnp.inf)
            prods.append((qi * (ki[s:s + 1] * jnp.exp(diff))).astype(BF16))
        sums = jnp.dot(jnp.concatenate(prods, axis=0), head_sum, preferred_element_type=F32)
        a_blk = jnp.zeros((GLA_SUB, GLA_HEADS * CHUNK), F32)
        for s in range(GLA_SUB):
            a_blk = jnp.where(out_key == blk * GLA_SUB + s,
                              sums[s * GLA_SUB:(s + 1) * GLA_SUB], a_blk)
        diag_rows.append(a_blk)
    a_diag = jnp.concatenate(diag_rows, axis=0)

    nw = nw_ref[...]
    for h in range(GLA_HEADS):
        kl = slice(h * GLA_DK, (h + 1) * GLA_DK)
        vl = slice(h * GLA_DV, (h + 1) * GLA_DV)
        v = v_ref[:, vl].astype(BF16)
        st = st_ref[h] if nchunks > 1 else s0_ref[h].T
        a_rows = [jnp.zeros((GLA_SUB, CHUNK), F32)]
        for blk in range(1, n_sub):
            a_rows.append(lax.dot_general(qt[blk][:, kl], kt[blk][:, kl], nt,
                                          preferred_element_type=F32))
        a = (jnp.concatenate(a_rows, axis=0) + a_diag[:, h * CHUNK:(h + 1) * CHUNK]).astype(BF16)
        o = (lax.dot_general(qe[:, kl], st.astype(BF16), nt, preferred_element_type=F32)
             + jnp.dot(a, v, preferred_element_type=F32))
        st_new = st * decay_last[:, kl] + lax.dot_general(
            v, kd[:, kl], (((0,), (0,)), ((), ())), preferred_element_type=F32)
        if nchunks > 1:
            st_ref[h] = st_new

            @pl.when(c == nchunks - 1)
            def _(h=h, st_new=st_new):
                sout_ref[h] = st_new.T
        else:
            sout_ref[h] = st_new.T

        o = o * lax.rsqrt(jnp.mean(o * o, axis=-1, keepdims=True) + NORM_EPS) * nw
        go = go_ref[:, vl]
        o_ref[:, vl] = (o * (go * jax.nn.sigmoid(go))).astype(o_ref.dtype)


def _gla(proj, state, w_gk2, b_gk, norm_w, nseq, nchunks):
    def rows(width, col):
        return pl.BlockSpec((GLA_SEQS, CHUNK, width), lambda b, c: (b, c, col))

    state_spec = pl.BlockSpec((GLA_SEQS, GLA_HEADS, GLA_DK, GLA_DV), lambda b, c: (b, 0, 0, 0))
    w_gk2 = jnp.pad(w_gk2, ((0, LANES - GLA_GATE_RANK), (0, 0)))
    b_gk = b_gk.reshape(1, GLA_KEY_DIM)
    norm_w = norm_w.reshape(1, GLA_DV)
    proj = proj.reshape(nseq, nchunks * CHUNK, GLA_IN_PAD)
    o, s_out = pl.pallas_call(
        functools.partial(_gla_kernel, nchunks=nchunks),
        name="gla",
        grid=(nseq // GLA_SEQS, nchunks),
        in_specs=[
            rows(GLA_KEY_DIM, 0), rows(GLA_KEY_DIM, 1), rows(GLA_VALUE_DIM, 1),
            rows(GLA_VALUE_DIM, 2), rows(LANES, GLA_IN_PAD // LANES - 1),
            state_spec,
            _resident(w_gk2), _resident(b_gk), _resident(norm_w),
        ],
        out_specs=[rows(GLA_VALUE_DIM, 0), state_spec],
        out_shape=[
            jax.ShapeDtypeStruct((nseq, nchunks * CHUNK, GLA_VALUE_DIM), BF16),
            jax.ShapeDtypeStruct((nseq, GLA_HEADS, GLA_DK, GLA_DV), F32),
        ],
        scratch_shapes=[pltpu.VMEM((GLA_SEQS, GLA_HEADS, GLA_DV, GLA_DK), F32)],
        compiler_params=_params("parallel", "arbitrary"),
    )(proj, proj, proj, proj, proj, state, w_gk2, b_gk, norm_w)
    return o.reshape(nseq * nchunks * CHUNK, GLA_VALUE_DIM), s_out


def kernel(x_prompt, x_sample, cache_sb_k, cache_sb_v, state_conv, state_gla, norm_mix, norm_mlp,
           sb_w_qkv, sb_w_o, conv_w_in, conv_w, conv_w_out, gla_w_in, gla_w_gk2, gla_b_gk,
           gla_norm, gla_w_o, mlp_w_up, mlp_w_down, norm_final):
    xp = x_prompt.reshape(N_PROMPT, D_MODEL)
    xs = x_sample.reshape(N_SAMPLE, D_MODEL)
    n_sb = cache_sb_k.shape[0]
    cache_kt = cache_sb_k.transpose(0, 1, 3, 4, 2).reshape(n_sb, DEC_BATCH, D_MODEL, PAST_LEN)
    cache_vt = cache_sb_v.transpose(0, 1, 3, 4, 2).reshape(n_sb, DEC_BATCH, D_MODEL, PAST_LEN)
    sb_kt_p, sb_vt_p, sb_k_s, sb_v_s = [], [], [], []
    conv_p, conv_s, gla_p, gla_s = [], [], [], []

    def vec(a):
        return a.reshape(1, -1)

    for i in range(DEPTH):
        j = i // N_MIXERS
        if i % N_MIXERS == 0:
            w = sb_w_qkv[j]
            wq = (w[:, :D_MODEL] * (SB_HEAD_DIM ** -0.5)).astype(BF16)
            wkv = w[:, D_MODEL:].astype(BF16)
            q_p, kt, vt, ktb, vtb = _sb_qkv_prompt(xp, norm_mix[i], wq, wkv.T)
            q_s, k_s, v_s, kvb_s = _sb_qkv_sample(xs, norm_mix[i], wq, wkv)
            o_p = _sb_prompt(q_p, ktb, vtb)
            o_s = _sb_sample(q_s, kvb_s, cache_kt, cache_vt, j)
            xp, xs = _two_stream(_proj_res_body, "sb_out", [o_p, xp], [o_s, xs],
                                 [sb_w_o[j].astype(BF16)], [(D_MODEL, F32)])
            sb_kt_p.append(kt)
            sb_vt_p.append(vt)
            sb_k_s.append(k_s)
            sb_v_s.append(v_s)
        elif i % N_MIXERS == 1:
            gb_p, u_p, gb_s, u_s = _two_stream(
                _conv_in_body, "conv_in", [xp], [xs],
                [vec(norm_mix[i]), conv_w_in[j].astype(BF16)], [(D_MODEL, F32), (D_MODEL, F32)])
            keep = CONV_WIDTH - 1
            tails_p = u_p.reshape(BATCH, SEQ // CHUNK, CHUNK, D_MODEL)[:, :, CHUNK - keep:]
            prev_p = jnp.concatenate(
                [jnp.zeros((BATCH, 1, keep, D_MODEL), F32), tails_p[:, :-1]], axis=1)
            prev_p = prev_p.reshape(N_PROMPT // CHUNK, keep, D_MODEL)
            xp, xs = _two_stream(_conv_out_body, "conv_out",
                                 [u_p, prev_p, gb_p, xp], [u_s, state_conv[j], gb_s, xs],
                                 [conv_w[j], conv_w_out[j].astype(BF16)], [(D_MODEL, F32)])
            conv_p.append(tails_p[:, -1])
            conv_s.append(u_s.reshape(DEC_BATCH, DEC_SEQ, D_MODEL)[:, DEC_SEQ - keep:])
        else:
            w_in = jnp.pad(gla_w_in[j], ((0, 0), (0, GLA_IN_PAD - GLA_IN_DIM))).astype(BF16)
            proj_p, proj_s = _two_stream(_norm_proj_body, "gla_in", [xp], [xs],
                                         [vec(norm_mix[i]), w_in], [(GLA_IN_PAD, F32)])
            zero_state = jnp.zeros((BATCH, GLA_HEADS, GLA_DK, GLA_DV), F32)
            o_p, s_p = _gla(proj_p, zero_state, gla_w_gk2[j], gla_b_gk[j], gla_norm[j],
                            BATCH, SEQ // CHUNK)
            o_s, s_s = _gla(proj_s, state_gla[j], gla_w_gk2[j], gla_b_gk[j], gla_norm[j],
                            DEC_BATCH, DEC_SEQ // CHUNK)
            xp, xs = _two_stream(_proj_res_body, "gla_out", [o_p, xp], [o_s, xs],
                                 [gla_w_o[j].astype(BF16)], [(D_MODEL, F32)])
            gla_p.append(s_p)
            gla_s.append(s_s)
        xp, xs = _two_stream(
            functools.partial(_mlp_body, final_norm=(i == DEPTH - 1)), "mlp", [xp], [xs],
            [vec(norm_mlp[i]), mlp_w_up[i].astype(BF16), mlp_w_down[i].astype(BF16),
             vec(norm_final)], [(D_MODEL, F32)])

    def prompt_heads(a):
        a = jnp.stack(a).reshape(n_sb, BATCH, SB_HEADS, SB_HEAD_DIM, SEQ)
        return a.transpose(0, 1, 4, 2, 3)

    def sample_heads(a):
        return jnp.stack(a).reshape(n_sb, DEC_BATCH, DEC_SEQ, SB_HEADS, SB_HEAD_DIM)

    return (xp.reshape(BATCH, SEQ, D_MODEL), xs.reshape(DEC_BATCH, DEC_SEQ, D_MODEL),
            prompt_heads(sb_kt_p), prompt_heads(sb_vt_p), sample_heads(sb_k_s), sample_heads(sb_v_s),
            jnp.stack(conv_p), jnp.stack(conv_s), jnp.stack(gla_p), jnp.stack(gla_s))
```

```python
import functools

import jax
import jax.numpy as jnp
from jax import lax
from jax.experimental import pallas as pl
from jax.experimental.pallas import tpu as pltpu

F32 = jnp.float32
BF16 = jnp.bfloat16

D_MODEL = 1024
BATCH = 8
SEQ = 2048
DEPTH = 4
DEC_BATCH = 32
DEC_SEQ = 64
PAST_LEN = 1024
CHUNK = 64
N_MIXERS = 3
SB_HEADS = 16
SB_HEAD_DIM = D_MODEL // SB_HEADS
CONV_WIDTH = 3
GLA_HEADS = 4
GLA_KEY_DIM = D_MODEL // 2
GLA_VALUE_DIM = D_MODEL
GLA_DK = GLA_KEY_DIM // GLA_HEADS
GLA_DV = GLA_VALUE_DIM // GLA_HEADS
GLA_GATE_RANK = 16
GLA_GATE_NORMALIZER = 16.0
GLA_IN_DIM = 2 * GLA_KEY_DIM + 2 * GLA_VALUE_DIM + GLA_GATE_RANK
MLP_HIDDEN = 4 * D_MODEL
NORM_EPS = 1e-6

N_PROMPT = BATCH * SEQ
N_SAMPLE = DEC_BATCH * DEC_SEQ

LANES = 128
GLA_IN_PAD = 25 * LANES
TOKEN_TILE = 512
P_TILES = N_PROMPT // TOKEN_TILE
S_TILES = N_SAMPLE // TOKEN_TILE
COL_CHUNK = 512
SB_TILE = 256
SB_PAIR = 2 * SB_HEAD_DIM
SB_PAIRS = D_MODEL // SB_PAIR
SB_SAMPLE_PAIRS = 2
GLA_SEQS = 2
GLA_SUB = 16
VMEM_LIMIT = 56 * 1024 * 1024
LOG2E = 1.4426950408889634
SB_DEAD_LOG2 = 160.0


def _params(*sem):
    return pltpu.CompilerParams(dimension_semantics=sem, vmem_limit_bytes=VMEM_LIMIT)


def _rms(x, gain):
    ms = jnp.mean(x * x, axis=-1, keepdims=True)
    return x * lax.rsqrt(ms + NORM_EPS) * gain


def _split_bf16(x, passes):
    parts = []
    r = x
    for _ in range(passes):
        h = r.astype(BF16)
        parts.append(h)
        r = r - h.astype(F32)
    return parts


def _resident(arr):
    nd = arr.ndim
    return pl.BlockSpec(arr.shape, lambda *_: (0,) * nd, pipeline_mode=pl.Buffered(1))


def _two_stream(body, name, ins_p, ins_s, consts, outs):
    n_in, n_c, n_out = len(ins_p), len(consts), len(outs)

    def kern(*refs):
        p_in, s_in = refs[:n_in], refs[n_in:2 * n_in]
        c = refs[2 * n_in:2 * n_in + n_c]
        p_out = refs[2 * n_in + n_c:2 * n_in + n_c + n_out]
        s_out = refs[2 * n_in + n_c + n_out:]
        i = pl.program_id(0)

        @pl.when(i < P_TILES)
        def _():
            body(p_in, c, p_out)

        @pl.when(i >= P_TILES)
        def _():
            body(s_in, c, s_out)

    def p_idx(i):
        return jnp.minimum(i, P_TILES - 1)

    def s_idx(i):
        return jnp.maximum(i - P_TILES, 0)

    def spec(arr, tiles, idx):
        block = (arr.shape[0] // tiles,) + arr.shape[1:]
        zeros = (0,) * (arr.ndim - 1)
        return pl.BlockSpec(block, lambda i: (idx(i),) + zeros)

    def out_spec(width, idx):
        return pl.BlockSpec((TOKEN_TILE, width), lambda i: (idx(i), 0))

    return pl.pallas_call(
        kern,
        name=name,
        grid=(P_TILES + S_TILES,),
        in_specs=([spec(a, P_TILES, p_idx) for a in ins_p] + [spec(a, S_TILES, s_idx) for a in ins_s]
                  + [_resident(a) for a in consts]),
        out_specs=([out_spec(w, p_idx) for w, _ in outs] + [out_spec(w, s_idx) for w, _ in outs]),
        out_shape=([jax.ShapeDtypeStruct((N_PROMPT, w), dt) for w, dt in outs]
                   + [jax.ShapeDtypeStruct((N_SAMPLE, w), dt) for w, dt in outs]),
        compiler_params=_params("arbitrary"),
    )(*ins_p, *ins_s, *consts)


def _norm_proj_body(ins, consts, outs):
    (x_ref,), (g_ref, w_ref), (o_ref,) = ins, consts, outs
    xn = _rms(x_ref[...], g_ref[...]).astype(BF16)
    dout = o_ref.shape[1]
    for lo in range(0, dout, COL_CHUNK):
        cols = slice(lo, min(lo + COL_CHUNK, dout))
        o_ref[:, cols] = jnp.dot(xn, w_ref[:, cols], preferred_element_type=F32)


def _conv_in_body(ins, consts, outs):
    (x_ref,), (g_ref, w_ref), (gb_ref, u_ref) = ins, consts, outs
    xn = _rms(x_ref[...], g_ref[...]).astype(BF16)
    for lo in range(0, D_MODEL, COL_CHUNK):
        cols = slice(lo, lo + COL_CHUNK)
        cols_c = slice(D_MODEL + lo, D_MODEL + lo + COL_CHUNK)
        cols_x = slice(2 * D_MODEL + lo, 2 * D_MODEL + lo + COL_CHUNK)
        gb_ref[:, cols] = jnp.dot(xn, w_ref[:, cols], preferred_element_type=F32)
        gc = jnp.dot(xn, w_ref[:, cols_c], preferred_element_type=F32)
        xp = jnp.dot(xn, w_ref[:, cols_x], preferred_element_type=F32)
        u_ref[:, cols] = gc * xp


def _conv_out_body(ins, consts, outs):
    (u_ref, prev_ref, gb_ref, r_ref), (cw_ref, w_ref), (o_ref,) = ins, consts, outs
    units = TOKEN_TILE // CHUNK
    u = u_ref[...]
    prev = prev_ref[...]
    p0 = jnp.broadcast_to(prev[:, 0:1, :], (units, CHUNK, D_MODEL)).reshape(TOKEN_TILE, D_MODEL)
    p1 = jnp.broadcast_to(prev[:, 1:2, :], (units, CHUNK, D_MODEL)).reshape(TOKEN_TILE, D_MODEL)
    t = lax.broadcasted_iota(jnp.int32, (TOKEN_TILE, 1), 0) % CHUNK
    s1 = jnp.where(t == 0, p1, pltpu.roll(u, 1, axis=0))
    s2 = jnp.where(t == 0, p0, jnp.where(t == 1, p1, pltpu.roll(u, 2, axis=0)))
    cw = cw_ref[...]
    y = cw[0:1, :] * s2 + cw[1:2, :] * s1 + cw[2:3, :] * u
    a = (gb_ref[...] * y).astype(BF16)
    o_ref[...] = r_ref[...] + jnp.dot(a, w_ref[...], preferred_element_type=F32)


def _mlp_apply(x, g_ref, wu_ref, wd_ref, gf_ref, final_norm):
    xn = _rms(x, g_ref[...]).astype(BF16)
    acc = x
    for lo in range(0, MLP_HIDDEN, COL_CHUNK):
        cols = slice(lo, lo + COL_CHUNK)
        h = jnp.maximum(jnp.dot(xn, wu_ref[:, cols], preferred_element_type=F32), 0.0)
        acc = acc + jnp.dot((h * h).astype(BF16), wd_ref[cols, :], preferred_element_type=F32)
    return _rms(acc, gf_ref[...]) if final_norm else acc


def _mlp_body(ins, consts, outs, *, final_norm):
    (x_ref,), (g_ref, wu_ref, wd_ref, gf_ref), (o_ref,) = ins, consts, outs
    o_ref[...] = _mlp_apply(x_ref[...], g_ref, wu_ref, wd_ref, gf_ref, final_norm)


def _proj_mlp_body(ins, consts, outs, *, final_norm):
    (a_ref, r_ref), (w_ref, g_ref, wu_ref, wd_ref, gf_ref), (o_ref,) = ins, consts, outs
    x = r_ref[...] + jnp.dot(a_ref[...], w_ref[...], preferred_element_type=F32)
    o_ref[...] = _mlp_apply(x, g_ref, wu_ref, wd_ref, gf_ref, final_norm)


def _sb_qkv_prompt_kernel(x_ref, g_ref, wq_ref, wkvt_ref, q_ref, kt_ref, vt_ref, ktb_ref, vtb_ref):
    xn = _rms(x_ref[...], g_ref[...]).astype(BF16)
    for lo in range(0, D_MODEL, COL_CHUNK):
        cols = slice(lo, lo + COL_CHUNK)
        q_ref[:, cols] = jnp.dot(xn, wq_ref[:, cols], preferred_element_type=F32).astype(BF16)
    for out_ref, outb_ref, base in ((kt_ref, ktb_ref, 0), (vt_ref, vtb_ref, D_MODEL)):
        for lo in range(0, D_MODEL, COL_CHUNK):
            rows = slice(lo, lo + COL_CHUNK)
            w_rows = slice(base + lo, base + lo + COL_CHUNK)
            t = lax.dot_general(wkvt_ref[w_rows, :], xn, (((1,), (1,)), ((), ())),
                                preferred_element_type=F32)
            out_ref[rows, :] = t
            for kt in range(TOKEN_TILE // SB_TILE):
                outb_ref[kt, rows, :] = t[:, kt * SB_TILE:(kt + 1) * SB_TILE].astype(BF16)


def _sb_qkv_prompt(x, gain, wq, wkvt):
    per_seq = SEQ // TOKEN_TILE
    key_tiles = TOKEN_TILE // SB_TILE
    kt_spec = pl.BlockSpec((None, D_MODEL, TOKEN_TILE), lambda i: (i // per_seq, 0, i % per_seq))
    ktb_spec = pl.BlockSpec((None, key_tiles, D_MODEL, SB_TILE),
                            lambda i: (i // per_seq, i % per_seq, 0, 0))
    kt_shape = jax.ShapeDtypeStruct((BATCH, D_MODEL, SEQ), F32)
    ktb_shape = jax.ShapeDtypeStruct((BATCH, SEQ // SB_TILE, D_MODEL, SB_TILE), BF16)
    gain = gain.reshape(1, D_MODEL)
    return pl.pallas_call(
        _sb_qkv_prompt_kernel,
        name="sb_qkv_prompt",
        grid=(P_TILES,),
        in_specs=[pl.BlockSpec((TOKEN_TILE, D_MODEL), lambda i: (i, 0)),
                  _resident(gain), _resident(wq), _resident(wkvt)],
        out_specs=[pl.BlockSpec((TOKEN_TILE, D_MODEL), lambda i: (i, 0)),
                   kt_spec, kt_spec, ktb_spec, ktb_spec],
        out_shape=[jax.ShapeDtypeStruct((N_PROMPT, D_MODEL), BF16),
                   kt_shape, kt_shape, ktb_shape, ktb_shape],
        compiler_params=_params("parallel"),
    )(x, gain, wq, wkvt)


def _sb_qkv_sample_kernel(x_ref, g_ref, wq_ref, wkv_ref, q_ref, k_ref, v_ref, kvb_ref):
    xn = _rms(x_ref[...], g_ref[...]).astype(BF16)
    for lo in range(0, D_MODEL, COL_CHUNK):
        cols = slice(lo, lo + COL_CHUNK)
        q_ref[:, cols] = jnp.dot(xn, wq_ref[:, cols], preferred_element_type=F32).astype(BF16)
    for out_ref, base in ((k_ref, 0), (v_ref, D_MODEL)):
        for lo in range(0, D_MODEL, COL_CHUNK):
            w_cols = slice(base + lo, base + lo + COL_CHUNK)
            t = jnp.dot(xn, wkv_ref[:, w_cols], preferred_element_type=F32)
            out_ref[:, lo:lo + COL_CHUNK] = t
            kvb_ref[:, w_cols] = t.astype(BF16)


def _sb_qkv_sample(x, gain, wq, wkv):
    tok = pl.BlockSpec((TOKEN_TILE, D_MODEL), lambda i: (i, 0))
    gain = gain.reshape(1, D_MODEL)
    return pl.pallas_call(
        _sb_qkv_sample_kernel,
        name="sb_qkv_sample",
        grid=(S_TILES,),
        in_specs=[tok, _resident(gain), _resident(wq), _resident(wkv)],
        out_specs=[tok, tok, tok, pl.BlockSpec((TOKEN_TILE, 2 * D_MODEL), lambda i: (i, 0))],
        out_shape=[jax.ShapeDtypeStruct((N_SAMPLE, D_MODEL), BF16),
                   jax.ShapeDtypeStruct((N_SAMPLE, D_MODEL), F32),
                   jax.ShapeDtypeStruct((N_SAMPLE, D_MODEL), F32),
                   jax.ShapeDtypeStruct((N_SAMPLE, 2 * D_MODEL), BF16)],
        compiler_params=_params("parallel"),
    )(x, gain, wq, wkv)


def _sb_tile(qst, k, v, upper2, mask, carry, acc, key_minor):
    nt = (((1,), (1,)), ((), ()))
    if key_minor:
        z = jnp.dot(qst, k, preferred_element_type=F32)
    else:
        z = lax.dot_general(qst, k, nt, preferred_element_type=F32)
    z2 = z * LOG2E
    sp = jnp.maximum(z2, 0.0) + jnp.log2(1.0 + jnp.exp2(-jnp.abs(z2)))
    spm = sp if mask is None else jnp.where(mask, sp, 0.0)
    hi, lo = _split_bf16(spm, 2)
    below = jnp.dot(jnp.concatenate([hi, lo], axis=1), upper2, preferred_element_type=F32) + carry
    a = jnp.exp2(z2 - sp - below)
    if mask is not None:
        a = jnp.where(mask, a, 0.0)
    a = a.astype(BF16)
    if key_minor:
        acc = acc + lax.dot_general(a, v, nt, preferred_element_type=F32)
    else:
        acc = acc + jnp.dot(a, v, preferred_element_type=F32)
    carry = below[:, 0:1] + spm[:, 0:1]
    return carry, acc


def _suffix_matrix2(n):
    s = lax.broadcasted_iota(jnp.int32, (2 * n, n), 0) % n
    j = lax.broadcasted_iota(jnp.int32, (2 * n, n), 1)
    return (s > j).astype(BF16)


def _split_heads(q):
    lane = lax.broadcasted_iota(jnp.int32, (1, SB_PAIR), 1)
    zero = jnp.zeros_like(q)
    return jnp.where(lane < SB_HEAD_DIM, q, zero), jnp.where(lane >= SB_HEAD_DIM, q, zero)


def _stack_heads(q):
    return jnp.concatenate(_split_heads(q), axis=0)


def _unstack_heads(acc):
    lane = lax.broadcasted_iota(jnp.int32, (1, SB_PAIR), 1)
    m = acc.shape[0] // 2
    return jnp.where(lane < SB_HEAD_DIM, acc[:m], acc[m:])


def _causal_mask(tq, tk):
    t = lax.broadcasted_iota(jnp.int32, (2 * tq, tk), 0) % tq
    s = lax.broadcasted_iota(jnp.int32, (2 * tq, tk), 1)
    return s < t


def _sb_prompt_kernel(q_ref, kt_ref, vt_ref, o_ref):
    i = pl.program_id(2)
    qst = _stack_heads(q_ref[...])
    upper2 = _suffix_matrix2(SB_TILE)
    causal = _causal_mask(SB_TILE, SB_TILE)
    zero_c = jnp.zeros((2 * SB_TILE, 1), F32)
    zero_a = jnp.zeros((2 * SB_TILE, SB_PAIR), F32)

    @pl.when(i == 0)
    def _():
        _, acc = _sb_tile(qst, kt_ref[0], vt_ref[0], upper2, causal, zero_c, zero_a, True)
        o_ref[...] = _unstack_heads(acc).astype(o_ref.dtype)

    @pl.when(i > 0)
    def _():
        carry, acc = _sb_tile(qst, kt_ref[i], vt_ref[i], upper2, causal, zero_c, zero_a, True)
        carry, acc = _sb_tile(qst, kt_ref[i - 1], vt_ref[i - 1], upper2, None, carry, acc, True)

        def live(c):
            return (c[0] < i) & (jnp.min(c[1]) < SB_DEAD_LOG2)

        def body(c):
            j = i - 1 - c[0]
            return (c[0] + 1,) + _sb_tile(qst, kt_ref[j], vt_ref[j], upper2, None, c[1], c[2], True)

        _, _, acc = lax.while_loop(live, body, (jnp.int32(1), carry, acc))
        o_ref[...] = _unstack_heads(acc).astype(o_ref.dtype)


def _sb_prompt(q, ktb, vtb):
    nq = SEQ // SB_TILE
    kv_spec = pl.BlockSpec((None, nq, SB_PAIR, SB_TILE), lambda b, p, i: (b, 0, p, 0))
    return pl.pallas_call(
        _sb_prompt_kernel,
        name="sb_prompt",
        grid=(BATCH, SB_PAIRS, nq),
        in_specs=[pl.BlockSpec((SB_TILE, SB_PAIR), lambda b, p, i: (b * nq + i, p)),
                  kv_spec, kv_spec],
        out_specs=pl.BlockSpec((SB_TILE, SB_PAIR), lambda b, p, i: (b * nq + i, p)),
        out_shape=jax.ShapeDtypeStruct((N_PROMPT, D_MODEL), BF16),
        compiler_params=_params("parallel", "parallel", "arbitrary"),
    )(q, ktb, vtb)


def _sb_sample_kernel(q_ref, kn_ref, vn_ref, ck_ref, cv_ref, o_ref, carry_ref, acc_ref):
    upper_new = _suffix_matrix2(DEC_SEQ)
    upper2 = _suffix_matrix2(SB_TILE)
    causal = _causal_mask(DEC_SEQ, DEC_SEQ)
    n_cache = PAST_LEN // SB_TILE

    def cache_tile(pp, j, qst, carry, acc):
        feats = slice(pp * SB_PAIR, (pp + 1) * SB_PAIR)
        keys = slice(j * SB_TILE, (j + 1) * SB_TILE)
        return _sb_tile(qst, ck_ref[feats, keys].astype(BF16), cv_ref[feats, keys].astype(BF16),
                        upper2, None, carry, acc, True)

    qsts = []
    for pp in range(SB_SAMPLE_PAIRS):
        lanes = slice(pp * SB_PAIR, (pp + 1) * SB_PAIR)
        qst = _stack_heads(q_ref[:, lanes])
        carry = jnp.zeros((2 * DEC_SEQ, 1), F32)
        acc = jnp.zeros((2 * DEC_SEQ, SB_PAIR), F32)
        carry, acc = _sb_tile(qst, kn_ref[:, lanes], vn_ref[:, lanes], upper_new, causal,
                              carry, acc, False)
        carry_ref[pp], acc_ref[pp] = cache_tile(pp, n_cache - 1, qst, carry, acc)
        qsts.append(qst)

    for j in reversed(range(n_cache - 1)):
        @pl.when(jnp.min(carry_ref[...]) < SB_DEAD_LOG2)
        def _():
            for pp in range(SB_SAMPLE_PAIRS):
                carry_ref[pp], acc_ref[pp] = cache_tile(pp, j, qsts[pp], carry_ref[pp], acc_ref[pp])

    for pp in range(SB_SAMPLE_PAIRS):
        lanes = slice(pp * SB_PAIR, (pp + 1) * SB_PAIR)
        o_ref[:, lanes] = _unstack_heads(acc_ref[pp]).astype(o_ref.dtype)


def _sb_sample(q, kvb, cache_kt, cache_vt, layer):
    width = SB_SAMPLE_PAIRS * SB_PAIR
    steps = D_MODEL // width
    cache_spec = pl.BlockSpec((None, None, width, PAST_LEN), lambda b, p: (layer, b, p, 0))
    return pl.pallas_call(
        _sb_sample_kernel,
        name="sb_sample",
        grid=(DEC_BATCH, steps),
        in_specs=[pl.BlockSpec((DEC_SEQ, width), lambda b, p: (b, p)),
                  pl.BlockSpec((DEC_SEQ, width), lambda b, p: (b, p)),
                  pl.BlockSpec((DEC_SEQ, width), lambda b, p: (b, steps + p)),
                  cache_spec, cache_spec],
        out_specs=pl.BlockSpec((DEC_SEQ, width), lambda b, p: (b, p)),
        out_shape=jax.ShapeDtypeStruct((N_SAMPLE, D_MODEL), BF16),
        scratch_shapes=[pltpu.VMEM((SB_SAMPLE_PAIRS, 2 * DEC_SEQ, 1), F32),
                        pltpu.VMEM((SB_SAMPLE_PAIRS, 2 * DEC_SEQ, SB_PAIR), F32)],
        compiler_params=_params("parallel", "parallel"),
    )(q, kvb, kvb, cache_kt, cache_vt)


def _gla_kernel(q_ref, k_ref, v_ref, go_ref, gl_ref, s0_ref, wgk_ref, bgk_ref, nw_ref,
                o_ref, sout_ref, st_ref, *, nchunks):
    if nchunks > 1:
        @pl.when(pl.program_id(1) == 0)
        def _():
            for sq in range(GLA_SEQS):
                for h in range(GLA_HEADS):
                    st_ref[sq, h] = s0_ref[sq, h].T

    states = [_gla_seq(q_ref.at[sq], k_ref.at[sq], v_ref.at[sq], go_ref.at[sq], gl_ref.at[sq],
                       s0_ref.at[sq], wgk_ref, bgk_ref, nw_ref, o_ref.at[sq], st_ref.at[sq],
                       nchunks)
              for sq in range(GLA_SEQS)]

    def write_states():
        for sq in range(GLA_SEQS):
            for h in range(GLA_HEADS):
                sout_ref[sq, h] = states[sq][h].T

    if nchunks > 1:
        pl.when(pl.program_id(1) == nchunks - 1)(write_states)
    else:
        write_states()


def _gla_seq(q_ref, k_ref, v_ref, go_ref, gl_ref, s0_ref, wgk_ref, bgk_ref, nw_ref,
             o_ref, st_ref, nchunks):
    nt = (((1,), (1,)), ((), ()))
    q = q_ref[...] * (GLA_DK ** -0.5)
    k = k_ref[...]
    gate_in = jnp.dot(gl_ref[...].astype(BF16), wgk_ref[...].astype(BF16),
                      preferred_element_type=F32) + bgk_ref[...]
    g = (jnp.minimum(gate_in, 0.0) - jnp.log1p(jnp.exp(-jnp.abs(gate_in)))) / GLA_GATE_NORMALIZER

    t_i = lax.broadcasted_iota(jnp.int32, (CHUNK, CHUNK), 0)
    s_i = lax.broadcasted_iota(jnp.int32, (CHUNK, CHUNK), 1)
    lower = (s_i <= t_i).astype(BF16)
    b = jnp.zeros((CHUNK, GLA_KEY_DIM), F32)
    for part in _split_bf16(g, 3):
        b = b + jnp.dot(lower, part, preferred_element_type=F32)

    qe = (q * jnp.exp(b)).astype(BF16)
    b_last = b[CHUNK - 1:CHUNK]
    kd = (k * jnp.exp(b_last - b)).astype(BF16)
    decay_last = jnp.exp(b_last)

    key_row = lax.broadcasted_iota(jnp.int32, (CHUNK, 1), 0)
    sub_row = lax.broadcasted_iota(jnp.int32, (GLA_SUB, 1), 0)
    key_lane = lax.broadcasted_iota(jnp.int32, (1, CHUNK), 1)
    n_sub = CHUNK // GLA_SUB
    qt, kt = [None] * n_sub, [None] * n_sub
    for blk in range(1, n_sub):
        lo = blk * GLA_SUB
        ref = b[lo - 1:lo]
        qt[blk] = (q[lo:lo + GLA_SUB] * jnp.exp(b[lo:lo + GLA_SUB] - ref)).astype(BF16)
        kt[blk] = (k * jnp.exp(jnp.where(key_row < lo, ref - b, -jnp.inf))).astype(BF16)

    feat_head = lax.broadcasted_iota(jnp.int32, (GLA_KEY_DIM, GLA_HEADS * CHUNK), 0) // GLA_DK
    out_head = lax.broadcasted_iota(jnp.int32, (GLA_KEY_DIM, GLA_HEADS * CHUNK), 1) // CHUNK
    head_sum = (feat_head == out_head).astype(BF16)
    out_key = lax.broadcasted_iota(jnp.int32, (1, GLA_HEADS * CHUNK), 1) % CHUNK
    diag_rows = []
    for blk in range(n_sub):
        rows = slice(blk * GLA_SUB, (blk + 1) * GLA_SUB)
        qi, ki, bi = q[rows], k[rows], b[rows]
        prods = []
        for s in range(GLA_SUB):
            diff = jnp.where(sub_row >= s, bi - bi[s:s + 1], -jnp.inf)
            prods.append((qi * (ki[s:s + 1] * jnp.exp(diff))).astype(BF16))
        sums = jnp.dot(jnp.concatenate(prods, axis=0), head_sum, preferred_element_type=F32)
        a_blk = jnp.zeros((GLA_SUB, GLA_HEADS * CHUNK), F32)
        for s in range(GLA_SUB):
            a_blk = jnp.where(out_key == blk * GLA_SUB + s,
                              sums[s * GLA_SUB:(s + 1) * GLA_SUB], a_blk)
        diag_rows.append(a_blk)
    a_diag = jnp.concatenate(diag_rows, axis=0)

    nw = nw_ref[...]
    new_states = []
    for h in range(GLA_HEADS):
        kl = slice(h * GLA_DK, (h + 1) * GLA_DK)
        vl = slice(h * GLA_DV, (h + 1) * GLA_DV)
        v = v_ref[:, vl].astype(BF16)
        st = st_ref[h] if nchunks > 1 else s0_ref[h].T
        a_rows = [jnp.zeros((GLA_SUB, CHUNK), F32)]
        for blk in range(1, n_sub):
            a_rows.append(lax.dot_general(qt[blk][:, kl], kt[blk][:, kl], nt,
                                          preferred_element_type=F32))
        a = (jnp.concatenate(a_rows, axis=0) + a_diag[:, h * CHUNK:(h + 1) * CHUNK]).astype(BF16)
        o = (lax.dot_general(qe[:, kl], st.astype(BF16), nt, preferred_element_type=F32)
             + jnp.dot(a, v, preferred_element_type=F32))
        st_new = st * decay_last[:, kl] + lax.dot_general(
            v, kd[:, kl], (((0,), (0,)), ((), ())), preferred_element_type=F32)
        if nchunks > 1:
            st_ref[h] = st_new
        new_states.append(st_new)

        o = o * lax.rsqrt(jnp.mean(o * o, axis=-1, keepdims=True) + NORM_EPS) * nw
        go = go_ref[:, vl]
        o_ref[:, vl] = (o * (go * jax.nn.sigmoid(go))).astype(o_ref.dtype)
    return new_states


def _gla(proj, state, w_gk2, b_gk, norm_w, nseq, nchunks):
    def rows(width, col):
        return pl.BlockSpec((GLA_SEQS, CHUNK, width), lambda b, c: (b, c, col))

    state_spec = pl.BlockSpec((GLA_SEQS, GLA_HEADS, GLA_DK, GLA_DV), lambda b, c: (b, 0, 0, 0))
    w_gk2 = jnp.pad(w_gk2, ((0, LANES - GLA_GATE_RANK), (0, 0)))
    b_gk = b_gk.reshape(1, GLA_KEY_DIM)
    norm_w = norm_w.reshape(1, GLA_DV)
    proj = proj.reshape(nseq, nchunks * CHUNK, GLA_IN_PAD)
    o, s_out = pl.pallas_call(
        functools.partial(_gla_kernel, nchunks=nchunks),
        name="gla",
        grid=(nseq // GLA_SEQS, nchunks),
        in_specs=[
            rows(GLA_KEY_DIM, 0), rows(GLA_KEY_DIM, 1), rows(GLA_VALUE_DIM, 1),
            rows(GLA_VALUE_DIM, 2), rows(LANES, GLA_IN_PAD // LANES - 1),
            state_spec,
            _resident(w_gk2), _resident(b_gk), _resident(norm_w),
        ],
        out_specs=[rows(GLA_VALUE_DIM, 0), state_spec],
        out_shape=[
            jax.ShapeDtypeStruct((nseq, nchunks * CHUNK, GLA_VALUE_DIM), BF16),
            jax.ShapeDtypeStruct((nseq, GLA_HEADS, GLA_DK, GLA_DV), F32),
        ],
        scratch_shapes=[pltpu.VMEM((GLA_SEQS, GLA_HEADS, GLA_DV, GLA_DK), F32)],
        compiler_params=_params("parallel", "arbitrary"),
    )(proj, proj, proj, proj, proj, state, w_gk2, b_gk, norm_w)
    return o.reshape(nseq * nchunks * CHUNK, GLA_VALUE_DIM), s_out


def kernel(x_prompt, x_sample, cache_sb_k, cache_sb_v, state_conv, state_gla, norm_mix, norm_mlp,
           sb_w_qkv, sb_w_o, conv_w_in, conv_w, conv_w_out, gla_w_in, gla_w_gk2, gla_b_gk,
           gla_norm, gla_w_o, mlp_w_up, mlp_w_down, norm_final):
    xp = x_prompt.reshape(N_PROMPT, D_MODEL)
    xs = x_sample.reshape(N_SAMPLE, D_MODEL)
    n_sb = cache_sb_k.shape[0]
    cache_kt = cache_sb_k.transpose(0, 1, 3, 4, 2).reshape(n_sb, DEC_BATCH, D_MODEL, PAST_LEN)
    cache_vt = cache_sb_v.transpose(0, 1, 3, 4, 2).reshape(n_sb, DEC_BATCH, D_MODEL, PAST_LEN)
    sb_kt_p, sb_vt_p, sb_k_s, sb_v_s = [], [], [], []
    conv_p, conv_s, gla_p, gla_s = [], [], [], []

    def vec(a):
        return a.reshape(1, -1)

    for i in range(DEPTH):
        j = i // N_MIXERS
        final = i == DEPTH - 1
        mlp_consts = [vec(norm_mlp[i]), mlp_w_up[i].astype(BF16), mlp_w_down[i].astype(BF16),
                      vec(norm_final)]

        def proj_mlp(name, a_p, a_s, w_o):
            return _two_stream(functools.partial(_proj_mlp_body, final_norm=final), name,
                               [a_p, xp], [a_s, xs], [w_o.astype(BF16)] + mlp_consts,
                               [(D_MODEL, F32)])

        if i % N_MIXERS == 0:
            w = sb_w_qkv[j]
            wq = (w[:, :D_MODEL] * (SB_HEAD_DIM ** -0.5)).astype(BF16)
            wkv = w[:, D_MODEL:].astype(BF16)
            q_p, kt, vt, ktb, vtb = _sb_qkv_prompt(xp, norm_mix[i], wq, wkv.T)
            q_s, k_s, v_s, kvb_s = _sb_qkv_sample(xs, norm_mix[i], wq, wkv)
            o_p = _sb_prompt(q_p, ktb, vtb)
            o_s = _sb_sample(q_s, kvb_s, cache_kt, cache_vt, j)
            xp, xs = proj_mlp("sb_out_mlp", o_p, o_s, sb_w_o[j])
            sb_kt_p.append(kt)
            sb_vt_p.append(vt)
            sb_k_s.append(k_s)
            sb_v_s.append(v_s)
        elif i % N_MIXERS == 1:
            gb_p, u_p, gb_s, u_s = _two_stream(
                _conv_in_body, "conv_in", [xp], [xs],
                [vec(norm_mix[i]), conv_w_in[j].astype(BF16)], [(D_MODEL, F32), (D_MODEL, F32)])
            keep = CONV_WIDTH - 1
            tails_p = u_p.reshape(BATCH, SEQ // CHUNK, CHUNK, D_MODEL)[:, :, CHUNK - keep:]
            prev_p = jnp.concatenate(
                [jnp.zeros((BATCH, 1, keep, D_MODEL), F32), tails_p[:, :-1]], axis=1)
            prev_p = prev_p.reshape(N_PROMPT // CHUNK, keep, D_MODEL)
            xp, xs = _two_stream(_conv_out_body, "conv_out",
                                 [u_p, prev_p, gb_p, xp], [u_s, state_conv[j], gb_s, xs],
                                 [conv_w[j], conv_w_out[j].astype(BF16)], [(D_MODEL, F32)])
            xp, xs = _two_stream(functools.partial(_mlp_body, final_norm=final), "mlp",
                                 [xp], [xs], mlp_consts, [(D_MODEL, F32)])
            conv_p.append(tails_p[:, -1])
            conv_s.append(u_s.reshape(DEC_BATCH, DEC_SEQ, D_MODEL)[:, DEC_SEQ - keep:])
        else:
            w_in = jnp.pad(gla_w_in[j], ((0, 0), (0, GLA_IN_PAD - GLA_IN_DIM))).astype(BF16)
            proj_p, proj_s = _two_stream(_norm_proj_body, "gla_in", [xp], [xs],
                                         [vec(norm_mix[i]), w_in], [(GLA_IN_PAD, F32)])
            zero_state = jnp.zeros((BATCH, GLA_HEADS, GLA_DK, GLA_DV), F32)
            o_p, s_p = _gla(proj_p, zero_state, gla_w_gk2[j], gla_b_gk[j], gla_norm[j],
                            BATCH, SEQ // CHUNK)
            o_s, s_s = _gla(proj_s, state_gla[j], gla_w_gk2[j], gla_b_gk[j], gla_norm[j],
                            DEC_BATCH, DEC_SEQ // CHUNK)
            xp, xs = proj_mlp("gla_out_mlp", o_p, o_s, gla_w_o[j])
            gla_p.append(s_p)
            gla_s.append(s_s)

    def prompt_heads(a):
        a = jnp.stack(a).reshape(n_sb, BATCH, SB_HEADS, SB_HEAD_DIM, SEQ)
        return a.transpose(0, 1, 4, 2, 3)

    def sample_heads(a):
        return jnp.stack(a).reshape(n_sb, DEC_BATCH, DEC_SEQ, SB_HEADS, SB_HEAD_DIM)

    return (xp.reshape(BATCH, SEQ, D_MODEL), xs.reshape(DEC_BATCH, DEC_SEQ, D_MODEL),
            prompt_heads(sb_kt_p), prompt_heads(sb_vt_p), sample_heads(sb_k_s), sample_heads(sb_v_s),
            jnp.stack(conv_p), jnp.stack(conv_s), jnp.stack(gla_p), jnp.stack(gla_s))
```

```python
import functools

import jax
import jax.numpy as jnp
from jax import lax
from jax.experimental import pallas as pl
from jax.experimental.pallas import tpu as pltpu

F32 = jnp.float32
BF16 = jnp.bfloat16

D_MODEL = 1024
BATCH = 8
SEQ = 2048
DEPTH = 4
DEC_BATCH = 32
DEC_SEQ = 64
PAST_LEN = 1024
CHUNK = 64
N_MIXERS = 3
SB_HEADS = 16
SB_HEAD_DIM = D_MODEL // SB_HEADS
CONV_WIDTH = 3
GLA_HEADS = 4
GLA_KEY_DIM = D_MODEL // 2
GLA_VALUE_DIM = D_MODEL
GLA_DK = GLA_KEY_DIM // GLA_HEADS
GLA_DV = GLA_VALUE_DIM // GLA_HEADS
GLA_GATE_RANK = 16
GLA_GATE_NORMALIZER = 16.0
GLA_IN_DIM = 2 * GLA_KEY_DIM + 2 * GLA_VALUE_DIM + GLA_GATE_RANK
MLP_HIDDEN = 4 * D_MODEL
NORM_EPS = 1e-6

N_PROMPT = BATCH * SEQ
N_SAMPLE = DEC_BATCH * DEC_SEQ

LANES = 128
GLA_IN_PAD = 25 * LANES
TOKEN_TILE = 512
P_TILES = N_PROMPT // TOKEN_TILE
S_TILES = N_SAMPLE // TOKEN_TILE
COL_CHUNK = 512
SB_TILE = 256
SB_PAIR = 2 * SB_HEAD_DIM
SB_PAIRS = D_MODEL // SB_PAIR
SB_SUFFIX_PASSES = 1
SB_PROMPT_PAIRS = 2
SB_SAMPLE_PAIRS = 4
GLA_SEQS = 2
GLA_SUB = 16
VMEM_LIMIT = 56 * 1024 * 1024
LOG2E = 1.4426950408889634
SB_DEAD_LOG2 = 160.0


def _params(*sem):
    return pltpu.CompilerParams(dimension_semantics=sem, vmem_limit_bytes=VMEM_LIMIT)


def _rms(x, gain):
    ms = jnp.mean(x * x, axis=-1, keepdims=True)
    return x * lax.rsqrt(ms + NORM_EPS) * gain


def _split_bf16(x, passes):
    parts = []
    r = x
    for _ in range(passes):
        h = r.astype(BF16)
        parts.append(h)
        r = r - h.astype(F32)
    return parts


def _resident(arr):
    nd = arr.ndim
    return pl.BlockSpec(arr.shape, lambda *_: (0,) * nd, pipeline_mode=pl.Buffered(1))


def _two_stream(body, name, ins_p, ins_s, consts, outs):
    n_in, n_c, n_out = len(ins_p), len(consts), len(outs)

    def kern(*refs):
        p_in, s_in = refs[:n_in], refs[n_in:2 * n_in]
        c = refs[2 * n_in:2 * n_in + n_c]
        p_out = refs[2 * n_in + n_c:2 * n_in + n_c + n_out]
        s_out = refs[2 * n_in + n_c + n_out:]
        i = pl.program_id(0)

        @pl.when(i < P_TILES)
        def _():
            body(p_in, c, p_out)

        @pl.when(i >= P_TILES)
        def _():
            body(s_in, c, s_out)

    def p_idx(i):
        return jnp.minimum(i, P_TILES - 1)

    def s_idx(i):
        return jnp.maximum(i - P_TILES, 0)

    def spec(arr, tiles, idx):
        block = (arr.shape[0] // tiles,) + arr.shape[1:]
        zeros = (0,) * (arr.ndim - 1)
        return pl.BlockSpec(block, lambda i: (idx(i),) + zeros)

    def out_spec(width, idx):
        return pl.BlockSpec((TOKEN_TILE, width), lambda i: (idx(i), 0))

    return pl.pallas_call(
        kern,
        name=name,
        grid=(P_TILES + S_TILES,),
        in_specs=([spec(a, P_TILES, p_idx) for a in ins_p] + [spec(a, S_TILES, s_idx) for a in ins_s]
                  + [_resident(a) for a in consts]),
        out_specs=([out_spec(w, p_idx) for w, _ in outs] + [out_spec(w, s_idx) for w, _ in outs]),
        out_shape=([jax.ShapeDtypeStruct((N_PROMPT, w), dt) for w, dt in outs]
                   + [jax.ShapeDtypeStruct((N_SAMPLE, w), dt) for w, dt in outs]),
        compiler_params=_params("arbitrary"),
    )(*ins_p, *ins_s, *consts)


def _norm_proj_body(ins, consts, outs):
    (x_ref,), (g_ref, w_ref), (o_ref,) = ins, consts, outs
    xn = _rms(x_ref[...], g_ref[...]).astype(BF16)
    dout = o_ref.shape[1]
    for lo in range(0, dout, COL_CHUNK):
        cols = slice(lo, min(lo + COL_CHUNK, dout))
        o_ref[:, cols] = jnp.dot(xn, w_ref[:, cols], preferred_element_type=F32)


def _conv_in_body(ins, consts, outs):
    (x_ref,), (g_ref, w_ref), (gb_ref, u_ref) = ins, consts, outs
    xn = _rms(x_ref[...], g_ref[...]).astype(BF16)
    for lo in range(0, D_MODEL, COL_CHUNK):
        cols = slice(lo, lo + COL_CHUNK)
        cols_c = slice(D_MODEL + lo, D_MODEL + lo + COL_CHUNK)
        cols_x = slice(2 * D_MODEL + lo, 2 * D_MODEL + lo + COL_CHUNK)
        gb_ref[:, cols] = jnp.dot(xn, w_ref[:, cols], preferred_element_type=F32)
        gc = jnp.dot(xn, w_ref[:, cols_c], preferred_element_type=F32)
        xp = jnp.dot(xn, w_ref[:, cols_x], preferred_element_type=F32)
        u_ref[:, cols] = gc * xp


def _conv_out_body(ins, consts, outs):
    (u_ref, prev_ref, gb_ref, r_ref), (cw_ref, w_ref), (o_ref,) = ins, consts, outs
    units = TOKEN_TILE // CHUNK
    u = u_ref[...]
    prev = prev_ref[...]
    p0 = jnp.broadcast_to(prev[:, 0:1, :], (units, CHUNK, D_MODEL)).reshape(TOKEN_TILE, D_MODEL)
    p1 = jnp.broadcast_to(prev[:, 1:2, :], (units, CHUNK, D_MODEL)).reshape(TOKEN_TILE, D_MODEL)
    t = lax.broadcasted_iota(jnp.int32, (TOKEN_TILE, 1), 0) % CHUNK
    s1 = jnp.where(t == 0, p1, pltpu.roll(u, 1, axis=0))
    s2 = jnp.where(t == 0, p0, jnp.where(t == 1, p1, pltpu.roll(u, 2, axis=0)))
    cw = cw_ref[...]
    y = cw[0:1, :] * s2 + cw[1:2, :] * s1 + cw[2:3, :] * u
    a = (gb_ref[...] * y).astype(BF16)
    o_ref[...] = r_ref[...] + jnp.dot(a, w_ref[...], preferred_element_type=F32)


def _mlp_apply(x, g_ref, wu_ref, wd_ref, gf_ref, final_norm):
    xn = _rms(x, g_ref[...]).astype(BF16)
    acc = x
    for lo in range(0, MLP_HIDDEN, COL_CHUNK):
        cols = slice(lo, lo + COL_CHUNK)
        h = jnp.maximum(jnp.dot(xn, wu_ref[:, cols], preferred_element_type=F32), 0.0)
        acc = acc + jnp.dot((h * h).astype(BF16), wd_ref[cols, :], preferred_element_type=F32)
    return _rms(acc, gf_ref[...]) if final_norm else acc


def _mlp_body(ins, consts, outs, *, final_norm):
    (x_ref,), (g_ref, wu_ref, wd_ref, gf_ref), (o_ref,) = ins, consts, outs
    o_ref[...] = _mlp_apply(x_ref[...], g_ref, wu_ref, wd_ref, gf_ref, final_norm)


def _proj_mlp_body(ins, consts, outs, *, final_norm):
    (a_ref, r_ref), (w_ref, g_ref, wu_ref, wd_ref, gf_ref), (o_ref,) = ins, consts, outs
    x = r_ref[...] + jnp.dot(a_ref[...], w_ref[...], preferred_element_type=F32)
    o_ref[...] = _mlp_apply(x, g_ref, wu_ref, wd_ref, gf_ref, final_norm)


def _sb_qkv_prompt_kernel(x_ref, g_ref, wq_ref, wkvt_ref, q_ref, kt_ref, vt_ref, ktb_ref, vtb_ref):
    xn = _rms(x_ref[...], g_ref[...]).astype(BF16)
    for lo in range(0, D_MODEL, COL_CHUNK):
        cols = slice(lo, lo + COL_CHUNK)
        q_ref[:, cols] = jnp.dot(xn, wq_ref[:, cols], preferred_element_type=F32).astype(BF16)
    for out_ref, outb_ref, base in ((kt_ref, ktb_ref, 0), (vt_ref, vtb_ref, D_MODEL)):
        for lo in range(0, D_MODEL, COL_CHUNK):
            rows = slice(lo, lo + COL_CHUNK)
            w_rows = slice(base + lo, base + lo + COL_CHUNK)
            t = lax.dot_general(wkvt_ref[w_rows, :], xn, (((1,), (1,)), ((), ())),
                                preferred_element_type=F32)
            out_ref[rows, :] = t
            for kt in range(TOKEN_TILE // SB_TILE):
                outb_ref[kt, rows, :] = t[:, kt * SB_TILE:(kt + 1) * SB_TILE].astype(BF16)


def _sb_qkv_prompt(x, gain, wq, wkvt):
    per_seq = SEQ // TOKEN_TILE
    key_tiles = TOKEN_TILE // SB_TILE
    kt_spec = pl.BlockSpec((None, D_MODEL, TOKEN_TILE), lambda i: (i // per_seq, 0, i % per_seq))
    ktb_spec = pl.BlockSpec((None, key_tiles, D_MODEL, SB_TILE),
                            lambda i: (i // per_seq, i % per_seq, 0, 0))
    kt_shape = jax.ShapeDtypeStruct((BATCH, D_MODEL, SEQ), F32)
    ktb_shape = jax.ShapeDtypeStruct((BATCH, SEQ // SB_TILE, D_MODEL, SB_TILE), BF16)
    gain = gain.reshape(1, D_MODEL)
    return pl.pallas_call(
        _sb_qkv_prompt_kernel,
        name="sb_qkv_prompt",
        grid=(P_TILES,),
        in_specs=[pl.BlockSpec((TOKEN_TILE, D_MODEL), lambda i: (i, 0)),
                  _resident(gain), _resident(wq), _resident(wkvt)],
        out_specs=[pl.BlockSpec((TOKEN_TILE, D_MODEL), lambda i: (i, 0)),
                   kt_spec, kt_spec, ktb_spec, ktb_spec],
        out_shape=[jax.ShapeDtypeStruct((N_PROMPT, D_MODEL), BF16),
                   kt_shape, kt_shape, ktb_shape, ktb_shape],
        compiler_params=_params("parallel"),
    )(x, gain, wq, wkvt)


def _sb_qkv_sample_kernel(x_ref, g_ref, wq_ref, wkv_ref, q_ref, k_ref, v_ref, kvb_ref):
    xn = _rms(x_ref[...], g_ref[...]).astype(BF16)
    for lo in range(0, D_MODEL, COL_CHUNK):
        cols = slice(lo, lo + COL_CHUNK)
        q_ref[:, cols] = jnp.dot(xn, wq_ref[:, cols], preferred_element_type=F32).astype(BF16)
    for out_ref, base in ((k_ref, 0), (v_ref, D_MODEL)):
        for lo in range(0, D_MODEL, COL_CHUNK):
            w_cols = slice(base + lo, base + lo + COL_CHUNK)
            t = jnp.dot(xn, wkv_ref[:, w_cols], preferred_element_type=F32)
            out_ref[:, lo:lo + COL_CHUNK] = t
            kvb_ref[:, w_cols] = t.astype(BF16)


def _sb_qkv_sample(x, gain, wq, wkv):
    tok = pl.BlockSpec((TOKEN_TILE, D_MODEL), lambda i: (i, 0))
    gain = gain.reshape(1, D_MODEL)
    return pl.pallas_call(
        _sb_qkv_sample_kernel,
        name="sb_qkv_sample",
        grid=(S_TILES,),
        in_specs=[tok, _resident(gain), _resident(wq), _resident(wkv)],
        out_specs=[tok, tok, tok, pl.BlockSpec((TOKEN_TILE, 2 * D_MODEL), lambda i: (i, 0))],
        out_shape=[jax.ShapeDtypeStruct((N_SAMPLE, D_MODEL), BF16),
                   jax.ShapeDtypeStruct((N_SAMPLE, D_MODEL), F32),
                   jax.ShapeDtypeStruct((N_SAMPLE, D_MODEL), F32),
                   jax.ShapeDtypeStruct((N_SAMPLE, 2 * D_MODEL), BF16)],
        compiler_params=_params("parallel"),
    )(x, gain, wq, wkv)


def _sb_tile(qst, k, v, upper2, mask, carry, acc, key_minor):
    nt = (((1,), (1,)), ((), ()))
    if key_minor:
        z = jnp.dot(qst, k, preferred_element_type=F32)
    else:
        z = lax.dot_general(qst, k, nt, preferred_element_type=F32)
    z2 = z * LOG2E
    sp = jnp.maximum(z2, 0.0) + jnp.log2(1.0 + jnp.exp2(-jnp.abs(z2)))
    spm = sp if mask is None else jnp.where(mask, sp, 0.0)
    parts = _split_bf16(spm, SB_SUFFIX_PASSES)
    below = jnp.dot(jnp.concatenate(parts, axis=1), upper2, preferred_element_type=F32) + carry
    a = jnp.exp2(z2 - sp - below)
    if mask is not None:
        a = jnp.where(mask, a, 0.0)
    a = a.astype(BF16)
    if key_minor:
        acc = acc + lax.dot_general(a, v, nt, preferred_element_type=F32)
    else:
        acc = acc + jnp.dot(a, v, preferred_element_type=F32)
    carry = below[:, 0:1] + spm[:, 0:1]
    return carry, acc


def _suffix_matrix2(n):
    s = lax.broadcasted_iota(jnp.int32, (SB_SUFFIX_PASSES * n, n), 0) % n
    j = lax.broadcasted_iota(jnp.int32, (SB_SUFFIX_PASSES * n, n), 1)
    return (s > j).astype(BF16)


def _split_heads(q):
    lane = lax.broadcasted_iota(jnp.int32, (1, SB_PAIR), 1)
    zero = jnp.zeros_like(q)
    return jnp.where(lane < SB_HEAD_DIM, q, zero), jnp.where(lane >= SB_HEAD_DIM, q, zero)


def _stack_heads(q):
    return jnp.concatenate(_split_heads(q), axis=0)


def _unstack_heads(acc):
    lane = lax.broadcasted_iota(jnp.int32, (1, SB_PAIR), 1)
    m = acc.shape[0] // 2
    return jnp.where(lane < SB_HEAD_DIM, acc[:m], acc[m:])


def _causal_mask(tq, tk):
    t = lax.broadcasted_iota(jnp.int32, (2 * tq, tk), 0) % tq
    s = lax.broadcasted_iota(jnp.int32, (2 * tq, tk), 1)
    return s < t


def _sb_prompt_kernel(q_ref, kt_ref, vt_ref, o_ref):
    i = pl.program_id(2)
    pairs = range(SB_PROMPT_PAIRS)
    lanes = [slice(pp * SB_PAIR, (pp + 1) * SB_PAIR) for pp in pairs]
    qst = [_stack_heads(q_ref[:, lanes[pp]]) for pp in pairs]
    upper2 = _suffix_matrix2(SB_TILE)
    causal = _causal_mask(SB_TILE, SB_TILE)
    zero_c = jnp.zeros((2 * SB_TILE, 1), F32)
    zero_a = jnp.zeros((2 * SB_TILE, SB_PAIR), F32)

    def tile(pp, j, mask, carry, acc):
        return _sb_tile(qst[pp], kt_ref[j, lanes[pp], :], vt_ref[j, lanes[pp], :], upper2, mask,
                        carry, acc, True)

    def store(pp, acc):
        o_ref[:, lanes[pp]] = _unstack_heads(acc).astype(o_ref.dtype)

    @pl.when(i == 0)
    def _():
        for pp in pairs:
            store(pp, tile(pp, 0, causal, zero_c, zero_a)[1])

    @pl.when(i > 0)
    def _():
        state = []
        for pp in pairs:
            carry, acc = tile(pp, i, causal, zero_c, zero_a)
            state += tile(pp, i - 1, None, carry, acc)

        def live(c):
            dead = c[1]
            for pp in pairs[1:]:
                dead = jnp.minimum(dead, c[1 + 2 * pp])
            return (c[0] < i) & (jnp.min(dead) < SB_DEAD_LOG2)

        def body(c):
            j = i - 1 - c[0]
            out = (c[0] + 1,)
            for pp in pairs:
                out += tile(pp, j, None, c[1 + 2 * pp], c[2 + 2 * pp])
            return out

        final = lax.while_loop(live, body, (jnp.int32(1),) + tuple(state))
        for pp in pairs:
            store(pp, final[2 + 2 * pp])


def _sb_prompt(q, ktb, vtb):
    nq = SEQ // SB_TILE
    width = SB_PROMPT_PAIRS * SB_PAIR
    kv_spec = pl.BlockSpec((None, nq, width, SB_TILE), lambda b, p, i: (b, 0, p, 0))
    return pl.pallas_call(
        _sb_prompt_kernel,
        name="sb_prompt",
        grid=(BATCH, D_MODEL // width, nq),
        in_specs=[pl.BlockSpec((SB_TILE, width), lambda b, p, i: (b * nq + i, p)),
                  kv_spec, kv_spec],
        out_specs=pl.BlockSpec((SB_TILE, width), lambda b, p, i: (b * nq + i, p)),
        out_shape=jax.ShapeDtypeStruct((N_PROMPT, D_MODEL), BF16),
        compiler_params=_params("parallel", "parallel", "arbitrary"),
    )(q, ktb, vtb)


def _sb_sample_kernel(q_ref, kn_ref, vn_ref, ck_ref, cv_ref, o_ref, carry_ref, acc_ref):
    upper_new = _suffix_matrix2(DEC_SEQ)
    upper2 = _suffix_matrix2(SB_TILE)
    causal = _causal_mask(DEC_SEQ, DEC_SEQ)
    n_cache = PAST_LEN // SB_TILE

    def cache_tile(pp, j, qst, carry, acc):
        feats = slice(pp * SB_PAIR, (pp + 1) * SB_PAIR)
        keys = slice(j * SB_TILE, (j + 1) * SB_TILE)
        return _sb_tile(qst, ck_ref[feats, keys].astype(BF16), cv_ref[feats, keys].astype(BF16),
                        upper2, None, carry, acc, True)

    qsts = []
    for pp in range(SB_SAMPLE_PAIRS):
        lanes = slice(pp * SB_PAIR, (pp + 1) * SB_PAIR)
        qst = _stack_heads(q_ref[:, lanes])
        carry = jnp.zeros((2 * DEC_SEQ, 1), F32)
        acc = jnp.zeros((2 * DEC_SEQ, SB_PAIR), F32)
        carry, acc = _sb_tile(qst, kn_ref[:, lanes], vn_ref[:, lanes], upper_new, causal,
                              carry, acc, False)
        carry_ref[pp], acc_ref[pp] = cache_tile(pp, n_cache - 1, qst, carry, acc)
        qsts.append(qst)

    for j in reversed(range(n_cache - 1)):
        @pl.when(jnp.min(carry_ref[...]) < SB_DEAD_LOG2)
        def _():
            for pp in range(SB_SAMPLE_PAIRS):
                carry_ref[pp], acc_ref[pp] = cache_tile(pp, j, qsts[pp], carry_ref[pp], acc_ref[pp])

    for pp in range(SB_SAMPLE_PAIRS):
        lanes = slice(pp * SB_PAIR, (pp + 1) * SB_PAIR)
        o_ref[:, lanes] = _unstack_heads(acc_ref[pp]).astype(o_ref.dtype)


def _sb_sample(q, kvb, cache_kt, cache_vt, layer):
    width = SB_SAMPLE_PAIRS * SB_PAIR
    steps = D_MODEL // width
    cache_spec = pl.BlockSpec((None, None, width, PAST_LEN), lambda b, p: (layer, b, p, 0))
    return pl.pallas_call(
        _sb_sample_kernel,
        name="sb_sample",
        grid=(DEC_BATCH, steps),
        in_specs=[pl.BlockSpec((DEC_SEQ, width), lambda b, p: (b, p)),
                  pl.BlockSpec((DEC_SEQ, width), lambda b, p: (b, p)),
                  pl.BlockSpec((DEC_SEQ, width), lambda b, p: (b, steps + p)),
                  cache_spec, cache_spec],
        out_specs=pl.BlockSpec((DEC_SEQ, width), lambda b, p: (b, p)),
        out_shape=jax.ShapeDtypeStruct((N_SAMPLE, D_MODEL), BF16),
        scratch_shapes=[pltpu.VMEM((SB_SAMPLE_PAIRS, 2 * DEC_SEQ, 1), F32),
                        pltpu.VMEM((SB_SAMPLE_PAIRS, 2 * DEC_SEQ, SB_PAIR), F32)],
        compiler_params=_params("parallel", "parallel"),
    )(q, kvb, kvb, cache_kt, cache_vt)


def _gla_kernel(q_ref, k_ref, v_ref, go_ref, gl_ref, s0_ref, wgk_ref, bgk_ref, nw_ref,
                o_ref, sout_ref, st_ref, *, nchunks):
    if nchunks > 1:
        @pl.when(pl.program_id(1) == 0)
        def _():
            for sq in range(GLA_SEQS):
                for h in range(GLA_HEADS):
                    st_ref[sq, h] = s0_ref[sq, h].T

    states = [_gla_seq(q_ref.at[sq], k_ref.at[sq], v_ref.at[sq], go_ref.at[sq], gl_ref.at[sq],
                       s0_ref.at[sq], wgk_ref, bgk_ref, nw_ref, o_ref.at[sq], st_ref.at[sq],
                       nchunks)
              for sq in range(GLA_SEQS)]

    def write_states():
        for sq in range(GLA_SEQS):
            for h in range(GLA_HEADS):
                sout_ref[sq, h] = states[sq][h].T

    if nchunks > 1:
        pl.when(pl.program_id(1) == nchunks - 1)(write_states)
    else:
        write_states()


def _gla_seq(q_ref, k_ref, v_ref, go_ref, gl_ref, s0_ref, wgk_ref, bgk_ref, nw_ref,
             o_ref, st_ref, nchunks):
    nt = (((1,), (1,)), ((), ()))
    q = q_ref[...] * (GLA_DK ** -0.5)
    k = k_ref[...]
    gate_in = jnp.dot(gl_ref[...].astype(BF16), wgk_ref[...].astype(BF16),
                      preferred_element_type=F32) + bgk_ref[...]
    g = (jnp.minimum(gate_in, 0.0) - jnp.log1p(jnp.exp(-jnp.abs(gate_in)))) / GLA_GATE_NORMALIZER

    t_i = lax.broadcasted_iota(jnp.int32, (CHUNK, CHUNK), 0)
    s_i = lax.broadcasted_iota(jnp.int32, (CHUNK, CHUNK), 1)
    lower = (s_i <= t_i).astype(BF16)
    b = jnp.zeros((CHUNK, GLA_KEY_DIM), F32)
    for part in _split_bf16(g, 3):
        b = b + jnp.dot(lower, part, preferred_element_type=F32)

    qe = (q * jnp.exp(b)).astype(BF16)
    b_last = b[CHUNK - 1:CHUNK]
    kd = (k * jnp.exp(b_last - b)).astype(BF16)
    decay_last = jnp.exp(b_last)

    key_row = lax.broadcasted_iota(jnp.int32, (CHUNK, 1), 0)
    sub_row = lax.broadcasted_iota(jnp.int32, (GLA_SUB, 1), 0)
    key_lane = lax.broadcasted_iota(jnp.int32, (1, CHUNK), 1)
    n_sub = CHUNK // GLA_SUB
    qt, kt = [None] * n_sub, [None] * n_sub
    for blk in range(1, n_sub):
        lo = blk * GLA_SUB
        ref = b[lo - 1:lo]
        qt[blk] = (q[lo:lo + GLA_SUB] * jnp.exp(b[lo:lo + GLA_SUB] - ref)).astype(BF16)
        kt[blk] = (k * jnp.exp(jnp.where(key_row < lo, ref - b, -jnp.inf))).astype(BF16)

    feat_head = lax.broadcasted_iota(jnp.int32, (GLA_KEY_DIM, GLA_HEADS * CHUNK), 0) // GLA_DK
    out_head = lax.broadcasted_iota(jnp.int32, (GLA_KEY_DIM, GLA_HEADS * CHUNK), 1) // CHUNK
    head_sum = (feat_head == out_head).astype(BF16)
    out_key = lax.broadcasted_iota(jnp.int32, (1, GLA_HEADS * CHUNK), 1) % CHUNK
    diag_rows = []
    for blk in range(n_sub):
        rows = slice(blk * GLA_SUB, (blk + 1) * GLA_SUB)
        qi, ki, bi = q[rows], k[rows], b[rows]
        prods = []
        for s in range(GLA_SUB):
            diff = jnp.where(sub_row >= s, bi - bi[s:s + 1], -jnp.inf)
            prods.append((qi * (ki[s:s + 1] * jnp.exp(diff))).astype(BF16))
        sums = jnp.dot(jnp.concatenate(prods, axis=0), head_sum, preferred_element_type=F32)
        a_blk = jnp.zeros((GLA_SUB, GLA_HEADS * CHUNK), F32)
        for s in range(GLA_SUB):
            a_blk = jnp.where(out_key == blk * GLA_SUB + s,
                              sums[s * GLA_SUB:(s + 1) * GLA_SUB], a_blk)
        diag_rows.append(a_blk)
    a_diag = jnp.concatenate(diag_rows, axis=0)

    nw = nw_ref[...]
    new_states = []
    for h in range(GLA_HEADS):
        kl = slice(h * GLA_DK, (h + 1) * GLA_DK)
        vl = slice(h * GLA_DV, (h + 1) * GLA_DV)
        v = v_ref[:, vl].astype(BF16)
        st = st_ref[h] if nchunks > 1 else s0_ref[h].T
        a_rows = [jnp.zeros((GLA_SUB, CHUNK), F32)]
        for blk in range(1, n_sub):
            a_rows.append(lax.dot_general(qt[blk][:, kl], kt[blk][:, kl], nt,
                                          preferred_element_type=F32))
        a = (jnp.concatenate(a_rows, axis=0) + a_diag[:, h * CHUNK:(h + 1) * CHUNK]).astype(BF16)
        o = (lax.dot_general(qe[:, kl], st.astype(BF16), nt, preferred_element_type=F32)
             + jnp.dot(a, v, preferred_element_type=F32))
        st_new = st * decay_last[:, kl] + lax.dot_general(
            v, kd[:, kl], (((0,), (0,)), ((), ())), preferred_element_type=F32)
        if nchunks > 1:
            st_ref[h] = st_new
        new_states.append(st_new)

        o = o * lax.rsqrt(jnp.mean(o * o, axis=-1, keepdims=True) + NORM_EPS) * nw
        go = go_ref[:, vl]
        o_ref[:, vl] = (o * (go * jax.nn.sigmoid(go))).astype(o_ref.dtype)
    return new_states


def _gla(proj, state, w_gk2, b_gk, norm_w, nseq, nchunks):
    def rows(width, col):
        return pl.BlockSpec((GLA_SEQS, CHUNK, width), lambda b, c: (b, c, col))

    state_spec = pl.BlockSpec((GLA_SEQS, GLA_HEADS, GLA_DK, GLA_DV), lambda b, c: (b, 0, 0, 0))
    w_gk2 = jnp.pad(w_gk2, ((0, LANES - GLA_GATE_RANK), (0, 0)))
    b_gk = b_gk.reshape(1, GLA_KEY_DIM)
    norm_w = norm_w.reshape(1, GLA_DV)
    proj = proj.reshape(nseq, nchunks * CHUNK, GLA_IN_PAD)
    o, s_out = pl.pallas_call(
        functools.partial(_gla_kernel, nchunks=nchunks),
        name="gla",
        grid=(nseq // GLA_SEQS, nchunks),
        in_specs=[
            rows(GLA_KEY_DIM, 0), rows(GLA_KEY_DIM, 1), rows(GLA_VALUE_DIM, 1),
            rows(GLA_VALUE_DIM, 2), rows(LANES, GLA_IN_PAD // LANES - 1),
            state_spec,
            _resident(w_gk2), _resident(b_gk), _resident(norm_w),
        ],
        out_specs=[rows(GLA_VALUE_DIM, 0), state_spec],
        out_shape=[
            jax.ShapeDtypeStruct((nseq, nchunks * CHUNK, GLA_VALUE_DIM), BF16),
            jax.ShapeDtypeStruct((nseq, GLA_HEADS, GLA_DK, GLA_DV), F32),
        ],
        scratch_shapes=[pltpu.VMEM((GLA_SEQS, GLA_HEADS, GLA_DV, GLA_DK), F32)],
        compiler_params=_params("parallel", "arbitrary"),
    )(proj, proj, proj, proj, proj, state, w_gk2, b_gk, norm_w)
    return o.reshape(nseq * nchunks * CHUNK, GLA_VALUE_DIM), s_out


def kernel(x_prompt, x_sample, cache_sb_k, cache_sb_v, state_conv, state_gla, norm_mix, norm_mlp,
           sb_w_qkv, sb_w_o, conv_w_in, conv_w, conv_w_out, gla_w_in, gla_w_gk2, gla_b_gk,
           gla_norm, gla_w_o, mlp_w_up, mlp_w_down, norm_final):
    xp = x_prompt.reshape(N_PROMPT, D_MODEL)
    xs = x_sample.reshape(N_SAMPLE, D_MODEL)
    n_sb = cache_sb_k.shape[0]
    cache_kt = cache_sb_k.transpose(0, 1, 3, 4, 2).reshape(n_sb, DEC_BATCH, D_MODEL, PAST_LEN)
    cache_vt = cache_sb_v.transpose(0, 1, 3, 4, 2).reshape(n_sb, DEC_BATCH, D_MODEL, PAST_LEN)
    sb_kt_p, sb_vt_p, sb_k_s, sb_v_s = [], [], [], []
    conv_p, conv_s, gla_p, gla_s = [], [], [], []

    def vec(a):
        return a.reshape(1, -1)

    for i in range(DEPTH):
        j = i // N_MIXERS
        final = i == DEPTH - 1
        mlp_consts = [vec(norm_mlp[i]), mlp_w_up[i].astype(BF16), mlp_w_down[i].astype(BF16),
                      vec(norm_final)]

        def proj_mlp(name, a_p, a_s, w_o):
            return _two_stream(functools.partial(_proj_mlp_body, final_norm=final), name,
                               [a_p, xp], [a_s, xs], [w_o.astype(BF16)] + mlp_consts,
                               [(D_MODEL, F32)])

        if i % N_MIXERS == 0:
            w = sb_w_qkv[j]
            wq = (w[:, :D_MODEL] * (SB_HEAD_DIM ** -0.5)).astype(BF16)
            wkv = w[:, D_MODEL:].astype(BF16)
            q_p, kt, vt, ktb, vtb = _sb_qkv_prompt(xp, norm_mix[i], wq, wkv.T)
            q_s, k_s, v_s, kvb_s = _sb_qkv_sample(xs, norm_mix[i], wq, wkv)
            o_p = _sb_prompt(q_p, ktb, vtb)
            o_s = _sb_sample(q_s, kvb_s, cache_kt, cache_vt, j)
            xp, xs = proj_mlp("sb_out_mlp", o_p, o_s, sb_w_o[j])
            sb_kt_p.append(kt)
            sb_vt_p.append(vt)
            sb_k_s.append(k_s)
            sb_v_s.append(v_s)
        elif i % N_MIXERS == 1:
            gb_p, u_p, gb_s, u_s = _two_stream(
                _conv_in_body, "conv_in", [xp], [xs],
                [vec(norm_mix[i]), conv_w_in[j].astype(BF16)], [(D_MODEL, F32), (D_MODEL, F32)])
            keep = CONV_WIDTH - 1
            tails_p = u_p.reshape(BATCH, SEQ // CHUNK, CHUNK, D_MODEL)[:, :, CHUNK - keep:]
            prev_p = jnp.concatenate(
                [jnp.zeros((BATCH, 1, keep, D_MODEL), F32), tails_p[:, :-1]], axis=1)
            prev_p = prev_p.reshape(N_PROMPT // CHUNK, keep, D_MODEL)
            xp, xs = _two_stream(_conv_out_body, "conv_out",
                                 [u_p, prev_p, gb_p, xp], [u_s, state_conv[j], gb_s, xs],
                                 [conv_w[j], conv_w_out[j].astype(BF16)], [(D_MODEL, F32)])
            xp, xs = _two_stream(functools.partial(_mlp_body, final_norm=final), "mlp",
                                 [xp], [xs], mlp_consts, [(D_MODEL, F32)])
            conv_p.append(tails_p[:, -1])
            conv_s.append(u_s.reshape(DEC_BATCH, DEC_SEQ, D_MODEL)[:, DEC_SEQ - keep:])
        else:
            w_in = jnp.pad(gla_w_in[j], ((0, 0), (0, GLA_IN_PAD - GLA_IN_DIM))).astype(BF16)
            proj_p, proj_s = _two_stream(_norm_proj_body, "gla_in", [xp], [xs],
                                         [vec(norm_mix[i]), w_in], [(GLA_IN_PAD, F32)])
            zero_state = jnp.zeros((BATCH, GLA_HEADS, GLA_DK, GLA_DV), F32)
            o_p, s_p = _gla(proj_p, zero_state, gla_w_gk2[j], gla_b_gk[j], gla_norm[j],
                            BATCH, SEQ // CHUNK)
            o_s, s_s = _gla(proj_s, state_gla[j], gla_w_gk2[j], gla_b_gk[j], gla_norm[j],
                            DEC_BATCH, DEC_SEQ // CHUNK)
            xp, xs = proj_mlp("gla_out_mlp", o_p, o_s, gla_w_o[j])
            gla_p.append(s_p)
            gla_s.append(s_s)

    def prompt_heads(a):
        a = jnp.stack(a).reshape(n_sb, BATCH, SB_HEADS, SB_HEAD_DIM, SEQ)
        return a.transpose(0, 1, 4, 2, 3)

    def sample_heads(a):
        return jnp.stack(a).reshape(n_sb, DEC_BATCH, DEC_SEQ, SB_HEADS, SB_HEAD_DIM)

    return (xp.reshape(BATCH, SEQ, D_MODEL), xs.reshape(DEC_BATCH, DEC_SEQ, D_MODEL),
            prompt_heads(sb_kt_p), prompt_heads(sb_vt_p), sample_heads(sb_k_s), sample_heads(sb_v_s),
            jnp.stack(conv_p), jnp.stack(conv_s), jnp.stack(gla_p), jnp.stack(gla_s))
```

```python
import functools

import jax
import jax.numpy as jnp
from jax import lax
from jax.experimental import pallas as pl
from jax.experimental.pallas import tpu as pltpu

F32 = jnp.float32
BF16 = jnp.bfloat16

D_MODEL = 1024
BATCH = 8
SEQ = 2048
DEPTH = 4
DEC_BATCH = 32
DEC_SEQ = 64
PAST_LEN = 1024
CHUNK = 64
N_MIXERS = 3
SB_HEADS = 16
SB_HEAD_DIM = D_MODEL // SB_HEADS
CONV_WIDTH = 3
GLA_HEADS = 4
GLA_KEY_DIM = D_MODEL // 2
GLA_VALUE_DIM = D_MODEL
GLA_DK = GLA_KEY_DIM // GLA_HEADS
GLA_DV = GLA_VALUE_DIM // GLA_HEADS
GLA_GATE_RANK = 16
GLA_GATE_NORMALIZER = 16.0
GLA_IN_DIM = 2 * GLA_KEY_DIM + 2 * GLA_VALUE_DIM + GLA_GATE_RANK
MLP_HIDDEN = 4 * D_MODEL
NORM_EPS = 1e-6

N_PROMPT = BATCH * SEQ
N_SAMPLE = DEC_BATCH * DEC_SEQ

LANES = 128
GLA_IN_PAD = 25 * LANES
TOKEN_TILE = 512
P_TILES = N_PROMPT // TOKEN_TILE
S_TILES = N_SAMPLE // TOKEN_TILE
COL_CHUNK = 512
SB_TILE = 256
SB_PAIR = 2 * SB_HEAD_DIM
SB_PAIRS = D_MODEL // SB_PAIR
SB_SUFFIX_PASSES = 1
SB_PROMPT_PAIRS = 4
SB_SAMPLE_PAIRS = 4
GLA_SEQS = 2
GLA_SUB = 16
VMEM_LIMIT = 56 * 1024 * 1024
LOG2E = 1.4426950408889634
SB_DEAD_LOG2 = 160.0


def _params(*sem):
    return pltpu.CompilerParams(dimension_semantics=sem, vmem_limit_bytes=VMEM_LIMIT)


def _rms(x, gain):
    ms = jnp.mean(x * x, axis=-1, keepdims=True)
    return x * lax.rsqrt(ms + NORM_EPS) * gain


def _split_bf16(x, passes):
    parts = []
    r = x
    for _ in range(passes):
        h = r.astype(BF16)
        parts.append(h)
        r = r - h.astype(F32)
    return parts


def _resident(arr):
    nd = arr.ndim
    return pl.BlockSpec(arr.shape, lambda *_: (0,) * nd, pipeline_mode=pl.Buffered(1))


def _two_stream(body, name, ins_p, ins_s, consts, outs):
    n_in, n_c, n_out = len(ins_p), len(consts), len(outs)

    def kern(*refs):
        p_in, s_in = refs[:n_in], refs[n_in:2 * n_in]
        c = refs[2 * n_in:2 * n_in + n_c]
        p_out = refs[2 * n_in + n_c:2 * n_in + n_c + n_out]
        s_out = refs[2 * n_in + n_c + n_out:]
        i = pl.program_id(0)

        @pl.when(i < P_TILES)
        def _():
            body(p_in, c, p_out)

        @pl.when(i >= P_TILES)
        def _():
            body(s_in, c, s_out)

    def p_idx(i):
        return jnp.minimum(i, P_TILES - 1)

    def s_idx(i):
        return jnp.maximum(i - P_TILES, 0)

    def spec(arr, tiles, idx):
        block = (arr.shape[0] // tiles,) + arr.shape[1:]
        zeros = (0,) * (arr.ndim - 1)
        return pl.BlockSpec(block, lambda i: (idx(i),) + zeros)

    def out_spec(width, idx):
        return pl.BlockSpec((TOKEN_TILE, width), lambda i: (idx(i), 0))

    return pl.pallas_call(
        kern,
        name=name,
        grid=(P_TILES + S_TILES,),
        in_specs=([spec(a, P_TILES, p_idx) for a in ins_p] + [spec(a, S_TILES, s_idx) for a in ins_s]
                  + [_resident(a) for a in consts]),
        out_specs=([out_spec(w, p_idx) for w, _ in outs] + [out_spec(w, s_idx) for w, _ in outs]),
        out_shape=([jax.ShapeDtypeStruct((N_PROMPT, w), dt) for w, dt in outs]
                   + [jax.ShapeDtypeStruct((N_SAMPLE, w), dt) for w, dt in outs]),
        compiler_params=_params("arbitrary"),
    )(*ins_p, *ins_s, *consts)


def _norm_proj_body(ins, consts, outs):
    (x_ref,), (g_ref, w_ref), (o_ref,) = ins, consts, outs
    xn = _rms(x_ref[...], g_ref[...]).astype(BF16)
    dout = o_ref.shape[1]
    for lo in range(0, dout, COL_CHUNK):
        cols = slice(lo, min(lo + COL_CHUNK, dout))
        o_ref[:, cols] = jnp.dot(xn, w_ref[:, cols], preferred_element_type=F32)


def _conv_in_body(ins, consts, outs):
    (x_ref,), (g_ref, w_ref), (gb_ref, u_ref) = ins, consts, outs
    xn = _rms(x_ref[...], g_ref[...]).astype(BF16)
    for lo in range(0, D_MODEL, COL_CHUNK):
        cols = slice(lo, lo + COL_CHUNK)
        cols_c = slice(D_MODEL + lo, D_MODEL + lo + COL_CHUNK)
        cols_x = slice(2 * D_MODEL + lo, 2 * D_MODEL + lo + COL_CHUNK)
        gb_ref[:, cols] = jnp.dot(xn, w_ref[:, cols], preferred_element_type=F32)
        gc = jnp.dot(xn, w_ref[:, cols_c], preferred_element_type=F32)
        xp = jnp.dot(xn, w_ref[:, cols_x], preferred_element_type=F32)
        u_ref[:, cols] = gc * xp


def _conv_out_body(ins, consts, outs):
    (u_ref, prev_ref, gb_ref, r_ref), (cw_ref, w_ref), (o_ref,) = ins, consts, outs
    units = TOKEN_TILE // CHUNK
    u = u_ref[...]
    prev = prev_ref[...]
    p0 = jnp.broadcast_to(prev[:, 0:1, :], (units, CHUNK, D_MODEL)).reshape(TOKEN_TILE, D_MODEL)
    p1 = jnp.broadcast_to(prev[:, 1:2, :], (units, CHUNK, D_MODEL)).reshape(TOKEN_TILE, D_MODEL)
    t = lax.broadcasted_iota(jnp.int32, (TOKEN_TILE, 1), 0) % CHUNK
    s1 = jnp.where(t == 0, p1, pltpu.roll(u, 1, axis=0))
    s2 = jnp.where(t == 0, p0, jnp.where(t == 1, p1, pltpu.roll(u, 2, axis=0)))
    cw = cw_ref[...]
    y = cw[0:1, :] * s2 + cw[1:2, :] * s1 + cw[2:3, :] * u
    a = (gb_ref[...] * y).astype(BF16)
    o_ref[...] = r_ref[...] + jnp.dot(a, w_ref[...], preferred_element_type=F32)


def _mlp_apply(x, g_ref, wu_ref, wd_ref, gf_ref, final_norm):
    xn = _rms(x, g_ref[...]).astype(BF16)
    acc = x
    for lo in range(0, MLP_HIDDEN, COL_CHUNK):
        cols = slice(lo, lo + COL_CHUNK)
        h = jnp.maximum(jnp.dot(xn, wu_ref[:, cols], preferred_element_type=F32), 0.0)
        acc = acc + jnp.dot((h * h).astype(BF16), wd_ref[cols, :], preferred_element_type=F32)
    return _rms(acc, gf_ref[...]) if final_norm else acc


def _mlp_body(ins, consts, outs, *, final_norm):
    (x_ref,), (g_ref, wu_ref, wd_ref, gf_ref), (o_ref,) = ins, consts, outs
    o_ref[...] = _mlp_apply(x_ref[...], g_ref, wu_ref, wd_ref, gf_ref, final_norm)


def _proj_mlp_body(ins, consts, outs, *, final_norm):
    (a_ref, r_ref), (w_ref, g_ref, wu_ref, wd_ref, gf_ref), (o_ref,) = ins, consts, outs
    x = r_ref[...] + jnp.dot(a_ref[...], w_ref[...], preferred_element_type=F32)
    o_ref[...] = _mlp_apply(x, g_ref, wu_ref, wd_ref, gf_ref, final_norm)


def _sb_qkv_prompt_kernel(*refs, layer, n_layers, first):
    x_ref, g_ref, wq_ref, wkvt_ref = refs[:4]
    q_ref, kt_ref, vt_ref, ktb_ref, vtb_ref = refs[-5:]
    xn = _rms(x_ref[...], g_ref[...]).astype(BF16)
    for lo in range(0, D_MODEL, COL_CHUNK):
        cols = slice(lo, lo + COL_CHUNK)
        q_ref[:, cols] = jnp.dot(xn, wq_ref[:, cols], preferred_element_type=F32).astype(BF16)
    for out_ref, outb_ref, base in ((kt_ref, ktb_ref, 0), (vt_ref, vtb_ref, D_MODEL)):
        for lo in range(0, D_MODEL, COL_CHUNK):
            rows = slice(lo, lo + COL_CHUNK)
            w_rows = slice(base + lo, base + lo + COL_CHUNK)
            t = lax.dot_general(wkvt_ref[w_rows, :], xn, (((1,), (1,)), ((), ())),
                                preferred_element_type=F32)
            if first:
                for other in range(n_layers):
                    out_ref[other, rows, :] = t if other == layer else jnp.zeros_like(t)
            else:
                out_ref[rows, :] = t
            for kt in range(TOKEN_TILE // SB_TILE):
                outb_ref[kt, rows, :] = t[:, kt * SB_TILE:(kt + 1) * SB_TILE].astype(BF16)


def _sb_qkv_prompt(x, gain, wq, wkvt, layer, n_layers, stacked):
    per_seq = SEQ // TOKEN_TILE
    key_tiles = TOKEN_TILE // SB_TILE
    first = stacked is None
    if first:
        kt_spec = pl.BlockSpec((n_layers, None, D_MODEL, TOKEN_TILE),
                               lambda i: (0, i // per_seq, 0, i % per_seq))
    else:
        kt_spec = pl.BlockSpec((None, None, D_MODEL, TOKEN_TILE),
                               lambda i: (layer, i // per_seq, 0, i % per_seq))
    ktb_spec = pl.BlockSpec((None, key_tiles, D_MODEL, SB_TILE),
                            lambda i: (i // per_seq, i % per_seq, 0, 0))
    kt_shape = jax.ShapeDtypeStruct((n_layers, BATCH, D_MODEL, SEQ), F32)
    ktb_shape = jax.ShapeDtypeStruct((BATCH, SEQ // SB_TILE, D_MODEL, SB_TILE), BF16)
    gain = gain.reshape(1, D_MODEL)
    operands = [x, gain, wq, wkvt] + ([] if first else list(stacked))
    in_specs = [pl.BlockSpec((TOKEN_TILE, D_MODEL), lambda i: (i, 0)),
                _resident(gain), _resident(wq), _resident(wkvt)]
    if not first:
        in_specs += [pl.BlockSpec(memory_space=pl.ANY)] * 2
    return pl.pallas_call(
        functools.partial(_sb_qkv_prompt_kernel, layer=layer, n_layers=n_layers, first=first),
        name="sb_qkv_prompt",
        grid=(P_TILES,),
        in_specs=in_specs,
        out_specs=[pl.BlockSpec((TOKEN_TILE, D_MODEL), lambda i: (i, 0)),
                   kt_spec, kt_spec, ktb_spec, ktb_spec],
        out_shape=[jax.ShapeDtypeStruct((N_PROMPT, D_MODEL), BF16),
                   kt_shape, kt_shape, ktb_shape, ktb_shape],
        input_output_aliases={} if first else {4: 1, 5: 2},
        compiler_params=_params("parallel"),
    )(*operands)


def _sb_qkv_sample_kernel(x_ref, g_ref, wq_ref, wkv_ref, q_ref, k_ref, v_ref, kvb_ref):
    xn = _rms(x_ref[...], g_ref[...]).astype(BF16)
    for lo in range(0, D_MODEL, COL_CHUNK):
        cols = slice(lo, lo + COL_CHUNK)
        q_ref[:, cols] = jnp.dot(xn, wq_ref[:, cols], preferred_element_type=F32).astype(BF16)
    for out_ref, base in ((k_ref, 0), (v_ref, D_MODEL)):
        for lo in range(0, D_MODEL, COL_CHUNK):
            w_cols = slice(base + lo, base + lo + COL_CHUNK)
            t = jnp.dot(xn, wkv_ref[:, w_cols], preferred_element_type=F32)
            out_ref[:, lo:lo + COL_CHUNK] = t
            kvb_ref[:, w_cols] = t.astype(BF16)


def _sb_qkv_sample(x, gain, wq, wkv):
    tok = pl.BlockSpec((TOKEN_TILE, D_MODEL), lambda i: (i, 0))
    gain = gain.reshape(1, D_MODEL)
    return pl.pallas_call(
        _sb_qkv_sample_kernel,
        name="sb_qkv_sample",
        grid=(S_TILES,),
        in_specs=[tok, _resident(gain), _resident(wq), _resident(wkv)],
        out_specs=[tok, tok, tok, pl.BlockSpec((TOKEN_TILE, 2 * D_MODEL), lambda i: (i, 0))],
        out_shape=[jax.ShapeDtypeStruct((N_SAMPLE, D_MODEL), BF16),
                   jax.ShapeDtypeStruct((N_SAMPLE, D_MODEL), F32),
                   jax.ShapeDtypeStruct((N_SAMPLE, D_MODEL), F32),
                   jax.ShapeDtypeStruct((N_SAMPLE, 2 * D_MODEL), BF16)],
        compiler_params=_params("parallel"),
    )(x, gain, wq, wkv)


def _sb_tile(qst, k, v, upper2, mask, carry, acc, key_minor):
    nt = (((1,), (1,)), ((), ()))
    if key_minor:
        z = jnp.dot(qst, k, preferred_element_type=F32)
    else:
        z = lax.dot_general(qst, k, nt, preferred_element_type=F32)
    z2 = z * LOG2E
    sp = jnp.maximum(z2, 0.0) + jnp.log2(1.0 + jnp.exp2(-jnp.abs(z2)))
    spm = sp if mask is None else jnp.where(mask, sp, 0.0)
    parts = _split_bf16(spm, SB_SUFFIX_PASSES)
    below = jnp.dot(jnp.concatenate(parts, axis=1), upper2, preferred_element_type=F32) + carry
    a = jnp.exp2(z2 - sp - below)
    if mask is not None:
        a = jnp.where(mask, a, 0.0)
    a = a.astype(BF16)
    if key_minor:
        acc = acc + lax.dot_general(a, v, nt, preferred_element_type=F32)
    else:
        acc = acc + jnp.dot(a, v, preferred_element_type=F32)
    carry = below[:, 0:1] + spm[:, 0:1]
    return carry, acc


def _suffix_matrix2(n):
    s = lax.broadcasted_iota(jnp.int32, (SB_SUFFIX_PASSES * n, n), 0) % n
    j = lax.broadcasted_iota(jnp.int32, (SB_SUFFIX_PASSES * n, n), 1)
    return (s > j).astype(BF16)


def _split_heads(q):
    lane = lax.broadcasted_iota(jnp.int32, (1, SB_PAIR), 1)
    zero = jnp.zeros_like(q)
    return jnp.where(lane < SB_HEAD_DIM, q, zero), jnp.where(lane >= SB_HEAD_DIM, q, zero)


def _stack_heads(q):
    return jnp.concatenate(_split_heads(q), axis=0)


def _unstack_heads(acc):
    lane = lax.broadcasted_iota(jnp.int32, (1, SB_PAIR), 1)
    m = acc.shape[0] // 2
    return jnp.where(lane < SB_HEAD_DIM, acc[:m], acc[m:])


def _causal_mask(tq, tk):
    t = lax.broadcasted_iota(jnp.int32, (2 * tq, tk), 0) % tq
    s = lax.broadcasted_iota(jnp.int32, (2 * tq, tk), 1)
    return s < t


def _sb_prompt_kernel(q_ref, kt_ref, vt_ref, o_ref):
    i = pl.program_id(2)
    pairs = range(SB_PROMPT_PAIRS)
    lanes = [slice(pp * SB_PAIR, (pp + 1) * SB_PAIR) for pp in pairs]
    qst = [_stack_heads(q_ref[:, lanes[pp]]) for pp in pairs]
    upper2 = _suffix_matrix2(SB_TILE)
    causal = _causal_mask(SB_TILE, SB_TILE)
    zero_c = jnp.zeros((2 * SB_TILE, 1), F32)
    zero_a = jnp.zeros((2 * SB_TILE, SB_PAIR), F32)

    def tile(pp, j, mask, carry, acc):
        return _sb_tile(qst[pp], kt_ref[j, lanes[pp], :], vt_ref[j, lanes[pp], :], upper2, mask,
                        carry, acc, True)

    def store(pp, acc):
        o_ref[:, lanes[pp]] = _unstack_heads(acc).astype(o_ref.dtype)

    @pl.when(i == 0)
    def _():
        for pp in pairs:
            store(pp, tile(pp, 0, causal, zero_c, zero_a)[1])

    @pl.when(i > 0)
    def _():
        state = []
        for pp in pairs:
            carry, acc = tile(pp, i, causal, zero_c, zero_a)
            state += tile(pp, i - 1, None, carry, acc)

        def live(c):
            dead = c[1]
            for pp in pairs[1:]:
                dead = jnp.minimum(dead, c[1 + 2 * pp])
            return (c[0] < i) & (jnp.min(dead) < SB_DEAD_LOG2)

        def body(c):
            j = i - 1 - c[0]
            out = (c[0] + 1,)
            for pp in pairs:
                out += tile(pp, j, None, c[1 + 2 * pp], c[2 + 2 * pp])
            return out

        final = lax.while_loop(live, body, (jnp.int32(1),) + tuple(state))
        for pp in pairs:
            store(pp, final[2 + 2 * pp])


def _sb_prompt(q, ktb, vtb):
    nq = SEQ // SB_TILE
    width = SB_PROMPT_PAIRS * SB_PAIR
    kv_spec = pl.BlockSpec((None, nq, width, SB_TILE), lambda b, p, i: (b, 0, p, 0))
    return pl.pallas_call(
        _sb_prompt_kernel,
        name="sb_prompt",
        grid=(BATCH, D_MODEL // width, nq),
        in_specs=[pl.BlockSpec((SB_TILE, width), lambda b, p, i: (b * nq + i, p)),
                  kv_spec, kv_spec],
        out_specs=pl.BlockSpec((SB_TILE, width), lambda b, p, i: (b * nq + i, p)),
        out_shape=jax.ShapeDtypeStruct((N_PROMPT, D_MODEL), BF16),
        compiler_params=_params("parallel", "parallel", "arbitrary"),
    )(q, ktb, vtb)


def _sb_sample_kernel(q_ref, kn_ref, vn_ref, ck_ref, cv_ref, o_ref, carry_ref, acc_ref):
    upper_new = _suffix_matrix2(DEC_SEQ)
    upper2 = _suffix_matrix2(SB_TILE)
    causal = _causal_mask(DEC_SEQ, DEC_SEQ)
    n_cache = PAST_LEN // SB_TILE

    def cache_tile(pp, j, qst, carry, acc):
        feats = slice(pp * SB_PAIR, (pp + 1) * SB_PAIR)
        keys = slice(j * SB_TILE, (j + 1) * SB_TILE)
        return _sb_tile(qst, ck_ref[feats, keys].astype(BF16), cv_ref[feats, keys].astype(BF16),
                        upper2, None, carry, acc, True)

    qsts = []
    for pp in range(SB_SAMPLE_PAIRS):
        lanes = slice(pp * SB_PAIR, (pp + 1) * SB_PAIR)
        qst = _stack_heads(q_ref[:, lanes])
        carry = jnp.zeros((2 * DEC_SEQ, 1), F32)
        acc = jnp.zeros((2 * DEC_SEQ, SB_PAIR), F32)
        carry, acc = _sb_tile(qst, kn_ref[:, lanes], vn_ref[:, lanes], upper_new, causal,
                              carry, acc, False)
        carry_ref[pp], acc_ref[pp] = cache_tile(pp, n_cache - 1, qst, carry, acc)
        qsts.append(qst)

    for j in reversed(range(n_cache - 1)):
        @pl.when(jnp.min(carry_ref[...]) < SB_DEAD_LOG2)
        def _():
            for pp in range(SB_SAMPLE_PAIRS):
                carry_ref[pp], acc_ref[pp] = cache_tile(pp, j, qsts[pp], carry_ref[pp], acc_ref[pp])

    for pp in range(SB_SAMPLE_PAIRS):
        lanes = slice(pp * SB_PAIR, (pp + 1) * SB_PAIR)
        o_ref[:, lanes] = _unstack_heads(acc_ref[pp]).astype(o_ref.dtype)


def _sb_sample(q, kvb, cache_kt, cache_vt, layer):
    width = SB_SAMPLE_PAIRS * SB_PAIR
    steps = D_MODEL // width
    cache_spec = pl.BlockSpec((None, None, width, PAST_LEN), lambda b, p: (layer, b, p, 0))
    return pl.pallas_call(
        _sb_sample_kernel,
        name="sb_sample",
        grid=(DEC_BATCH, steps),
        in_specs=[pl.BlockSpec((DEC_SEQ, width), lambda b, p: (b, p)),
                  pl.BlockSpec((DEC_SEQ, width), lambda b, p: (b, p)),
                  pl.BlockSpec((DEC_SEQ, width), lambda b, p: (b, steps + p)),
                  cache_spec, cache_spec],
        out_specs=pl.BlockSpec((DEC_SEQ, width), lambda b, p: (b, p)),
        out_shape=jax.ShapeDtypeStruct((N_SAMPLE, D_MODEL), BF16),
        scratch_shapes=[pltpu.VMEM((SB_SAMPLE_PAIRS, 2 * DEC_SEQ, 1), F32),
                        pltpu.VMEM((SB_SAMPLE_PAIRS, 2 * DEC_SEQ, SB_PAIR), F32)],
        compiler_params=_params("parallel", "parallel"),
    )(q, kvb, kvb, cache_kt, cache_vt)


def _gla_kernel(q_ref, k_ref, v_ref, go_ref, gl_ref, s0_ref, wgk_ref, bgk_ref, nw_ref,
                o_ref, sout_ref, st_ref, *, nchunks):
    if nchunks > 1:
        @pl.when(pl.program_id(1) == 0)
        def _():
            for sq in range(GLA_SEQS):
                for h in range(GLA_HEADS):
                    st_ref[sq, h] = s0_ref[sq, h].T

    states = [_gla_seq(q_ref.at[sq], k_ref.at[sq], v_ref.at[sq], go_ref.at[sq], gl_ref.at[sq],
                       s0_ref.at[sq], wgk_ref, bgk_ref, nw_ref, o_ref.at[sq], st_ref.at[sq],
                       nchunks)
              for sq in range(GLA_SEQS)]

    def write_states():
        for sq in range(GLA_SEQS):
            for h in range(GLA_HEADS):
                sout_ref[sq, h] = states[sq][h].T

    if nchunks > 1:
        pl.when(pl.program_id(1) == nchunks - 1)(write_states)
    else:
        write_states()


def _gla_seq(q_ref, k_ref, v_ref, go_ref, gl_ref, s0_ref, wgk_ref, bgk_ref, nw_ref,
             o_ref, st_ref, nchunks):
    nt = (((1,), (1,)), ((), ()))
    q = q_ref[...] * (GLA_DK ** -0.5)
    k = k_ref[...]
    gate_in = jnp.dot(gl_ref[...].astype(BF16), wgk_ref[...].astype(BF16),
                      preferred_element_type=F32) + bgk_ref[...]
    g = (jnp.minimum(gate_in, 0.0) - jnp.log1p(jnp.exp(-jnp.abs(gate_in)))) / GLA_GATE_NORMALIZER

    t_i = lax.broadcasted_iota(jnp.int32, (CHUNK, CHUNK), 0)
    s_i = lax.broadcasted_iota(jnp.int32, (CHUNK, CHUNK), 1)
    lower = (s_i <= t_i).astype(BF16)
    b = jnp.zeros((CHUNK, GLA_KEY_DIM), F32)
    for part in _split_bf16(g, 3):
        b = b + jnp.dot(lower, part, preferred_element_type=F32)

    qe = (q * jnp.exp(b)).astype(BF16)
    b_last = b[CHUNK - 1:CHUNK]
    kd = (k * jnp.exp(b_last - b)).astype(BF16)
    decay_last = jnp.exp(b_last)

    key_row = lax.broadcasted_iota(jnp.int32, (CHUNK, 1), 0)
    sub_row = lax.broadcasted_iota(jnp.int32, (GLA_SUB, 1), 0)
    key_lane = lax.broadcasted_iota(jnp.int32, (1, CHUNK), 1)
    n_sub = CHUNK // GLA_SUB
    qt, kt = [None] * n_sub, [None] * n_sub
    for blk in range(1, n_sub):
        lo = blk * GLA_SUB
        ref = b[lo - 1:lo]
        qt[blk] = (q[lo:lo + GLA_SUB] * jnp.exp(b[lo:lo + GLA_SUB] - ref)).astype(BF16)
        kt[blk] = (k * jnp.exp(jnp.where(key_row < lo, ref - b, -jnp.inf))).astype(BF16)

    feat_head = lax.broadcasted_iota(jnp.int32, (GLA_KEY_DIM, GLA_HEADS * CHUNK), 0) // GLA_DK
    out_head = lax.broadcasted_iota(jnp.int32, (GLA_KEY_DIM, GLA_HEADS * CHUNK), 1) // CHUNK
    head_sum = (feat_head == out_head).astype(BF16)
    out_key = lax.broadcasted_iota(jnp.int32, (1, GLA_HEADS * CHUNK), 1) % CHUNK
    diag_rows = []
    for blk in range(n_sub):
        rows = slice(blk * GLA_SUB, (blk + 1) * GLA_SUB)
        qi, ki, bi = q[rows], k[rows], b[rows]
        prods = []
        for s in range(GLA_SUB):
            diff = jnp.where(sub_row >= s, bi - bi[s:s + 1], -jnp.inf)
            prods.append((qi * (ki[s:s + 1] * jnp.exp(diff))).astype(BF16))
        sums = jnp.dot(jnp.concatenate(prods, axis=0), head_sum, preferred_element_type=F32)
        a_blk = jnp.zeros((GLA_SUB, GLA_HEADS * CHUNK), F32)
        for s in range(GLA_SUB):
            a_blk = jnp.where(out_key == blk * GLA_SUB + s,
                              sums[s * GLA_SUB:(s + 1) * GLA_SUB], a_blk)
        diag_rows.append(a_blk)
    a_diag = jnp.concatenate(diag_rows, axis=0)

    nw = nw_ref[...]
    new_states = []
    for h in range(GLA_HEADS):
        kl = slice(h * GLA_DK, (h + 1) * GLA_DK)
        vl = slice(h * GLA_DV, (h + 1) * GLA_DV)
        v = v_ref[:, vl].astype(BF16)
        st = st_ref[h] if nchunks > 1 else s0_ref[h].T
        a_rows = [jnp.zeros((GLA_SUB, CHUNK), F32)]
        for blk in range(1, n_sub):
            a_rows.append(lax.dot_general(qt[blk][:, kl], kt[blk][:, kl], nt,
                                          preferred_element_type=F32))
        a = (jnp.concatenate(a_rows, axis=0) + a_diag[:, h * CHUNK:(h + 1) * CHUNK]).astype(BF16)
        o = (lax.dot_general(qe[:, kl], st.astype(BF16), nt, preferred_element_type=F32)
             + jnp.dot(a, v, preferred_element_type=F32))
        st_new = st * decay_last[:, kl] + lax.dot_general(
            v, kd[:, kl], (((0,), (0,)), ((), ())), preferred_element_type=F32)
        if nchunks > 1:
            st_ref[h] = st_new
        new_states.append(st_new)

        o = o * lax.rsqrt(jnp.mean(o * o, axis=-1, keepdims=True) + NORM_EPS) * nw
        go = go_ref[:, vl]
        o_ref[:, vl] = (o * (go * jax.nn.sigmoid(go))).astype(o_ref.dtype)
    return new_states


def _gla(proj, state, w_gk2, b_gk, norm_w, nseq, nchunks):
    def rows(width, col):
        return pl.BlockSpec((GLA_SEQS, CHUNK, width), lambda b, c: (b, c, col))

    state_spec = pl.BlockSpec((GLA_SEQS, GLA_HEADS, GLA_DK, GLA_DV), lambda b, c: (b, 0, 0, 0))
    w_gk2 = jnp.pad(w_gk2, ((0, LANES - GLA_GATE_RANK), (0, 0)))
    b_gk = b_gk.reshape(1, GLA_KEY_DIM)
    norm_w = norm_w.reshape(1, GLA_DV)
    proj = proj.reshape(nseq, nchunks * CHUNK, GLA_IN_PAD)
    o, s_out = pl.pallas_call(
        functools.partial(_gla_kernel, nchunks=nchunks),
        name="gla",
        grid=(nseq // GLA_SEQS, nchunks),
        in_specs=[
            rows(GLA_KEY_DIM, 0), rows(GLA_KEY_DIM, 1), rows(GLA_VALUE_DIM, 1),
            rows(GLA_VALUE_DIM, 2), rows(LANES, GLA_IN_PAD // LANES - 1),
            state_spec,
            _resident(w_gk2), _resident(b_gk), _resident(norm_w),
        ],
        out_specs=[rows(GLA_VALUE_DIM, 0), state_spec],
        out_shape=[
            jax.ShapeDtypeStruct((nseq, nchunks * CHUNK, GLA_VALUE_DIM), BF16),
            jax.ShapeDtypeStruct((nseq, GLA_HEADS, GLA_DK, GLA_DV), F32),
        ],
        scratch_shapes=[pltpu.VMEM((GLA_SEQS, GLA_HEADS, GLA_DV, GLA_DK), F32)],
        compiler_params=_params("parallel", "arbitrary"),
    )(proj, proj, proj, proj, proj, state, w_gk2, b_gk, norm_w)
    return o.reshape(nseq * nchunks * CHUNK, GLA_VALUE_DIM), s_out


def kernel(x_prompt, x_sample, cache_sb_k, cache_sb_v, state_conv, state_gla, norm_mix, norm_mlp,
           sb_w_qkv, sb_w_o, conv_w_in, conv_w, conv_w_out, gla_w_in, gla_w_gk2, gla_b_gk,
           gla_norm, gla_w_o, mlp_w_up, mlp_w_down, norm_final):
    xp = x_prompt.reshape(N_PROMPT, D_MODEL)
    xs = x_sample.reshape(N_SAMPLE, D_MODEL)
    n_sb = cache_sb_k.shape[0]
    cache_kt = cache_sb_k.transpose(0, 1, 3, 4, 2).reshape(n_sb, DEC_BATCH, D_MODEL, PAST_LEN)
    cache_vt = cache_sb_v.transpose(0, 1, 3, 4, 2).reshape(n_sb, DEC_BATCH, D_MODEL, PAST_LEN)
    sb_kv_p, sb_k_s, sb_v_s = None, [], []
    conv_p, conv_s, gla_p, gla_s = [], [], [], []

    def vec(a):
        return a.reshape(1, -1)

    for i in range(DEPTH):
        j = i // N_MIXERS
        final = i == DEPTH - 1
        mlp_consts = [vec(norm_mlp[i]), mlp_w_up[i].astype(BF16), mlp_w_down[i].astype(BF16),
                      vec(norm_final)]

        def proj_mlp(name, a_p, a_s, w_o):
            return _two_stream(functools.partial(_proj_mlp_body, final_norm=final), name,
                               [a_p, xp], [a_s, xs], [w_o.astype(BF16)] + mlp_consts,
                               [(D_MODEL, F32)])

        if i % N_MIXERS == 0:
            w = sb_w_qkv[j]
            wq = (w[:, :D_MODEL] * (SB_HEAD_DIM ** -0.5)).astype(BF16)
            wkv = w[:, D_MODEL:].astype(BF16)
            q_p, kt, vt, ktb, vtb = _sb_qkv_prompt(xp, norm_mix[i], wq, wkv.T, j, n_sb, sb_kv_p)
            sb_kv_p = (kt, vt)
            q_s, k_s, v_s, kvb_s = _sb_qkv_sample(xs, norm_mix[i], wq, wkv)
            o_p = _sb_prompt(q_p, ktb, vtb)
            o_s = _sb_sample(q_s, kvb_s, cache_kt, cache_vt, j)
            xp, xs = proj_mlp("sb_out_mlp", o_p, o_s, sb_w_o[j])
            sb_k_s.append(k_s)
            sb_v_s.append(v_s)
        elif i % N_MIXERS == 1:
            gb_p, u_p, gb_s, u_s = _two_stream(
                _conv_in_body, "conv_in", [xp], [xs],
                [vec(norm_mix[i]), conv_w_in[j].astype(BF16)], [(D_MODEL, F32), (D_MODEL, F32)])
            keep = CONV_WIDTH - 1
            tails_p = u_p.reshape(BATCH, SEQ // CHUNK, CHUNK, D_MODEL)[:, :, CHUNK - keep:]
            prev_p = jnp.concatenate(
                [jnp.zeros((BATCH, 1, keep, D_MODEL), F32), tails_p[:, :-1]], axis=1)
            prev_p = prev_p.reshape(N_PROMPT // CHUNK, keep, D_MODEL)
            xp, xs = _two_stream(_conv_out_body, "conv_out",
                                 [u_p, prev_p, gb_p, xp], [u_s, state_conv[j], gb_s, xs],
                                 [conv_w[j], conv_w_out[j].astype(BF16)], [(D_MODEL, F32)])
            xp, xs = _two_stream(functools.partial(_mlp_body, final_norm=final), "mlp",
                                 [xp], [xs], mlp_consts, [(D_MODEL, F32)])
            conv_p.append(tails_p[:, -1])
            conv_s.append(u_s.reshape(DEC_BATCH, DEC_SEQ, D_MODEL)[:, DEC_SEQ - keep:])
        else:
            w_in = jnp.pad(gla_w_in[j], ((0, 0), (0, GLA_IN_PAD - GLA_IN_DIM))).astype(BF16)
            proj_p, proj_s = _two_stream(_norm_proj_body, "gla_in", [xp], [xs],
                                         [vec(norm_mix[i]), w_in], [(GLA_IN_PAD, F32)])
            zero_state = jnp.zeros((BATCH, GLA_HEADS, GLA_DK, GLA_DV), F32)
            o_p, s_p = _gla(proj_p, zero_state, gla_w_gk2[j], gla_b_gk[j], gla_norm[j],
                            BATCH, SEQ // CHUNK)
            o_s, s_s = _gla(proj_s, state_gla[j], gla_w_gk2[j], gla_b_gk[j], gla_norm[j],
                            DEC_BATCH, DEC_SEQ // CHUNK)
            xp, xs = proj_mlp("gla_out_mlp", o_p, o_s, gla_w_o[j])
            gla_p.append(s_p)
            gla_s.append(s_s)

    def prompt_heads(a):
        return a.reshape(n_sb, BATCH, SB_HEADS, SB_HEAD_DIM, SEQ).transpose(0, 1, 4, 2, 3)

    def sample_heads(a):
        return jnp.stack(a).reshape(n_sb, DEC_BATCH, DEC_SEQ, SB_HEADS, SB_HEAD_DIM)

    return (xp.reshape(BATCH, SEQ, D_MODEL), xs.reshape(DEC_BATCH, DEC_SEQ, D_MODEL),
            prompt_heads(sb_kv_p[0]), prompt_heads(sb_kv_p[1]),
            sample_heads(sb_k_s), sample_heads(sb_v_s),
            jnp.stack(conv_p), jnp.stack(conv_s), jnp.stack(gla_p), jnp.stack(gla_s))
```

```python
import functools

import jax
import jax.numpy as jnp
from jax import lax
from jax.experimental import pallas as pl
from jax.experimental.pallas import tpu as pltpu

F32 = jnp.float32
BF16 = jnp.bfloat16

D_MODEL = 1024
BATCH = 8
SEQ = 2048
DEPTH = 4
DEC_BATCH = 32
DEC_SEQ = 64
PAST_LEN = 1024
CHUNK = 64
N_MIXERS = 3
SB_HEADS = 16
SB_HEAD_DIM = D_MODEL // SB_HEADS
CONV_WIDTH = 3
GLA_HEADS = 4
GLA_KEY_DIM = D_MODEL // 2
GLA_VALUE_DIM = D_MODEL
GLA_DK = GLA_KEY_DIM // GLA_HEADS
GLA_DV = GLA_VALUE_DIM // GLA_HEADS
GLA_GATE_RANK = 16
GLA_GATE_NORMALIZER = 16.0
GLA_IN_DIM = 2 * GLA_KEY_DIM + 2 * GLA_VALUE_DIM + GLA_GATE_RANK
MLP_HIDDEN = 4 * D_MODEL
NORM_EPS = 1e-6

N_PROMPT = BATCH * SEQ
N_SAMPLE = DEC_BATCH * DEC_SEQ

LANES = 128
GLA_IN_PAD = 25 * LANES
TOKEN_TILE = 512
P_TILES = N_PROMPT // TOKEN_TILE
S_TILES = N_SAMPLE // TOKEN_TILE
CONV_TILE = 256
COL_CHUNK = 512
SB_TILE = 256
SB_PAIR = 2 * SB_HEAD_DIM
SB_PAIRS = D_MODEL // SB_PAIR
SB_SUFFIX_PASSES = 1
SB_PROMPT_PAIRS = 4
SB_SAMPLE_PAIRS = 4
GLA_SEQS = 2
GLA_SUB = 16
VMEM_LIMIT = 56 * 1024 * 1024
LOG2E = 1.4426950408889634
SB_DEAD_LOG2 = 160.0


def _params(*sem):
    return pltpu.CompilerParams(dimension_semantics=sem, vmem_limit_bytes=VMEM_LIMIT)


def _rms(x, gain):
    ms = jnp.mean(x * x, axis=-1, keepdims=True)
    return x * lax.rsqrt(ms + NORM_EPS) * gain


def _split_bf16(x, passes):
    parts = []
    r = x
    for _ in range(passes):
        h = r.astype(BF16)
        parts.append(h)
        r = r - h.astype(F32)
    return parts


def _resident(arr):
    nd = arr.ndim
    return pl.BlockSpec(arr.shape, lambda *_: (0,) * nd, pipeline_mode=pl.Buffered(1))


def _two_stream(body, name, ins_p, ins_s, consts, outs, tile=TOKEN_TILE):
    n_in, n_c, n_out = len(ins_p), len(consts), len(outs)
    p_tiles, s_tiles = N_PROMPT // tile, N_SAMPLE // tile

    def kern(*refs):
        p_in, s_in = refs[:n_in], refs[n_in:2 * n_in]
        c = refs[2 * n_in:2 * n_in + n_c]
        p_out = refs[2 * n_in + n_c:2 * n_in + n_c + n_out]
        s_out = refs[2 * n_in + n_c + n_out:]
        i = pl.program_id(0)

        @pl.when(i < p_tiles)
        def _():
            body(p_in, c, p_out)

        @pl.when(i >= p_tiles)
        def _():
            body(s_in, c, s_out)

    def p_idx(i):
        return jnp.minimum(i, p_tiles - 1)

    def s_idx(i):
        return jnp.maximum(i - p_tiles, 0)

    def spec(arr, tiles, idx):
        block = (arr.shape[0] // tiles,) + arr.shape[1:]
        zeros = (0,) * (arr.ndim - 1)
        return pl.BlockSpec(block, lambda i: (idx(i),) + zeros)

    def out_spec(width, idx):
        return pl.BlockSpec((tile, width), lambda i: (idx(i), 0))

    return pl.pallas_call(
        kern,
        name=name,
        grid=(p_tiles + s_tiles,),
        in_specs=([spec(a, p_tiles, p_idx) for a in ins_p] + [spec(a, s_tiles, s_idx) for a in ins_s]
                  + [_resident(a) for a in consts]),
        out_specs=([out_spec(w, p_idx) for w, _ in outs] + [out_spec(w, s_idx) for w, _ in outs]),
        out_shape=([jax.ShapeDtypeStruct((N_PROMPT, w), dt) for w, dt in outs]
                   + [jax.ShapeDtypeStruct((N_SAMPLE, w), dt) for w, dt in outs]),
        compiler_params=_params("arbitrary"),
    )(*ins_p, *ins_s, *consts)


def _norm_proj_body(ins, consts, outs):
    (x_ref,), (g_ref, w_ref), (o_ref,) = ins, consts, outs
    xn = _rms(x_ref[...], g_ref[...]).astype(BF16)
    dout = o_ref.shape[1]
    for lo in range(0, dout, COL_CHUNK):
        cols = slice(lo, min(lo + COL_CHUNK, dout))
        o_ref[:, cols] = jnp.dot(xn, w_ref[:, cols], preferred_element_type=F32)


def _conv_in_body(ins, consts, outs):
    (x_ref,), (g_ref, w_ref), (gb_ref, u_ref) = ins, consts, outs
    xn = _rms(x_ref[...], g_ref[...]).astype(BF16)
    for lo in range(0, D_MODEL, COL_CHUNK):
        cols = slice(lo, lo + COL_CHUNK)
        cols_c = slice(D_MODEL + lo, D_MODEL + lo + COL_CHUNK)
        cols_x = slice(2 * D_MODEL + lo, 2 * D_MODEL + lo + COL_CHUNK)
        gb_ref[:, cols] = jnp.dot(xn, w_ref[:, cols], preferred_element_type=F32)
        gc = jnp.dot(xn, w_ref[:, cols_c], preferred_element_type=F32)
        xp = jnp.dot(xn, w_ref[:, cols_x], preferred_element_type=F32)
        u_ref[:, cols] = gc * xp


def _conv_mlp_body(ins, consts, outs, *, final_norm):
    (u_ref, prev_ref, gb_ref, r_ref), (cw_ref, w_ref, g_ref, wu_ref, wd_ref, gf_ref), (o_ref,) = (
        ins, consts, outs)
    rows = u_ref.shape[0]
    units = rows // CHUNK
    u = u_ref[...]
    prev = prev_ref[...]
    p0 = jnp.broadcast_to(prev[:, 0:1, :], (units, CHUNK, D_MODEL)).reshape(rows, D_MODEL)
    p1 = jnp.broadcast_to(prev[:, 1:2, :], (units, CHUNK, D_MODEL)).reshape(rows, D_MODEL)
    t = lax.broadcasted_iota(jnp.int32, (rows, 1), 0) % CHUNK
    s1 = jnp.where(t == 0, p1, pltpu.roll(u, 1, axis=0))
    s2 = jnp.where(t == 0, p0, jnp.where(t == 1, p1, pltpu.roll(u, 2, axis=0)))
    cw = cw_ref[...]
    y = cw[0:1, :] * s2 + cw[1:2, :] * s1 + cw[2:3, :] * u
    a = (gb_ref[...] * y).astype(BF16)
    x = r_ref[...] + jnp.dot(a, w_ref[...], preferred_element_type=F32)
    o_ref[...] = _mlp_apply(x, g_ref, wu_ref, wd_ref, gf_ref, final_norm)


def _mlp_apply(x, g_ref, wu_ref, wd_ref, gf_ref, final_norm):
    xn = _rms(x, g_ref[...]).astype(BF16)
    acc = x
    for lo in range(0, MLP_HIDDEN, COL_CHUNK):
        cols = slice(lo, lo + COL_CHUNK)
        h = jnp.maximum(jnp.dot(xn, wu_ref[:, cols], preferred_element_type=F32), 0.0)
        acc = acc + jnp.dot((h * h).astype(BF16), wd_ref[cols, :], preferred_element_type=F32)
    return _rms(acc, gf_ref[...]) if final_norm else acc


def _proj_mlp_body(ins, consts, outs, *, final_norm):
    (a_ref, r_ref), (w_ref, g_ref, wu_ref, wd_ref, gf_ref), (o_ref,) = ins, consts, outs
    x = r_ref[...] + jnp.dot(a_ref[...], w_ref[...], preferred_element_type=F32)
    o_ref[...] = _mlp_apply(x, g_ref, wu_ref, wd_ref, gf_ref, final_norm)


def _sb_qkv_prompt_kernel(*refs, layer, n_layers, first):
    x_ref, g_ref, wq_ref, wkvt_ref = refs[:4]
    q_ref, kt_ref, vt_ref, ktb_ref, vtb_ref = refs[-5:]
    xn = _rms(x_ref[...], g_ref[...]).astype(BF16)
    for lo in range(0, D_MODEL, COL_CHUNK):
        cols = slice(lo, lo + COL_CHUNK)
        q_ref[:, cols] = jnp.dot(xn, wq_ref[:, cols], preferred_element_type=F32).astype(BF16)
    for out_ref, outb_ref, base in ((kt_ref, ktb_ref, 0), (vt_ref, vtb_ref, D_MODEL)):
        for lo in range(0, D_MODEL, COL_CHUNK):
            rows = slice(lo, lo + COL_CHUNK)
            w_rows = slice(base + lo, base + lo + COL_CHUNK)
            t = lax.dot_general(wkvt_ref[w_rows, :], xn, (((1,), (1,)), ((), ())),
                                preferred_element_type=F32)
            if first:
                for other in range(n_layers):
                    out_ref[other, rows, :] = t if other == layer else jnp.zeros_like(t)
            else:
                out_ref[rows, :] = t
            for kt in range(TOKEN_TILE // SB_TILE):
                outb_ref[kt, rows, :] = t[:, kt * SB_TILE:(kt + 1) * SB_TILE].astype(BF16)


def _sb_qkv_prompt(x, gain, wq, wkvt, layer, n_layers, stacked):
    per_seq = SEQ // TOKEN_TILE
    key_tiles = TOKEN_TILE // SB_TILE
    first = stacked is None
    if first:
        kt_spec = pl.BlockSpec((n_layers, None, D_MODEL, TOKEN_TILE),
                               lambda i: (0, i // per_seq, 0, i % per_seq))
    else:
        kt_spec = pl.BlockSpec((None, None, D_MODEL, TOKEN_TILE),
                               lambda i: (layer, i // per_seq, 0, i % per_seq))
    ktb_spec = pl.BlockSpec((None, key_tiles, D_MODEL, SB_TILE),
                            lambda i: (i // per_seq, i % per_seq, 0, 0))
    kt_shape = jax.ShapeDtypeStruct((n_layers, BATCH, D_MODEL, SEQ), F32)
    ktb_shape = jax.ShapeDtypeStruct((BATCH, SEQ // SB_TILE, D_MODEL, SB_TILE), BF16)
    gain = gain.reshape(1, D_MODEL)
    operands = [x, gain, wq, wkvt] + ([] if first else list(stacked))
    in_specs = [pl.BlockSpec((TOKEN_TILE, D_MODEL), lambda i: (i, 0)),
                _resident(gain), _resident(wq), _resident(wkvt)]
    if not first:
        in_specs += [pl.BlockSpec(memory_space=pl.ANY)] * 2
    return pl.pallas_call(
        functools.partial(_sb_qkv_prompt_kernel, layer=layer, n_layers=n_layers, first=first),
        name="sb_qkv_prompt",
        grid=(P_TILES,),
        in_specs=in_specs,
        out_specs=[pl.BlockSpec((TOKEN_TILE, D_MODEL), lambda i: (i, 0)),
                   kt_spec, kt_spec, ktb_spec, ktb_spec],
        out_shape=[jax.ShapeDtypeStruct((N_PROMPT, D_MODEL), BF16),
                   kt_shape, kt_shape, ktb_shape, ktb_shape],
        input_output_aliases={} if first else {4: 1, 5: 2},
        compiler_params=_params("parallel"),
    )(*operands)


def _sb_qkv_sample_kernel(*refs, layer, n_layers, first):
    x_ref, g_ref, wq_ref, wkv_ref = refs[:4]
    q_ref, k_ref, v_ref, kvb_ref = refs[-4:]
    xn = _rms(x_ref[...], g_ref[...]).astype(BF16)
    for lo in range(0, D_MODEL, COL_CHUNK):
        cols = slice(lo, lo + COL_CHUNK)
        q_ref[:, cols] = jnp.dot(xn, wq_ref[:, cols], preferred_element_type=F32).astype(BF16)
    for out_ref, base in ((k_ref, 0), (v_ref, D_MODEL)):
        for lo in range(0, D_MODEL, COL_CHUNK):
            cols = slice(lo, lo + COL_CHUNK)
            w_cols = slice(base + lo, base + lo + COL_CHUNK)
            t = jnp.dot(xn, wkv_ref[:, w_cols], preferred_element_type=F32)
            if first:
                for other in range(n_layers):
                    out_ref[other, :, cols] = t if other == layer else jnp.zeros_like(t)
            else:
                out_ref[:, cols] = t
            kvb_ref[:, w_cols] = t.astype(BF16)


def _sb_qkv_sample(x, gain, wq, wkv, layer, n_layers, stacked):
    first = stacked is None
    tok = pl.BlockSpec((TOKEN_TILE, D_MODEL), lambda i: (i, 0))
    if first:
        kv_spec = pl.BlockSpec((n_layers, TOKEN_TILE, D_MODEL), lambda i: (0, i, 0))
    else:
        kv_spec = pl.BlockSpec((None, TOKEN_TILE, D_MODEL), lambda i: (layer, i, 0))
    kv_shape = jax.ShapeDtypeStruct((n_layers, N_SAMPLE, D_MODEL), F32)
    gain = gain.reshape(1, D_MODEL)
    operands = [x, gain, wq, wkv] + ([] if first else list(stacked))
    in_specs = [tok, _resident(gain), _resident(wq), _resident(wkv)]
    if not first:
        in_specs += [pl.BlockSpec(memory_space=pl.ANY)] * 2
    return pl.pallas_call(
        functools.partial(_sb_qkv_sample_kernel, layer=layer, n_layers=n_layers, first=first),
        name="sb_qkv_sample",
        grid=(S_TILES,),
        in_specs=in_specs,
        out_specs=[tok, kv_spec, kv_spec,
                   pl.BlockSpec((TOKEN_TILE, 2 * D_MODEL), lambda i: (i, 0))],
        out_shape=[jax.ShapeDtypeStruct((N_SAMPLE, D_MODEL), BF16), kv_shape, kv_shape,
                   jax.ShapeDtypeStruct((N_SAMPLE, 2 * D_MODEL), BF16)],
        input_output_aliases={} if first else {4: 1, 5: 2},
        compiler_params=_params("parallel"),
    )(*operands)


def _sb_tile(qst, k, v, upper2, mask, carry, acc, key_minor):
    nt = (((1,), (1,)), ((), ()))
    if key_minor:
        z = jnp.dot(qst, k, preferred_element_type=F32)
    else:
        z = lax.dot_general(qst, k, nt, preferred_element_type=F32)
    z2 = z * LOG2E
    sp = jnp.maximum(z2, 0.0) + jnp.log2(1.0 + jnp.exp2(-jnp.abs(z2)))
    spm = sp if mask is None else jnp.where(mask, sp, 0.0)
    parts = _split_bf16(spm, SB_SUFFIX_PASSES)
    below = jnp.dot(jnp.concatenate(parts, axis=1), upper2, preferred_element_type=F32) + carry
    a = jnp.exp2(z2 - sp - below)
    if mask is not None:
        a = jnp.where(mask, a, 0.0)
    a = a.astype(BF16)
    if key_minor:
        acc = acc + lax.dot_general(a, v, nt, preferred_element_type=F32)
    else:
        acc = acc + jnp.dot(a, v, preferred_element_type=F32)
    carry = below[:, 0:1] + spm[:, 0:1]
    return carry, acc


def _suffix_matrix2(n):
    s = lax.broadcasted_iota(jnp.int32, (SB_SUFFIX_PASSES * n, n), 0) % n
    j = lax.broadcasted_iota(jnp.int32, (SB_SUFFIX_PASSES * n, n), 1)
    return (s > j).astype(BF16)


def _split_heads(q):
    lane = lax.broadcasted_iota(jnp.int32, (1, SB_PAIR), 1)
    zero = jnp.zeros_like(q)
    return jnp.where(lane < SB_HEAD_DIM, q, zero), jnp.where(lane >= SB_HEAD_DIM, q, zero)


def _stack_heads(q):
    return jnp.concatenate(_split_heads(q), axis=0)


def _unstack_heads(acc):
    lane = lax.broadcasted_iota(jnp.int32, (1, SB_PAIR), 1)
    m = acc.shape[0] // 2
    return jnp.where(lane < SB_HEAD_DIM, acc[:m], acc[m:])


def _causal_mask(tq, tk):
    t = lax.broadcasted_iota(jnp.int32, (2 * tq, tk), 0) % tq
    s = lax.broadcasted_iota(jnp.int32, (2 * tq, tk), 1)
    return s < t


def _sb_prompt_kernel(q_ref, kt_ref, vt_ref, o_ref):
    i = pl.program_id(2)
    pairs = range(SB_PROMPT_PAIRS)
    lanes = [slice(pp * SB_PAIR, (pp + 1) * SB_PAIR) for pp in pairs]
    qst = [_stack_heads(q_ref[:, lanes[pp]]) for pp in pairs]
    upper2 = _suffix_matrix2(SB_TILE)
    causal = _causal_mask(SB_TILE, SB_TILE)
    zero_c = jnp.zeros((2 * SB_TILE, 1), F32)
    zero_a = jnp.zeros((2 * SB_TILE, SB_PAIR), F32)

    def tile(pp, j, mask, carry, acc):
        return _sb_tile(qst[pp], kt_ref[j, lanes[pp], :], vt_ref[j, lanes[pp], :], upper2, mask,
                        carry, acc, True)

    def store(pp, acc):
        o_ref[:, lanes[pp]] = _unstack_heads(acc).astype(o_ref.dtype)

    @pl.when(i == 0)
    def _():
        for pp in pairs:
            store(pp, tile(pp, 0, causal, zero_c, zero_a)[1])

    @pl.when(i > 0)
    def _():
        state = []
        for pp in pairs:
            carry, acc = tile(pp, i, causal, zero_c, zero_a)
            state += tile(pp, i - 1, None, carry, acc)

        def live(c):
            dead = c[1]
            for pp in pairs[1:]:
                dead = jnp.minimum(dead, c[1 + 2 * pp])
            return (c[0] < i) & (jnp.min(dead) < SB_DEAD_LOG2)

        def body(c):
            j = i - 1 - c[0]
            out = (c[0] + 1,)
            for pp in pairs:
                out += tile(pp, j, None, c[1 + 2 * pp], c[2 + 2 * pp])
            return out

        final = lax.while_loop(live, body, (jnp.int32(1),) + tuple(state))
        for pp in pairs:
            store(pp, final[2 + 2 * pp])


def _sb_prompt(q, ktb, vtb):
    nq = SEQ // SB_TILE
    width = SB_PROMPT_PAIRS * SB_PAIR
    kv_spec = pl.BlockSpec((None, nq, width, SB_TILE), lambda b, p, i: (b, 0, p, 0))
    return pl.pallas_call(
        _sb_prompt_kernel,
        name="sb_prompt",
        grid=(BATCH, D_MODEL // width, nq),
        in_specs=[pl.BlockSpec((SB_TILE, width), lambda b, p, i: (b * nq + i, p)),
                  kv_spec, kv_spec],
        out_specs=pl.BlockSpec((SB_TILE, width), lambda b, p, i: (b * nq + i, p)),
        out_shape=jax.ShapeDtypeStruct((N_PROMPT, D_MODEL), BF16),
        compiler_params=_params("parallel", "parallel", "arbitrary"),
    )(q, ktb, vtb)


def _sb_sample_kernel(q_ref, kn_ref, vn_ref, ck_ref, cv_ref, o_ref, carry_ref, acc_ref):
    upper2 = _suffix_matrix2(SB_TILE)
    n_cache = PAST_LEN // SB_TILE
    new_w = SB_PAIR
    first_w = SB_TILE + new_w

    def cache_kv(pp, j):
        feats = slice(pp * SB_PAIR, (pp + 1) * SB_PAIR)
        keys = slice(j * SB_TILE, (j + 1) * SB_TILE)
        return ck_ref[feats, keys].astype(BF16), cv_ref[feats, keys].astype(BF16)

    def key_minor(new):
        padded = jnp.concatenate([new, jnp.zeros((new_w - DEC_SEQ, SB_PAIR), new.dtype)], axis=0)
        return padded.astype(F32).T.astype(BF16)

    t = lax.broadcasted_iota(jnp.int32, (2 * DEC_SEQ, first_w), 0) % DEC_SEQ
    s = lax.broadcasted_iota(jnp.int32, (2 * DEC_SEQ, first_w), 1) - SB_TILE
    first_mask = s < t
    upper_first = _suffix_matrix2(first_w)

    qsts = []
    for pp in range(SB_SAMPLE_PAIRS):
        lanes = slice(pp * SB_PAIR, (pp + 1) * SB_PAIR)
        qst = _stack_heads(q_ref[:, lanes])
        ck, cv = cache_kv(pp, n_cache - 1)
        k_first = jnp.concatenate([ck, key_minor(kn_ref[:, lanes])], axis=1)
        v_first = jnp.concatenate([cv, key_minor(vn_ref[:, lanes])], axis=1)
        carry_ref[pp], acc_ref[pp] = _sb_tile(
            qst, k_first, v_first, upper_first, first_mask,
            jnp.zeros((2 * DEC_SEQ, 1), F32), jnp.zeros((2 * DEC_SEQ, SB_PAIR), F32), True)
        qsts.append(qst)

    def cache_tile(pp, j, qst, carry, acc):
        ck, cv = cache_kv(pp, j)
        return _sb_tile(qst, ck, cv, upper2, None, carry, acc, True)

    for j in reversed(range(n_cache - 1)):
        @pl.when(jnp.min(carry_ref[...]) < SB_DEAD_LOG2)
        def _():
            for pp in range(SB_SAMPLE_PAIRS):
                carry_ref[pp], acc_ref[pp] = cache_tile(pp, j, qsts[pp], carry_ref[pp], acc_ref[pp])

    for pp in range(SB_SAMPLE_PAIRS):
        lanes = slice(pp * SB_PAIR, (pp + 1) * SB_PAIR)
        o_ref[:, lanes] = _unstack_heads(acc_ref[pp]).astype(o_ref.dtype)


def _sb_sample(q, kvb, cache_kt, cache_vt, layer):
    width = SB_SAMPLE_PAIRS * SB_PAIR
    steps = D_MODEL // width
    cache_spec = pl.BlockSpec((None, None, width, PAST_LEN), lambda b, p: (layer, b, p, 0))
    return pl.pallas_call(
        _sb_sample_kernel,
        name="sb_sample",
        grid=(DEC_BATCH, steps),
        in_specs=[pl.BlockSpec((DEC_SEQ, width), lambda b, p: (b, p)),
                  pl.BlockSpec((DEC_SEQ, width), lambda b, p: (b, p)),
                  pl.BlockSpec((DEC_SEQ, width), lambda b, p: (b, steps + p)),
                  cache_spec, cache_spec],
        out_specs=pl.BlockSpec((DEC_SEQ, width), lambda b, p: (b, p)),
        out_shape=jax.ShapeDtypeStruct((N_SAMPLE, D_MODEL), BF16),
        scratch_shapes=[pltpu.VMEM((SB_SAMPLE_PAIRS, 2 * DEC_SEQ, 1), F32),
                        pltpu.VMEM((SB_SAMPLE_PAIRS, 2 * DEC_SEQ, SB_PAIR), F32)],
        compiler_params=_params("parallel", "parallel"),
    )(q, kvb, kvb, cache_kt, cache_vt)


def _gla_kernel(q_ref, k_ref, v_ref, go_ref, gl_ref, s0_ref, wgk_ref, bgk_ref, nw_ref,
                o_ref, sout_ref, st_ref, *, nchunks):
    if nchunks > 1:
        @pl.when(pl.program_id(1) == 0)
        def _():
            for sq in range(GLA_SEQS):
                for h in range(GLA_HEADS):
                    st_ref[sq, h] = s0_ref[sq, h].T

    states = [_gla_seq(q_ref.at[sq], k_ref.at[sq], v_ref.at[sq], go_ref.at[sq], gl_ref.at[sq],
                       s0_ref.at[sq], wgk_ref, bgk_ref, nw_ref, o_ref.at[sq], st_ref.at[sq],
                       nchunks)
              for sq in range(GLA_SEQS)]

    def write_states():
        for sq in range(GLA_SEQS):
            for h in range(GLA_HEADS):
                sout_ref[sq, h] = states[sq][h].T

    if nchunks > 1:
        pl.when(pl.program_id(1) == nchunks - 1)(write_states)
    else:
        write_states()


def _gla_seq(q_ref, k_ref, v_ref, go_ref, gl_ref, s0_ref, wgk_ref, bgk_ref, nw_ref,
             o_ref, st_ref, nchunks):
    nt = (((1,), (1,)), ((), ()))
    q = q_ref[...] * (GLA_DK ** -0.5)
    k = k_ref[...]
    gate_in = jnp.dot(gl_ref[...].astype(BF16), wgk_ref[...].astype(BF16),
                      preferred_element_type=F32) + bgk_ref[...]
    g = (jnp.minimum(gate_in, 0.0) - jnp.log1p(jnp.exp(-jnp.abs(gate_in)))) / GLA_GATE_NORMALIZER

    t_i = lax.broadcasted_iota(jnp.int32, (CHUNK, CHUNK), 0)
    s_i = lax.broadcasted_iota(jnp.int32, (CHUNK, CHUNK), 1)
    lower = (s_i <= t_i).astype(BF16)
    b = jnp.zeros((CHUNK, GLA_KEY_DIM), F32)
    for part in _split_bf16(g, 3):
        b = b + jnp.dot(lower, part, preferred_element_type=F32)

    qe = (q * jnp.exp(b)).astype(BF16)
    b_last = b[CHUNK - 1:CHUNK]
    kd = (k * jnp.exp(b_last - b)).astype(BF16)
    decay_last = jnp.exp(b_last)

    key_row = lax.broadcasted_iota(jnp.int32, (CHUNK, 1), 0)
    sub_row = lax.broadcasted_iota(jnp.int32, (GLA_SUB, 1), 0)
    key_lane = lax.broadcasted_iota(jnp.int32, (1, CHUNK), 1)
    n_sub = CHUNK // GLA_SUB
    qt, kt = [None] * n_sub, [None] * n_sub
    for blk in range(1, n_sub):
        lo = blk * GLA_SUB
        ref = b[lo - 1:lo]
        qt[blk] = (q[lo:lo + GLA_SUB] * jnp.exp(b[lo:lo + GLA_SUB] - ref)).astype(BF16)
        kt[blk] = (k * jnp.exp(jnp.where(key_row < lo, ref - b, -jnp.inf))).astype(BF16)

    feat_head = lax.broadcasted_iota(jnp.int32, (GLA_KEY_DIM, GLA_HEADS * CHUNK), 0) // GLA_DK
    out_head = lax.broadcasted_iota(jnp.int32, (GLA_KEY_DIM, GLA_HEADS * CHUNK), 1) // CHUNK
    head_sum = (feat_head == out_head).astype(BF16)
    out_key = lax.broadcasted_iota(jnp.int32, (1, GLA_HEADS * CHUNK), 1) % CHUNK
    diag_rows = []
    for blk in range(n_sub):
        rows = slice(blk * GLA_SUB, (blk + 1) * GLA_SUB)
        qi, ki, bi = q[rows], k[rows], b[rows]
        prods = []
        for s in range(GLA_SUB):
            diff = jnp.where(sub_row >= s, bi - bi[s:s + 1], -jnp.inf)
            prods.append((qi * (ki[s:s + 1] * jnp.exp(diff))).astype(BF16))
        sums = jnp.dot(jnp.concatenate(prods, axis=0), head_sum, preferred_element_type=F32)
        a_blk = jnp.zeros((GLA_SUB, GLA_HEADS * CHUNK), F32)
        for s in range(GLA_SUB):
            a_blk = jnp.where(out_key == blk * GLA_SUB + s,
                              sums[s * GLA_SUB:(s + 1) * GLA_SUB], a_blk)
        diag_rows.append(a_blk)
    a_diag = jnp.concatenate(diag_rows, axis=0)

    nw = nw_ref[...]
    new_states = []
    for h in range(GLA_HEADS):
        kl = slice(h * GLA_DK, (h + 1) * GLA_DK)
        vl = slice(h * GLA_DV, (h + 1) * GLA_DV)
        v = v_ref[:, vl].astype(BF16)
        st = st_ref[h] if nchunks > 1 else s0_ref[h].T
        a_rows = [jnp.zeros((GLA_SUB, CHUNK), F32)]
        for blk in range(1, n_sub):
            a_rows.append(lax.dot_general(qt[blk][:, kl], kt[blk][:, kl], nt,
                                          preferred_element_type=F32))
        a = (jnp.concatenate(a_rows, axis=0) + a_diag[:, h * CHUNK:(h + 1) * CHUNK]).astype(BF16)
        o = (lax.dot_general(qe[:, kl], st.astype(BF16), nt, preferred_element_type=F32)
             + jnp.dot(a, v, preferred_element_type=F32))
        st_new = st * decay_last[:, kl] + lax.dot_general(
            v, kd[:, kl], (((0,), (0,)), ((), ())), preferred_element_type=F32)
        if nchunks > 1:
            st_ref[h] = st_new
        new_states.append(st_new)

        o = o * lax.rsqrt(jnp.mean(o * o, axis=-1, keepdims=True) + NORM_EPS) * nw
        go = go_ref[:, vl]
        o_ref[:, vl] = (o * (go * jax.nn.sigmoid(go))).astype(o_ref.dtype)
    return new_states


def _gla(proj, state, w_gk2, b_gk, norm_w, nseq, nchunks):
    def rows(width, col):
        return pl.BlockSpec((GLA_SEQS, CHUNK, width), lambda b, c: (b, c, col))

    state_spec = pl.BlockSpec((GLA_SEQS, GLA_HEADS, GLA_DK, GLA_DV), lambda b, c: (b, 0, 0, 0))
    w_gk2 = jnp.pad(w_gk2, ((0, LANES - GLA_GATE_RANK), (0, 0)))
    b_gk = b_gk.reshape(1, GLA_KEY_DIM)
    norm_w = norm_w.reshape(1, GLA_DV)
    proj = proj.reshape(nseq, nchunks * CHUNK, GLA_IN_PAD)
    o, s_out = pl.pallas_call(
        functools.partial(_gla_kernel, nchunks=nchunks),
        name="gla",
        grid=(nseq // GLA_SEQS, nchunks),
        in_specs=[
            rows(GLA_KEY_DIM, 0), rows(GLA_KEY_DIM, 1), rows(GLA_VALUE_DIM, 1),
            rows(GLA_VALUE_DIM, 2), rows(LANES, GLA_IN_PAD // LANES - 1),
            state_spec,
            _resident(w_gk2), _resident(b_gk), _resident(norm_w),
        ],
        out_specs=[rows(GLA_VALUE_DIM, 0), state_spec],
        out_shape=[
            jax.ShapeDtypeStruct((nseq, nchunks * CHUNK, GLA_VALUE_DIM), BF16),
            jax.ShapeDtypeStruct((nseq, GLA_HEADS, GLA_DK, GLA_DV), F32),
        ],
        scratch_shapes=[pltpu.VMEM((GLA_SEQS, GLA_HEADS, GLA_DV, GLA_DK), F32)],
        compiler_params=_params("parallel", "arbitrary"),
    )(proj, proj, proj, proj, proj, state, w_gk2, b_gk, norm_w)
    return o.reshape(nseq * nchunks * CHUNK, GLA_VALUE_DIM), s_out


def kernel(x_prompt, x_sample, cache_sb_k, cache_sb_v, state_conv, state_gla, norm_mix, norm_mlp,
           sb_w_qkv, sb_w_o, conv_w_in, conv_w, conv_w_out, gla_w_in, gla_w_gk2, gla_b_gk,
           gla_norm, gla_w_o, mlp_w_up, mlp_w_down, norm_final):
    xp = x_prompt.reshape(N_PROMPT, D_MODEL)
    xs = x_sample.reshape(N_SAMPLE, D_MODEL)
    n_sb = cache_sb_k.shape[0]
    cache_kt = cache_sb_k.transpose(0, 1, 3, 4, 2).reshape(n_sb, DEC_BATCH, D_MODEL, PAST_LEN)
    cache_vt = cache_sb_v.transpose(0, 1, 3, 4, 2).reshape(n_sb, DEC_BATCH, D_MODEL, PAST_LEN)
    sb_kv_p, sb_kv_s = None, None
    conv_p, conv_s, gla_p, gla_s = [], [], [], []

    def vec(a):
        return a.reshape(1, -1)

    for i in range(DEPTH):
        j = i // N_MIXERS
        final = i == DEPTH - 1
        mlp_consts = [vec(norm_mlp[i]), mlp_w_up[i].astype(BF16), mlp_w_down[i].astype(BF16),
                      vec(norm_final)]

        def proj_mlp(name, a_p, a_s, w_o):
            return _two_stream(functools.partial(_proj_mlp_body, final_norm=final), name,
                               [a_p, xp], [a_s, xs], [w_o.astype(BF16)] + mlp_consts,
                               [(D_MODEL, F32)])

        if i % N_MIXERS == 0:
            w = sb_w_qkv[j]
            wq = (w[:, :D_MODEL] * (SB_HEAD_DIM ** -0.5)).astype(BF16)
            wkv = w[:, D_MODEL:].astype(BF16)
            q_p, kt, vt, ktb, vtb = _sb_qkv_prompt(xp, norm_mix[i], wq, wkv.T, j, n_sb, sb_kv_p)
            sb_kv_p = (kt, vt)
            q_s, k_s, v_s, kvb_s = _sb_qkv_sample(xs, norm_mix[i], wq, wkv, j, n_sb, sb_kv_s)
            sb_kv_s = (k_s, v_s)
            o_p = _sb_prompt(q_p, ktb, vtb)
            o_s = _sb_sample(q_s, kvb_s, cache_kt, cache_vt, j)
            xp, xs = proj_mlp("sb_out_mlp", o_p, o_s, sb_w_o[j])
        elif i % N_MIXERS == 1:
            gb_p, u_p, gb_s, u_s = _two_stream(
                _conv_in_body, "conv_in", [xp], [xs],
                [vec(norm_mix[i]), conv_w_in[j].astype(BF16)], [(D_MODEL, F32), (D_MODEL, F32)])
            keep = CONV_WIDTH - 1
            tails_p = u_p.reshape(BATCH, SEQ // CHUNK, CHUNK, D_MODEL)[:, :, CHUNK - keep:]
            prev_p = jnp.concatenate(
                [jnp.zeros((BATCH, 1, keep, D_MODEL), F32), tails_p[:, :-1]], axis=1)
            prev_p = prev_p.reshape(N_PROMPT // CHUNK, keep, D_MODEL)
            xp, xs = _two_stream(functools.partial(_conv_mlp_body, final_norm=final),
                                 "conv_out_mlp",
                                 [u_p, prev_p, gb_p, xp], [u_s, state_conv[j], gb_s, xs],
                                 [conv_w[j], conv_w_out[j].astype(BF16)] + mlp_consts,
                                 [(D_MODEL, F32)], tile=CONV_TILE)
            conv_p.append(tails_p[:, -1])
            conv_s.append(u_s.reshape(DEC_BATCH, DEC_SEQ, D_MODEL)[:, DEC_SEQ - keep:])
        else:
            w_in = jnp.pad(gla_w_in[j], ((0, 0), (0, GLA_IN_PAD - GLA_IN_DIM))).astype(BF16)
            proj_p, proj_s = _two_stream(_norm_proj_body, "gla_in", [xp], [xs],
                                         [vec(norm_mix[i]), w_in], [(GLA_IN_PAD, F32)])
            zero_state = jnp.zeros((BATCH, GLA_HEADS, GLA_DK, GLA_DV), F32)
            o_p, s_p = _gla(proj_p, zero_state, gla_w_gk2[j], gla_b_gk[j], gla_norm[j],
                            BATCH, SEQ // CHUNK)
            o_s, s_s = _gla(proj_s, state_gla[j], gla_w_gk2[j], gla_b_gk[j], gla_norm[j],
                            DEC_BATCH, DEC_SEQ // CHUNK)
            xp, xs = proj_mlp("gla_out_mlp", o_p, o_s, gla_w_o[j])
            gla_p.append(s_p)
            gla_s.append(s_s)

    def prompt_heads(a):
        return a.reshape(n_sb, BATCH, SB_HEADS, SB_HEAD_DIM, SEQ).transpose(0, 1, 4, 2, 3)

    def sample_heads(a):
        return a.reshape(n_sb, DEC_BATCH, DEC_SEQ, SB_HEADS, SB_HEAD_DIM)

    return (xp.reshape(BATCH, SEQ, D_MODEL), xs.reshape(DEC_BATCH, DEC_SEQ, D_MODEL),
            prompt_heads(sb_kv_p[0]), prompt_heads(sb_kv_p[1]),
            sample_heads(sb_kv_s[0]), sample_heads(sb_kv_s[1]),
            jnp.stack(conv_p), jnp.stack(conv_s), jnp.stack(gla_p), jnp.stack(gla_s))
```

```python
import functools

import jax
import jax.numpy as jnp
from jax import lax
from jax.experimental import pallas as pl
from jax.experimental.pallas import tpu as pltpu

F32 = jnp.float32
BF16 = jnp.bfloat16

D_MODEL = 1024
BATCH = 8
SEQ = 2048
DEPTH = 4
DEC_BATCH = 32
DEC_SEQ = 64
PAST_LEN = 1024
CHUNK = 64
N_MIXERS = 3
SB_HEADS = 16
SB_HEAD_DIM = D_MODEL // SB_HEADS
CONV_WIDTH = 3
GLA_HEADS = 4
GLA_KEY_DIM = D_MODEL // 2
GLA_VALUE_DIM = D_MODEL
GLA_DK = GLA_KEY_DIM // GLA_HEADS
GLA_DV = GLA_VALUE_DIM // GLA_HEADS
GLA_GATE_RANK = 16
GLA_GATE_NORMALIZER = 16.0
GLA_IN_DIM = 2 * GLA_KEY_DIM + 2 * GLA_VALUE_DIM + GLA_GATE_RANK
MLP_HIDDEN = 4 * D_MODEL
NORM_EPS = 1e-6

N_PROMPT = BATCH * SEQ
N_SAMPLE = DEC_BATCH * DEC_SEQ

LANES = 128
GLA_IN_PAD = 25 * LANES
TOKEN_TILE = 512
P_TILES = N_PROMPT // TOKEN_TILE
S_TILES = N_SAMPLE // TOKEN_TILE
COL_CHUNK = 512
SB_TILE = 256
SB_PAIR = 2 * SB_HEAD_DIM
SB_PAIRS = D_MODEL // SB_PAIR
SB_SUFFIX_PASSES = 1
SB_PROMPT_PAIRS = 4
SB_SAMPLE_PAIRS = 4
GLA_SEQS = 4
GLA_SUB = 16
VMEM_LIMIT = 56 * 1024 * 1024
LOG2E = 1.4426950408889634
SB_DEAD_LOG2 = 160.0


def _params(*sem):
    return pltpu.CompilerParams(dimension_semantics=sem, vmem_limit_bytes=VMEM_LIMIT)


def _rms(x, gain):
    ms = jnp.mean(x * x, axis=-1, keepdims=True)
    return x * lax.rsqrt(ms + NORM_EPS) * gain


def _split_bf16(x, passes):
    parts = []
    r = x
    for _ in range(passes):
        h = r.astype(BF16)
        parts.append(h)
        r = r - h.astype(F32)
    return parts


def _resident(arr):
    nd = arr.ndim
    return pl.BlockSpec(arr.shape, lambda *_: (0,) * nd, pipeline_mode=pl.Buffered(1))


def _two_stream(body, name, ins_p, ins_s, consts, outs, tile=TOKEN_TILE):
    n_in, n_c, n_out = len(ins_p), len(consts), len(outs)
    p_tiles, s_tiles = N_PROMPT // tile, N_SAMPLE // tile

    def kern(*refs):
        p_in, s_in = refs[:n_in], refs[n_in:2 * n_in]
        c = refs[2 * n_in:2 * n_in + n_c]
        p_out = refs[2 * n_in + n_c:2 * n_in + n_c + n_out]
        s_out = refs[2 * n_in + n_c + n_out:]
        i = pl.program_id(0)

        @pl.when(i < p_tiles)
        def _():
            body(p_in, c, p_out)

        @pl.when(i >= p_tiles)
        def _():
            body(s_in, c, s_out)

    def p_idx(i):
        return jnp.minimum(i, p_tiles - 1)

    def s_idx(i):
        return jnp.maximum(i - p_tiles, 0)

    def spec(arr, tiles, idx):
        block = (arr.shape[0] // tiles,) + arr.shape[1:]
        zeros = (0,) * (arr.ndim - 1)
        return pl.BlockSpec(block, lambda i: (idx(i),) + zeros)

    def out_spec(width, idx):
        return pl.BlockSpec((tile, width), lambda i: (idx(i), 0))

    return pl.pallas_call(
        kern,
        name=name,
        grid=(p_tiles + s_tiles,),
        in_specs=([spec(a, p_tiles, p_idx) for a in ins_p] + [spec(a, s_tiles, s_idx) for a in ins_s]
                  + [_resident(a) for a in consts]),
        out_specs=([out_spec(w, p_idx) for w, _ in outs] + [out_spec(w, s_idx) for w, _ in outs]),
        out_shape=([jax.ShapeDtypeStruct((N_PROMPT, w), dt) for w, dt in outs]
                   + [jax.ShapeDtypeStruct((N_SAMPLE, w), dt) for w, dt in outs]),
        compiler_params=_params("arbitrary"),
    )(*ins_p, *ins_s, *consts)


def _norm_proj_body(ins, consts, outs):
    (x_ref,), (g_ref, w_ref), (o_ref,) = ins, consts, outs
    xn = _rms(x_ref[...], g_ref[...]).astype(BF16)
    dout = o_ref.shape[1]
    for lo in range(0, dout, COL_CHUNK):
        cols = slice(lo, min(lo + COL_CHUNK, dout))
        o_ref[:, cols] = jnp.dot(xn, w_ref[:, cols], preferred_element_type=F32)


def _conv_in_body(ins, consts, outs):
    (x_ref,), (g_ref, w_ref), (gb_ref, u_ref) = ins, consts, outs
    xn = _rms(x_ref[...], g_ref[...]).astype(BF16)
    for lo in range(0, D_MODEL, COL_CHUNK):
        cols = slice(lo, lo + COL_CHUNK)
        cols_c = slice(D_MODEL + lo, D_MODEL + lo + COL_CHUNK)
        cols_x = slice(2 * D_MODEL + lo, 2 * D_MODEL + lo + COL_CHUNK)
        gb_ref[:, cols] = jnp.dot(xn, w_ref[:, cols], preferred_element_type=F32)
        gc = jnp.dot(xn, w_ref[:, cols_c], preferred_element_type=F32)
        xp = jnp.dot(xn, w_ref[:, cols_x], preferred_element_type=F32)
        u_ref[:, cols] = gc * xp


def _conv_out_body(ins, consts, outs):
    (u_ref, prev_ref, gb_ref, r_ref), (cw_ref, w_ref), (o_ref,) = ins, consts, outs
    rows = u_ref.shape[0]
    units = rows // CHUNK
    u = u_ref[...]
    prev = prev_ref[...]
    p0 = jnp.broadcast_to(prev[:, 0:1, :], (units, CHUNK, D_MODEL)).reshape(rows, D_MODEL)
    p1 = jnp.broadcast_to(prev[:, 1:2, :], (units, CHUNK, D_MODEL)).reshape(rows, D_MODEL)
    t = lax.broadcasted_iota(jnp.int32, (rows, 1), 0) % CHUNK
    s1 = jnp.where(t == 0, p1, pltpu.roll(u, 1, axis=0))
    s2 = jnp.where(t == 0, p0, jnp.where(t == 1, p1, pltpu.roll(u, 2, axis=0)))
    cw = cw_ref[...]
    y = cw[0:1, :] * s2 + cw[1:2, :] * s1 + cw[2:3, :] * u
    a = (gb_ref[...] * y).astype(BF16)
    o_ref[...] = r_ref[...] + jnp.dot(a, w_ref[...], preferred_element_type=F32)


def _mlp_apply(x, g_ref, wu_ref, wd_ref, gf_ref, final_norm):
    xn = _rms(x, g_ref[...]).astype(BF16)
    acc = x
    for lo in range(0, MLP_HIDDEN, COL_CHUNK):
        cols = slice(lo, lo + COL_CHUNK)
        h = jnp.maximum(jnp.dot(xn, wu_ref[:, cols], preferred_element_type=F32), 0.0)
        acc = acc + jnp.dot((h * h).astype(BF16), wd_ref[cols, :], preferred_element_type=F32)
    return _rms(acc, gf_ref[...]) if final_norm else acc


def _mlp_body(ins, consts, outs, *, final_norm):
    (x_ref,), (g_ref, wu_ref, wd_ref, gf_ref), (o_ref,) = ins, consts, outs
    o_ref[...] = _mlp_apply(x_ref[...], g_ref, wu_ref, wd_ref, gf_ref, final_norm)


def _proj_mlp_body(ins, consts, outs, *, final_norm):
    (a_ref, r_ref), (w_ref, g_ref, wu_ref, wd_ref, gf_ref), (o_ref,) = ins, consts, outs
    x = r_ref[...] + jnp.dot(a_ref[...], w_ref[...], preferred_element_type=F32)
    o_ref[...] = _mlp_apply(x, g_ref, wu_ref, wd_ref, gf_ref, final_norm)


def _sb_qkv_prompt_kernel(*refs, layer, n_layers, first):
    x_ref, g_ref, wq_ref, wkvt_ref = refs[:4]
    q_ref, kt_ref, vt_ref, ktb_ref, vtb_ref = refs[-5:]
    xn = _rms(x_ref[...], g_ref[...]).astype(BF16)
    for lo in range(0, D_MODEL, COL_CHUNK):
        cols = slice(lo, lo + COL_CHUNK)
        q_ref[:, cols] = jnp.dot(xn, wq_ref[:, cols], preferred_element_type=F32).astype(BF16)
    for out_ref, outb_ref, base in ((kt_ref, ktb_ref, 0), (vt_ref, vtb_ref, D_MODEL)):
        for lo in range(0, D_MODEL, COL_CHUNK):
            rows = slice(lo, lo + COL_CHUNK)
            w_rows = slice(base + lo, base + lo + COL_CHUNK)
            t = lax.dot_general(wkvt_ref[w_rows, :], xn, (((1,), (1,)), ((), ())),
                                preferred_element_type=F32)
            if first:
                for other in range(n_layers):
                    out_ref[other, rows, :] = t if other == layer else jnp.zeros_like(t)
            else:
                out_ref[rows, :] = t
            for kt in range(TOKEN_TILE // SB_TILE):
                outb_ref[kt, rows, :] = t[:, kt * SB_TILE:(kt + 1) * SB_TILE].astype(BF16)


def _sb_qkv_prompt(x, gain, wq, wkvt, layer, n_layers, stacked):
    per_seq = SEQ // TOKEN_TILE
    key_tiles = TOKEN_TILE // SB_TILE
    first = stacked is None
    if first:
        kt_spec = pl.BlockSpec((n_layers, None, D_MODEL, TOKEN_TILE),
                               lambda i: (0, i // per_seq, 0, i % per_seq))
    else:
        kt_spec = pl.BlockSpec((None, None, D_MODEL, TOKEN_TILE),
                               lambda i: (layer, i // per_seq, 0, i % per_seq))
    ktb_spec = pl.BlockSpec((None, key_tiles, D_MODEL, SB_TILE),
                            lambda i: (i // per_seq, i % per_seq, 0, 0))
    kt_shape = jax.ShapeDtypeStruct((n_layers, BATCH, D_MODEL, SEQ), F32)
    ktb_shape = jax.ShapeDtypeStruct((BATCH, SEQ // SB_TILE, D_MODEL, SB_TILE), BF16)
    gain = gain.reshape(1, D_MODEL)
    operands = [x, gain, wq, wkvt] + ([] if first else list(stacked))
    in_specs = [pl.BlockSpec((TOKEN_TILE, D_MODEL), lambda i: (i, 0)),
                _resident(gain), _resident(wq), _resident(wkvt)]
    if not first:
        in_specs += [pl.BlockSpec(memory_space=pl.ANY)] * 2
    return pl.pallas_call(
        functools.partial(_sb_qkv_prompt_kernel, layer=layer, n_layers=n_layers, first=first),
        name="sb_qkv_prompt",
        grid=(P_TILES,),
        in_specs=in_specs,
        out_specs=[pl.BlockSpec((TOKEN_TILE, D_MODEL), lambda i: (i, 0)),
                   kt_spec, kt_spec, ktb_spec, ktb_spec],
        out_shape=[jax.ShapeDtypeStruct((N_PROMPT, D_MODEL), BF16),
                   kt_shape, kt_shape, ktb_shape, ktb_shape],
        input_output_aliases={} if first else {4: 1, 5: 2},
        compiler_params=_params("parallel"),
    )(*operands)


def _sb_qkv_sample_kernel(*refs, layer, n_layers, first):
    x_ref, g_ref, wq_ref, wkv_ref = refs[:4]
    q_ref, k_ref, v_ref, kvb_ref = refs[-4:]
    xn = _rms(x_ref[...], g_ref[...]).astype(BF16)
    for lo in range(0, D_MODEL, COL_CHUNK):
        cols = slice(lo, lo + COL_CHUNK)
        q_ref[:, cols] = jnp.dot(xn, wq_ref[:, cols], preferred_element_type=F32).astype(BF16)
    for out_ref, base in ((k_ref, 0), (v_ref, D_MODEL)):
        for lo in range(0, D_MODEL, COL_CHUNK):
            cols = slice(lo, lo + COL_CHUNK)
            w_cols = slice(base + lo, base + lo + COL_CHUNK)
            t = jnp.dot(xn, wkv_ref[:, w_cols], preferred_element_type=F32)
            if first:
                for other in range(n_layers):
                    out_ref[other, :, cols] = t if other == layer else jnp.zeros_like(t)
            else:
                out_ref[:, cols] = t
            kvb_ref[:, w_cols] = t.astype(BF16)


def _sb_qkv_sample(x, gain, wq, wkv, layer, n_layers, stacked):
    first = stacked is None
    tok = pl.BlockSpec((TOKEN_TILE, D_MODEL), lambda i: (i, 0))
    if first:
        kv_spec = pl.BlockSpec((n_layers, TOKEN_TILE, D_MODEL), lambda i: (0, i, 0))
    else:
        kv_spec = pl.BlockSpec((None, TOKEN_TILE, D_MODEL), lambda i: (layer, i, 0))
    kv_shape = jax.ShapeDtypeStruct((n_layers, N_SAMPLE, D_MODEL), F32)
    gain = gain.reshape(1, D_MODEL)
    operands = [x, gain, wq, wkv] + ([] if first else list(stacked))
    in_specs = [tok, _resident(gain), _resident(wq), _resident(wkv)]
    if not first:
        in_specs += [pl.BlockSpec(memory_space=pl.ANY)] * 2
    return pl.pallas_call(
        functools.partial(_sb_qkv_sample_kernel, layer=layer, n_layers=n_layers, first=first),
        name="sb_qkv_sample",
        grid=(S_TILES,),
        in_specs=in_specs,
        out_specs=[tok, kv_spec, kv_spec,
                   pl.BlockSpec((TOKEN_TILE, 2 * D_MODEL), lambda i: (i, 0))],
        out_shape=[jax.ShapeDtypeStruct((N_SAMPLE, D_MODEL), BF16), kv_shape, kv_shape,
                   jax.ShapeDtypeStruct((N_SAMPLE, 2 * D_MODEL), BF16)],
        input_output_aliases={} if first else {4: 1, 5: 2},
        compiler_params=_params("parallel"),
    )(*operands)


def _sb_tile(qst, k, v, upper2, mask, carry, acc, key_minor):
    nt = (((1,), (1,)), ((), ()))
    if key_minor:
        z = jnp.dot(qst, k, preferred_element_type=F32)
    else:
        z = lax.dot_general(qst, k, nt, preferred_element_type=F32)
    z2 = z * LOG2E
    sp = jnp.maximum(z2, 0.0) + jnp.log2(1.0 + jnp.exp2(-jnp.abs(z2)))
    spm = sp if mask is None else jnp.where(mask, sp, 0.0)
    parts = _split_bf16(spm, SB_SUFFIX_PASSES)
    below = jnp.dot(jnp.concatenate(parts, axis=1), upper2, preferred_element_type=F32) + carry
    a = jnp.exp2(z2 - sp - below)
    if mask is not None:
        a = jnp.where(mask, a, 0.0)
    a = a.astype(BF16)
    if key_minor:
        acc = acc + lax.dot_general(a, v, nt, preferred_element_type=F32)
    else:
        acc = acc + jnp.dot(a, v, preferred_element_type=F32)
    carry = below[:, 0:1] + spm[:, 0:1]
    return carry, acc


def _suffix_matrix2(n):
    s = lax.broadcasted_iota(jnp.int32, (SB_SUFFIX_PASSES * n, n), 0) % n
    j = lax.broadcasted_iota(jnp.int32, (SB_SUFFIX_PASSES * n, n), 1)
    return (s > j).astype(BF16)


def _split_heads(q):
    lane = lax.broadcasted_iota(jnp.int32, (1, SB_PAIR), 1)
    zero = jnp.zeros_like(q)
    return jnp.where(lane < SB_HEAD_DIM, q, zero), jnp.where(lane >= SB_HEAD_DIM, q, zero)


def _stack_heads(q):
    return jnp.concatenate(_split_heads(q), axis=0)


def _unstack_heads(acc):
    lane = lax.broadcasted_iota(jnp.int32, (1, SB_PAIR), 1)
    m = acc.shape[0] // 2
    return jnp.where(lane < SB_HEAD_DIM, acc[:m], acc[m:])


def _causal_mask(tq, tk):
    t = lax.broadcasted_iota(jnp.int32, (2 * tq, tk), 0) % tq
    s = lax.broadcasted_iota(jnp.int32, (2 * tq, tk), 1)
    return s < t


def _sb_prompt_kernel(q_ref, kt_ref, vt_ref, o_ref):
    i = pl.program_id(2)
    pairs = range(SB_PROMPT_PAIRS)
    lanes = [slice(pp * SB_PAIR, (pp + 1) * SB_PAIR) for pp in pairs]
    qst = [_stack_heads(q_ref[:, lanes[pp]]) for pp in pairs]
    upper2 = _suffix_matrix2(SB_TILE)
    causal = _causal_mask(SB_TILE, SB_TILE)
    zero_c = jnp.zeros((2 * SB_TILE, 1), F32)
    zero_a = jnp.zeros((2 * SB_TILE, SB_PAIR), F32)

    def tile(pp, j, mask, carry, acc):
        return _sb_tile(qst[pp], kt_ref[j, lanes[pp], :], vt_ref[j, lanes[pp], :], upper2, mask,
                        carry, acc, True)

    def store(pp, acc):
        o_ref[:, lanes[pp]] = _unstack_heads(acc).astype(o_ref.dtype)

    @pl.when(i == 0)
    def _():
        for pp in pairs:
            store(pp, tile(pp, 0, causal, zero_c, zero_a)[1])

    @pl.when(i > 0)
    def _():
        state = []
        for pp in pairs:
            carry, acc = tile(pp, i, causal, zero_c, zero_a)
            state += tile(pp, i - 1, None, carry, acc)

        def live(c):
            dead = c[1]
            for pp in pairs[1:]:
                dead = jnp.minimum(dead, c[1 + 2 * pp])
            return (c[0] < i) & (jnp.min(dead) < SB_DEAD_LOG2)

        def body(c):
            j = i - 1 - c[0]
            out = (c[0] + 1,)
            for pp in pairs:
                out += tile(pp, j, None, c[1 + 2 * pp], c[2 + 2 * pp])
            return out

        final = lax.while_loop(live, body, (jnp.int32(1),) + tuple(state))
        for pp in pairs:
            store(pp, final[2 + 2 * pp])


def _sb_prompt(q, ktb, vtb):
    nq = SEQ // SB_TILE
    width = SB_PROMPT_PAIRS * SB_PAIR
    kv_spec = pl.BlockSpec((None, nq, width, SB_TILE), lambda b, p, i: (b, 0, p, 0))
    return pl.pallas_call(
        _sb_prompt_kernel,
        name="sb_prompt",
        grid=(BATCH, D_MODEL // width, nq),
        in_specs=[pl.BlockSpec((SB_TILE, width), lambda b, p, i: (b * nq + i, p)),
                  kv_spec, kv_spec],
        out_specs=pl.BlockSpec((SB_TILE, width), lambda b, p, i: (b * nq + i, p)),
        out_shape=jax.ShapeDtypeStruct((N_PROMPT, D_MODEL), BF16),
        compiler_params=_params("parallel", "parallel", "arbitrary"),
    )(q, ktb, vtb)


def _sb_sample_kernel(q_ref, kn_ref, vn_ref, ck_ref, cv_ref, o_ref, carry_ref, acc_ref):
    upper2 = _suffix_matrix2(SB_TILE)
    n_cache = PAST_LEN // SB_TILE
    new_w = SB_PAIR
    first_w = SB_TILE + new_w

    def cache_kv(pp, j):
        feats = slice(pp * SB_PAIR, (pp + 1) * SB_PAIR)
        keys = slice(j * SB_TILE, (j + 1) * SB_TILE)
        return ck_ref[feats, keys].astype(BF16), cv_ref[feats, keys].astype(BF16)

    def key_minor(new):
        padded = jnp.concatenate([new, jnp.zeros((new_w - DEC_SEQ, SB_PAIR), new.dtype)], axis=0)
        return padded.astype(F32).T.astype(BF16)

    t = lax.broadcasted_iota(jnp.int32, (2 * DEC_SEQ, first_w), 0) % DEC_SEQ
    s = lax.broadcasted_iota(jnp.int32, (2 * DEC_SEQ, first_w), 1) - SB_TILE
    first_mask = s < t
    upper_first = _suffix_matrix2(first_w)

    qsts = []
    for pp in range(SB_SAMPLE_PAIRS):
        lanes = slice(pp * SB_PAIR, (pp + 1) * SB_PAIR)
        qst = _stack_heads(q_ref[:, lanes])
        ck, cv = cache_kv(pp, n_cache - 1)
        k_first = jnp.concatenate([ck, key_minor(kn_ref[:, lanes])], axis=1)
        v_first = jnp.concatenate([cv, key_minor(vn_ref[:, lanes])], axis=1)
        carry_ref[pp], acc_ref[pp] = _sb_tile(
            qst, k_first, v_first, upper_first, first_mask,
            jnp.zeros((2 * DEC_SEQ, 1), F32), jnp.zeros((2 * DEC_SEQ, SB_PAIR), F32), True)
        qsts.append(qst)

    def cache_tile(pp, j, qst, carry, acc):
        ck, cv = cache_kv(pp, j)
        return _sb_tile(qst, ck, cv, upper2, None, carry, acc, True)

    for j in reversed(range(n_cache - 1)):
        @pl.when(jnp.min(carry_ref[...]) < SB_DEAD_LOG2)
        def _():
            for pp in range(SB_SAMPLE_PAIRS):
                carry_ref[pp], acc_ref[pp] = cache_tile(pp, j, qsts[pp], carry_ref[pp], acc_ref[pp])

    for pp in range(SB_SAMPLE_PAIRS):
        lanes = slice(pp * SB_PAIR, (pp + 1) * SB_PAIR)
        o_ref[:, lanes] = _unstack_heads(acc_ref[pp]).astype(o_ref.dtype)


def _sb_sample(q, kvb, cache_kt, cache_vt, layer):
    width = SB_SAMPLE_PAIRS * SB_PAIR
    steps = D_MODEL // width
    cache_spec = pl.BlockSpec((None, None, width, PAST_LEN), lambda b, p: (layer, b, p, 0))
    return pl.pallas_call(
        _sb_sample_kernel,
        name="sb_sample",
        grid=(DEC_BATCH, steps),
        in_specs=[pl.BlockSpec((DEC_SEQ, width), lambda b, p: (b, p)),
                  pl.BlockSpec((DEC_SEQ, width), lambda b, p: (b, p)),
                  pl.BlockSpec((DEC_SEQ, width), lambda b, p: (b, steps + p)),
                  cache_spec, cache_spec],
        out_specs=pl.BlockSpec((DEC_SEQ, width), lambda b, p: (b, p)),
        out_shape=jax.ShapeDtypeStruct((N_SAMPLE, D_MODEL), BF16),
        scratch_shapes=[pltpu.VMEM((SB_SAMPLE_PAIRS, 2 * DEC_SEQ, 1), F32),
                        pltpu.VMEM((SB_SAMPLE_PAIRS, 2 * DEC_SEQ, SB_PAIR), F32)],
        compiler_params=_params("parallel", "parallel"),
    )(q, kvb, kvb, cache_kt, cache_vt)


def _gla_kernel(q_ref, k_ref, v_ref, go_ref, gl_ref, s0_ref, wgk_ref, bgk_ref, nw_ref,
                o_ref, sout_ref, st_ref, *, nchunks):
    if nchunks > 1:
        @pl.when(pl.program_id(1) == 0)
        def _():
            for sq in range(GLA_SEQS):
                for h in range(GLA_HEADS):
                    st_ref[sq, h] = s0_ref[sq, h].T

    states = [_gla_seq(q_ref.at[sq], k_ref.at[sq], v_ref.at[sq], go_ref.at[sq], gl_ref.at[sq],
                       s0_ref.at[sq], wgk_ref, bgk_ref, nw_ref, o_ref.at[sq], st_ref.at[sq],
                       nchunks)
              for sq in range(GLA_SEQS)]

    def write_states():
        for sq in range(GLA_SEQS):
            for h in range(GLA_HEADS):
                sout_ref[sq, h] = states[sq][h].T

    if nchunks > 1:
        pl.when(pl.program_id(1) == nchunks - 1)(write_states)
    else:
        write_states()


def _gla_seq(q_ref, k_ref, v_ref, go_ref, gl_ref, s0_ref, wgk_ref, bgk_ref, nw_ref,
             o_ref, st_ref, nchunks):
    nt = (((1,), (1,)), ((), ()))
    q = q_ref[...] * (GLA_DK ** -0.5)
    k = k_ref[...]
    gate_in = jnp.dot(gl_ref[...].astype(BF16), wgk_ref[...].astype(BF16),
                      preferred_element_type=F32) + bgk_ref[...]
    g = (jnp.minimum(gate_in, 0.0) - jnp.log1p(jnp.exp(-jnp.abs(gate_in)))) / GLA_GATE_NORMALIZER

    t_i = lax.broadcasted_iota(jnp.int32, (CHUNK, CHUNK), 0)
    s_i = lax.broadcasted_iota(jnp.int32, (CHUNK, CHUNK), 1)
    lower = (s_i <= t_i).astype(BF16)
    b = jnp.zeros((CHUNK, GLA_KEY_DIM), F32)
    for part in _split_bf16(g, 3):
        b = b + jnp.dot(lower, part, preferred_element_type=F32)

    qe = (q * jnp.exp(b)).astype(BF16)
    b_last = b[CHUNK - 1:CHUNK]
    kd = (k * jnp.exp(b_last - b)).astype(BF16)
    decay_last = jnp.exp(b_last)

    key_row = lax.broadcasted_iota(jnp.int32, (CHUNK, 1), 0)
    sub_row = lax.broadcasted_iota(jnp.int32, (GLA_SUB, 1), 0)
    key_lane = lax.broadcasted_iota(jnp.int32, (1, CHUNK), 1)
    n_sub = CHUNK // GLA_SUB
    qt, kt = [None] * n_sub, [None] * n_sub
    for blk in range(1, n_sub):
        lo = blk * GLA_SUB
        ref = b[lo - 1:lo]
        qt[blk] = (q[lo:lo + GLA_SUB] * jnp.exp(b[lo:lo + GLA_SUB] - ref)).astype(BF16)
        kt[blk] = (k * jnp.exp(jnp.where(key_row < lo, ref - b, -jnp.inf))).astype(BF16)

    feat_head = lax.broadcasted_iota(jnp.int32, (GLA_KEY_DIM, GLA_HEADS * CHUNK), 0) // GLA_DK
    out_head = lax.broadcasted_iota(jnp.int32, (GLA_KEY_DIM, GLA_HEADS * CHUNK), 1) // CHUNK
    head_sum = (feat_head == out_head).astype(BF16)
    out_key = lax.broadcasted_iota(jnp.int32, (1, GLA_HEADS * CHUNK), 1) % CHUNK
    diag_rows = []
    for blk in range(n_sub):
        rows = slice(blk * GLA_SUB, (blk + 1) * GLA_SUB)
        qi, ki, bi = q[rows], k[rows], b[rows]
        prods = []
        for s in range(GLA_SUB):
            diff = jnp.where(sub_row >= s, bi - bi[s:s + 1], -jnp.inf)
            prods.append((qi * (ki[s:s + 1] * jnp.exp(diff))).astype(BF16))
        sums = jnp.dot(jnp.concatenate(prods, axis=0), head_sum, preferred_element_type=F32)
        a_blk = jnp.zeros((GLA_SUB, GLA_HEADS * CHUNK), F32)
        for s in range(GLA_SUB):
            a_blk = jnp.where(out_key == blk * GLA_SUB + s,
                              sums[s * GLA_SUB:(s + 1) * GLA_SUB], a_blk)
        diag_rows.append(a_blk)
    a_diag = jnp.concatenate(diag_rows, axis=0)

    nw = nw_ref[...]
    new_states = []
    for h in range(GLA_HEADS):
        kl = slice(h * GLA_DK, (h + 1) * GLA_DK)
        vl = slice(h * GLA_DV, (h + 1) * GLA_DV)
        v = v_ref[:, vl].astype(BF16)
        st = st_ref[h] if nchunks > 1 else s0_ref[h].T
        a_rows = [jnp.zeros((GLA_SUB, CHUNK), F32)]
        for blk in range(1, n_sub):
            a_rows.append(lax.dot_general(qt[blk][:, kl], kt[blk][:, kl], nt,
                                          preferred_element_type=F32))
        a = (jnp.concatenate(a_rows, axis=0) + a_diag[:, h * CHUNK:(h + 1) * CHUNK]).astype(BF16)
        o = (lax.dot_general(qe[:, kl], st.astype(BF16), nt, preferred_element_type=F32)
             + jnp.dot(a, v, preferred_element_type=F32))
        st_new = st * decay_last[:, kl] + lax.dot_general(
            v, kd[:, kl], (((0,), (0,)), ((), ())), preferred_element_type=F32)
        if nchunks > 1:
            st_ref[h] = st_new
        new_states.append(st_new)

        o = o * lax.rsqrt(jnp.mean(o * o, axis=-1, keepdims=True) + NORM_EPS) * nw
        go = go_ref[:, vl]
        o_ref[:, vl] = (o * (go * jax.nn.sigmoid(go))).astype(o_ref.dtype)
    return new_states


def _gla(proj, state, w_gk2, b_gk, norm_w, nseq, nchunks):
    def rows(width, col):
        return pl.BlockSpec((GLA_SEQS, CHUNK, width), lambda b, c: (b, c, col))

    state_spec = pl.BlockSpec((GLA_SEQS, GLA_HEADS, GLA_DK, GLA_DV), lambda b, c: (b, 0, 0, 0))
    w_gk2 = jnp.pad(w_gk2, ((0, LANES - GLA_GATE_RANK), (0, 0)))
    b_gk = b_gk.reshape(1, GLA_KEY_DIM)
    norm_w = norm_w.reshape(1, GLA_DV)
    proj = proj.reshape(nseq, nchunks * CHUNK, GLA_IN_PAD)
    o, s_out = pl.pallas_call(
        functools.partial(_gla_kernel, nchunks=nchunks),
        name="gla",
        grid=(nseq // GLA_SEQS, nchunks),
        in_specs=[
            rows(GLA_KEY_DIM, 0), rows(GLA_KEY_DIM, 1), rows(GLA_VALUE_DIM, 1),
            rows(GLA_VALUE_DIM, 2), rows(LANES, GLA_IN_PAD // LANES - 1),
            state_spec,
            _resident(w_gk2), _resident(b_gk), _resident(norm_w),
        ],
        out_specs=[rows(GLA_VALUE_DIM, 0), state_spec],
        out_shape=[
            jax.ShapeDtypeStruct((nseq, nchunks * CHUNK, GLA_VALUE_DIM), BF16),
            jax.ShapeDtypeStruct((nseq, GLA_HEADS, GLA_DK, GLA_DV), F32),
        ],
        scratch_shapes=[pltpu.VMEM((GLA_SEQS, GLA_HEADS, GLA_DV, GLA_DK), F32)],
        compiler_params=_params("parallel", "arbitrary"),
    )(proj, proj, proj, proj, proj, state, w_gk2, b_gk, norm_w)
    return o.reshape(nseq * nchunks * CHUNK, GLA_VALUE_DIM), s_out


def kernel(x_prompt, x_sample, cache_sb_k, cache_sb_v, state_conv, state_gla, norm_mix, norm_mlp,
           sb_w_qkv, sb_w_o, conv_w_in, conv_w, conv_w_out, gla_w_in, gla_w_gk2, gla_b_gk,
           gla_norm, gla_w_o, mlp_w_up, mlp_w_down, norm_final):
    xp = x_prompt.reshape(N_PROMPT, D_MODEL)
    xs = x_sample.reshape(N_SAMPLE, D_MODEL)
    n_sb = cache_sb_k.shape[0]
    cache_kt = cache_sb_k.transpose(0, 1, 3, 4, 2).reshape(n_sb, DEC_BATCH, D_MODEL, PAST_LEN)
    cache_vt = cache_sb_v.transpose(0, 1, 3, 4, 2).reshape(n_sb, DEC_BATCH, D_MODEL, PAST_LEN)
    sb_kv_p, sb_kv_s = None, None
    conv_p, conv_s, gla_p, gla_s = [], [], [], []

    def vec(a):
        return a.reshape(1, -1)

    for i in range(DEPTH):
        j = i // N_MIXERS
        final = i == DEPTH - 1
        mlp_consts = [vec(norm_mlp[i]), mlp_w_up[i].astype(BF16), mlp_w_down[i].astype(BF16),
                      vec(norm_final)]

        def proj_mlp(name, a_p, a_s, w_o):
            return _two_stream(functools.partial(_proj_mlp_body, final_norm=final), name,
                               [a_p, xp], [a_s, xs], [w_o.astype(BF16)] + mlp_consts,
                               [(D_MODEL, F32)])

        if i % N_MIXERS == 0:
            w = sb_w_qkv[j]
            wq = (w[:, :D_MODEL] * (SB_HEAD_DIM ** -0.5)).astype(BF16)
            wkv = w[:, D_MODEL:].astype(BF16)
            q_p, kt, vt, ktb, vtb = _sb_qkv_prompt(xp, norm_mix[i], wq, wkv.T, j, n_sb, sb_kv_p)
            sb_kv_p = (kt, vt)
            q_s, k_s, v_s, kvb_s = _sb_qkv_sample(xs, norm_mix[i], wq, wkv, j, n_sb, sb_kv_s)
            sb_kv_s = (k_s, v_s)
            o_p = _sb_prompt(q_p, ktb, vtb)
            o_s = _sb_sample(q_s, kvb_s, cache_kt, cache_vt, j)
            xp, xs = proj_mlp("sb_out_mlp", o_p, o_s, sb_w_o[j])
        elif i % N_MIXERS == 1:
            gb_p, u_p, gb_s, u_s = _two_stream(
                _conv_in_body, "conv_in", [xp], [xs],
                [vec(norm_mix[i]), conv_w_in[j].astype(BF16)], [(D_MODEL, F32), (D_MODEL, F32)])
            keep = CONV_WIDTH - 1
            tails_p = u_p.reshape(BATCH, SEQ // CHUNK, CHUNK, D_MODEL)[:, :, CHUNK - keep:]
            prev_p = jnp.concatenate(
                [jnp.zeros((BATCH, 1, keep, D_MODEL), F32), tails_p[:, :-1]], axis=1)
            prev_p = prev_p.reshape(N_PROMPT // CHUNK, keep, D_MODEL)
            xp, xs = _two_stream(_conv_out_body, "conv_out",
                                 [u_p, prev_p, gb_p, xp], [u_s, state_conv[j], gb_s, xs],
                                 [conv_w[j], conv_w_out[j].astype(BF16)], [(D_MODEL, F32)])
            xp, xs = _two_stream(functools.partial(_mlp_body, final_norm=final), "mlp",
                                 [xp], [xs], mlp_consts, [(D_MODEL, F32)])
            conv_p.append(tails_p[:, -1])
            conv_s.append(u_s.reshape(DEC_BATCH, DEC_SEQ, D_MODEL)[:, DEC_SEQ - keep:])
        else:
            w_in = jnp.pad(gla_w_in[j], ((0, 0), (0, GLA_IN_PAD - GLA_IN_DIM))).astype(BF16)
            proj_p, proj_s = _two_stream(_norm_proj_body, "gla_in", [xp], [xs],
                                         [vec(norm_mix[i]), w_in], [(GLA_IN_PAD, F32)])
            zero_state = jnp.zeros((BATCH, GLA_HEADS, GLA_DK, GLA_DV), F32)
            o_p, s_p = _gla(proj_p, zero_state, gla_w_gk2[j], gla_b_gk[j], gla_norm[j],
                            BATCH, SEQ // CHUNK)
            o_s, s_s = _gla(proj_s, state_gla[j], gla_w_gk2[j], gla_b_gk[j], gla_norm[j],
                            DEC_BATCH, DEC_SEQ // CHUNK)
            xp, xs = proj_mlp("gla_out_mlp", o_p, o_s, gla_w_o[j])
            gla_p.append(s_p)
            gla_s.append(s_s)

    def prompt_heads(a):
        return a.reshape(n_sb, BATCH, SB_HEADS, SB_HEAD_DIM, SEQ).transpose(0, 1, 4, 2, 3)

    def sample_heads(a):
        return a.reshape(n_sb, DEC_BATCH, DEC_SEQ, SB_HEADS, SB_HEAD_DIM)

    return (xp.reshape(BATCH, SEQ, D_MODEL), xs.reshape(DEC_BATCH, DEC_SEQ, D_MODEL),
            prompt_heads(sb_kv_p[0]), prompt_heads(sb_kv_p[1]),
            sample_heads(sb_kv_s[0]), sample_heads(sb_kv_s[1]),
            jnp.stack(conv_p), jnp.stack(conv_s), jnp.stack(gla_p), jnp.stack(gla_s))
```

```python
import functools

import jax
import jax.numpy as jnp
from jax import lax
from jax.experimental import pallas as pl
from jax.experimental.pallas import tpu as pltpu

F32 = jnp.float32
BF16 = jnp.bfloat16

D_MODEL = 1024
BATCH = 8
SEQ = 2048
DEPTH = 4
DEC_BATCH = 32
DEC_SEQ = 64
PAST_LEN = 1024
CHUNK = 64
N_MIXERS = 3
SB_HEADS = 16
SB_HEAD_DIM = D_MODEL // SB_HEADS
CONV_WIDTH = 3
GLA_HEADS = 4
GLA_KEY_DIM = D_MODEL // 2
GLA_VALUE_DIM = D_MODEL
GLA_DK = GLA_KEY_DIM // GLA_HEADS
GLA_DV = GLA_VALUE_DIM // GLA_HEADS
GLA_GATE_RANK = 16
GLA_GATE_NORMALIZER = 16.0
GLA_IN_DIM = 2 * GLA_KEY_DIM + 2 * GLA_VALUE_DIM + GLA_GATE_RANK
MLP_HIDDEN = 4 * D_MODEL
NORM_EPS = 1e-6

N_PROMPT = BATCH * SEQ
N_SAMPLE = DEC_BATCH * DEC_SEQ

LANES = 128
GLA_IN_PAD = 25 * LANES
TOKEN_TILE = 512
P_TILES = N_PROMPT // TOKEN_TILE
S_TILES = N_SAMPLE // TOKEN_TILE
COL_CHUNK = 512
SB_TILE = 256
SB_PAIR = 2 * SB_HEAD_DIM
SB_PAIRS = D_MODEL // SB_PAIR
SB_SUFFIX_PASSES = 1
SB_PROMPT_PAIRS = 4
SB_SAMPLE_PAIRS = 4
GLA_SEQS = 4
GLA_SUB = 16
VMEM_LIMIT = 56 * 1024 * 1024
LOG2E = 1.4426950408889634
SB_DEAD_LOG2 = 152.0


def _params(*sem):
    return pltpu.CompilerParams(dimension_semantics=sem, vmem_limit_bytes=VMEM_LIMIT)


def _rms(x, gain):
    ms = jnp.mean(x * x, axis=-1, keepdims=True)
    return x * lax.rsqrt(ms + NORM_EPS) * gain


def _split_bf16(x, passes):
    parts = []
    r = x
    for _ in range(passes):
        h = r.astype(BF16)
        parts.append(h)
        r = r - h.astype(F32)
    return parts


def _resident(arr):
    nd = arr.ndim
    return pl.BlockSpec(arr.shape, lambda *_: (0,) * nd, pipeline_mode=pl.Buffered(1))


def _two_stream(body, name, ins_p, ins_s, consts, outs, tile=TOKEN_TILE):
    n_in, n_c, n_out = len(ins_p), len(consts), len(outs)
    p_tiles, s_tiles = N_PROMPT // tile, N_SAMPLE // tile

    def kern(*refs):
        p_in, s_in = refs[:n_in], refs[n_in:2 * n_in]
        c = refs[2 * n_in:2 * n_in + n_c]
        p_out = refs[2 * n_in + n_c:2 * n_in + n_c + n_out]
        s_out = refs[2 * n_in + n_c + n_out:]
        i = pl.program_id(0)

        @pl.when(i < p_tiles)
        def _():
            body(p_in, c, p_out)

        @pl.when(i >= p_tiles)
        def _():
            body(s_in, c, s_out)

    def p_idx(i):
        return jnp.minimum(i, p_tiles - 1)

    def s_idx(i):
        return jnp.maximum(i - p_tiles, 0)

    def spec(arr, tiles, idx):
        block = (arr.shape[0] // tiles,) + arr.shape[1:]
        zeros = (0,) * (arr.ndim - 1)
        return pl.BlockSpec(block, lambda i: (idx(i),) + zeros)

    def out_spec(width, idx):
        return pl.BlockSpec((tile, width), lambda i: (idx(i), 0))

    return pl.pallas_call(
        kern,
        name=name,
        grid=(p_tiles + s_tiles,),
        in_specs=([spec(a, p_tiles, p_idx) for a in ins_p] + [spec(a, s_tiles, s_idx) for a in ins_s]
                  + [_resident(a) for a in consts]),
        out_specs=([out_spec(w, p_idx) for w, _ in outs] + [out_spec(w, s_idx) for w, _ in outs]),
        out_shape=([jax.ShapeDtypeStruct((N_PROMPT, w), dt) for w, dt in outs]
                   + [jax.ShapeDtypeStruct((N_SAMPLE, w), dt) for w, dt in outs]),
        compiler_params=_params("arbitrary"),
    )(*ins_p, *ins_s, *consts)


def _norm_proj_body(ins, consts, outs):
    (x_ref,), (g_ref, w_ref), (o_ref,) = ins, consts, outs
    xn = _rms(x_ref[...], g_ref[...]).astype(BF16)
    dout = o_ref.shape[1]
    for lo in range(0, dout, COL_CHUNK):
        cols = slice(lo, min(lo + COL_CHUNK, dout))
        o_ref[:, cols] = jnp.dot(xn, w_ref[:, cols], preferred_element_type=F32)


def _conv_in_body(ins, consts, outs):
    (x_ref,), (g_ref, w_ref), (gb_ref, u_ref) = ins, consts, outs
    xn = _rms(x_ref[...], g_ref[...]).astype(BF16)
    for lo in range(0, D_MODEL, COL_CHUNK):
        cols = slice(lo, lo + COL_CHUNK)
        cols_c = slice(D_MODEL + lo, D_MODEL + lo + COL_CHUNK)
        cols_x = slice(2 * D_MODEL + lo, 2 * D_MODEL + lo + COL_CHUNK)
        gb_ref[:, cols] = jnp.dot(xn, w_ref[:, cols], preferred_element_type=F32)
        gc = jnp.dot(xn, w_ref[:, cols_c], preferred_element_type=F32)
        xp = jnp.dot(xn, w_ref[:, cols_x], preferred_element_type=F32)
        u_ref[:, cols] = gc * xp


def _conv_out_body(ins, consts, outs):
    (u_ref, prev_ref, gb_ref, r_ref), (cw_ref, w_ref), (o_ref,) = ins, consts, outs
    rows = u_ref.shape[0]
    units = rows // CHUNK
    u = u_ref[...]
    prev = prev_ref[...]
    p0 = jnp.broadcast_to(prev[:, 0:1, :], (units, CHUNK, D_MODEL)).reshape(rows, D_MODEL)
    p1 = jnp.broadcast_to(prev[:, 1:2, :], (units, CHUNK, D_MODEL)).reshape(rows, D_MODEL)
    t = lax.broadcasted_iota(jnp.int32, (rows, 1), 0) % CHUNK
    s1 = jnp.where(t == 0, p1, pltpu.roll(u, 1, axis=0))
    s2 = jnp.where(t == 0, p0, jnp.where(t == 1, p1, pltpu.roll(u, 2, axis=0)))
    cw = cw_ref[...]
    y = cw[0:1, :] * s2 + cw[1:2, :] * s1 + cw[2:3, :] * u
    a = (gb_ref[...] * y).astype(BF16)
    o_ref[...] = r_ref[...] + jnp.dot(a, w_ref[...], preferred_element_type=F32)


def _mlp_apply(x, g_ref, wu_ref, wd_ref, gf_ref, final_norm):
    xn = _rms(x, g_ref[...]).astype(BF16)
    acc = x
    for lo in range(0, MLP_HIDDEN, COL_CHUNK):
        cols = slice(lo, lo + COL_CHUNK)
        h = jnp.maximum(jnp.dot(xn, wu_ref[:, cols], preferred_element_type=F32), 0.0)
        acc = acc + jnp.dot((h * h).astype(BF16), wd_ref[cols, :], preferred_element_type=F32)
    return _rms(acc, gf_ref[...]) if final_norm else acc


def _mlp_body(ins, consts, outs, *, final_norm):
    (x_ref,), (g_ref, wu_ref, wd_ref, gf_ref), (o_ref,) = ins, consts, outs
    o_ref[...] = _mlp_apply(x_ref[...], g_ref, wu_ref, wd_ref, gf_ref, final_norm)


def _proj_mlp_body(ins, consts, outs, *, final_norm):
    (a_ref, r_ref), (w_ref, g_ref, wu_ref, wd_ref, gf_ref), (o_ref,) = ins, consts, outs
    x = r_ref[...] + jnp.dot(a_ref[...], w_ref[...], preferred_element_type=F32)
    o_ref[...] = _mlp_apply(x, g_ref, wu_ref, wd_ref, gf_ref, final_norm)


def _sb_qkv_prompt_kernel(*refs, layer, n_layers, first):
    x_ref, g_ref, wq_ref, wkvt_ref = refs[:4]
    q_ref, kt_ref, vt_ref, ktb_ref, vtb_ref = refs[-5:]
    xn = _rms(x_ref[...], g_ref[...]).astype(BF16)
    for lo in range(0, D_MODEL, COL_CHUNK):
        cols = slice(lo, lo + COL_CHUNK)
        q_ref[:, cols] = jnp.dot(xn, wq_ref[:, cols], preferred_element_type=F32).astype(BF16)
    for out_ref, outb_ref, base in ((kt_ref, ktb_ref, 0), (vt_ref, vtb_ref, D_MODEL)):
        for lo in range(0, D_MODEL, COL_CHUNK):
            rows = slice(lo, lo + COL_CHUNK)
            w_rows = slice(base + lo, base + lo + COL_CHUNK)
            t = lax.dot_general(wkvt_ref[w_rows, :], xn, (((1,), (1,)), ((), ())),
                                preferred_element_type=F32)
            if first:
                for other in range(n_layers):
                    out_ref[other, rows, :] = t if other == layer else jnp.zeros_like(t)
            else:
                out_ref[rows, :] = t
            for kt in range(TOKEN_TILE // SB_TILE):
                outb_ref[kt, rows, :] = t[:, kt * SB_TILE:(kt + 1) * SB_TILE].astype(BF16)


def _sb_qkv_prompt(x, gain, wq, wkvt, layer, n_layers, stacked):
    per_seq = SEQ // TOKEN_TILE
    key_tiles = TOKEN_TILE // SB_TILE
    first = stacked is None
    if first:
        kt_spec = pl.BlockSpec((n_layers, None, D_MODEL, TOKEN_TILE),
                               lambda i: (0, i // per_seq, 0, i % per_seq))
    else:
        kt_spec = pl.BlockSpec((None, None, D_MODEL, TOKEN_TILE),
                               lambda i: (layer, i // per_seq, 0, i % per_seq))
    ktb_spec = pl.BlockSpec((None, key_tiles, D_MODEL, SB_TILE),
                            lambda i: (i // per_seq, i % per_seq, 0, 0))
    kt_shape = jax.ShapeDtypeStruct((n_layers, BATCH, D_MODEL, SEQ), F32)
    ktb_shape = jax.ShapeDtypeStruct((BATCH, SEQ // SB_TILE, D_MODEL, SB_TILE), BF16)
    gain = gain.reshape(1, D_MODEL)
    operands = [x, gain, wq, wkvt] + ([] if first else list(stacked))
    in_specs = [pl.BlockSpec((TOKEN_TILE, D_MODEL), lambda i: (i, 0)),
                _resident(gain), _resident(wq), _resident(wkvt)]
    if not first:
        in_specs += [pl.BlockSpec(memory_space=pl.ANY)] * 2
    return pl.pallas_call(
        functools.partial(_sb_qkv_prompt_kernel, layer=layer, n_layers=n_layers, first=first),
        name="sb_qkv_prompt",
        grid=(P_TILES,),
        in_specs=in_specs,
        out_specs=[pl.BlockSpec((TOKEN_TILE, D_MODEL), lambda i: (i, 0)),
                   kt_spec, kt_spec, ktb_spec, ktb_spec],
        out_shape=[jax.ShapeDtypeStruct((N_PROMPT, D_MODEL), BF16),
                   kt_shape, kt_shape, ktb_shape, ktb_shape],
        input_output_aliases={} if first else {4: 1, 5: 2},
        compiler_params=_params("parallel"),
    )(*operands)


def _sb_qkv_sample_kernel(*refs, layer, n_layers, first):
    x_ref, g_ref, wq_ref, wkv_ref = refs[:4]
    q_ref, k_ref, v_ref, kvb_ref = refs[-4:]
    xn = _rms(x_ref[...], g_ref[...]).astype(BF16)
    for lo in range(0, D_MODEL, COL_CHUNK):
        cols = slice(lo, lo + COL_CHUNK)
        q_ref[:, cols] = jnp.dot(xn, wq_ref[:, cols], preferred_element_type=F32).astype(BF16)
    for out_ref, base in ((k_ref, 0), (v_ref, D_MODEL)):
        for lo in range(0, D_MODEL, COL_CHUNK):
            cols = slice(lo, lo + COL_CHUNK)
            w_cols = slice(base + lo, base + lo + COL_CHUNK)
            t = jnp.dot(xn, wkv_ref[:, w_cols], preferred_element_type=F32)
            if first:
                for other in range(n_layers):
                    out_ref[other, :, cols] = t if other == layer else jnp.zeros_like(t)
            else:
                out_ref[:, cols] = t
            kvb_ref[:, w_cols] = t.astype(BF16)


def _sb_qkv_sample(x, gain, wq, wkv, layer, n_layers, stacked):
    first = stacked is None
    tok = pl.BlockSpec((TOKEN_TILE, D_MODEL), lambda i: (i, 0))
    if first:
        kv_spec = pl.BlockSpec((n_layers, TOKEN_TILE, D_MODEL), lambda i: (0, i, 0))
    else:
        kv_spec = pl.BlockSpec((None, TOKEN_TILE, D_MODEL), lambda i: (layer, i, 0))
    kv_shape = jax.ShapeDtypeStruct((n_layers, N_SAMPLE, D_MODEL), F32)
    gain = gain.reshape(1, D_MODEL)
    operands = [x, gain, wq, wkv] + ([] if first else list(stacked))
    in_specs = [tok, _resident(gain), _resident(wq), _resident(wkv)]
    if not first:
        in_specs += [pl.BlockSpec(memory_space=pl.ANY)] * 2
    return pl.pallas_call(
        functools.partial(_sb_qkv_sample_kernel, layer=layer, n_layers=n_layers, first=first),
        name="sb_qkv_sample",
        grid=(S_TILES,),
        in_specs=in_specs,
        out_specs=[tok, kv_spec, kv_spec,
                   pl.BlockSpec((TOKEN_TILE, 2 * D_MODEL), lambda i: (i, 0))],
        out_shape=[jax.ShapeDtypeStruct((N_SAMPLE, D_MODEL), BF16), kv_shape, kv_shape,
                   jax.ShapeDtypeStruct((N_SAMPLE, 2 * D_MODEL), BF16)],
        input_output_aliases={} if first else {4: 1, 5: 2},
        compiler_params=_params("parallel"),
    )(*operands)


def _sb_tiles(qsts, ks, vs, upper2, mask, carries, accs, key_minor):
    nt = (((1,), (1,)), ((), ()))
    z2s, sps, spms, parts = [], [], [], []
    for qst, k in zip(qsts, ks):
        if key_minor:
            z = jnp.dot(qst, k, preferred_element_type=F32)
        else:
            z = lax.dot_general(qst, k, nt, preferred_element_type=F32)
        z2 = z * LOG2E
        sp = jnp.maximum(z2, 0.0) + jnp.log2(1.0 + jnp.exp2(-jnp.abs(z2)))
        spm = sp if mask is None else jnp.where(mask, sp, 0.0)
        z2s.append(z2)
        sps.append(sp)
        spms.append(spm)
        parts.append(jnp.concatenate(_split_bf16(spm, SB_SUFFIX_PASSES), axis=1))
    sums = jnp.dot(jnp.concatenate(parts, axis=0), upper2, preferred_element_type=F32)
    m = qsts[0].shape[0]
    out = []
    for n, (z2, sp, spm, v, carry, acc) in enumerate(zip(z2s, sps, spms, vs, carries, accs)):
        below = sums[n * m:(n + 1) * m] + carry
        a = jnp.exp2(z2 - sp - below)
        if mask is not None:
            a = jnp.where(mask, a, 0.0)
        a = a.astype(BF16)
        if key_minor:
            acc = acc + lax.dot_general(a, v, nt, preferred_element_type=F32)
        else:
            acc = acc + jnp.dot(a, v, preferred_element_type=F32)
        out.append((below[:, 0:1] + spm[:, 0:1], acc))
    return out


def _suffix_matrix2(n):
    s = lax.broadcasted_iota(jnp.int32, (SB_SUFFIX_PASSES * n, n), 0) % n
    j = lax.broadcasted_iota(jnp.int32, (SB_SUFFIX_PASSES * n, n), 1)
    return (s > j).astype(BF16)


def _split_heads(q):
    lane = lax.broadcasted_iota(jnp.int32, (1, SB_PAIR), 1)
    zero = jnp.zeros_like(q)
    return jnp.where(lane < SB_HEAD_DIM, q, zero), jnp.where(lane >= SB_HEAD_DIM, q, zero)


def _stack_heads(q):
    return jnp.concatenate(_split_heads(q), axis=0)


def _unstack_heads(acc):
    lane = lax.broadcasted_iota(jnp.int32, (1, SB_PAIR), 1)
    m = acc.shape[0] // 2
    return jnp.where(lane < SB_HEAD_DIM, acc[:m], acc[m:])


def _causal_mask(tq, tk):
    t = lax.broadcasted_iota(jnp.int32, (2 * tq, tk), 0) % tq
    s = lax.broadcasted_iota(jnp.int32, (2 * tq, tk), 1)
    return s < t


def _sb_prompt_kernel(q_ref, kt_ref, vt_ref, o_ref):
    i = pl.program_id(2)
    pairs = range(SB_PROMPT_PAIRS)
    lanes = [slice(pp * SB_PAIR, (pp + 1) * SB_PAIR) for pp in pairs]
    qst = [_stack_heads(q_ref[:, lanes[pp]]) for pp in pairs]
    upper2 = _suffix_matrix2(SB_TILE)
    causal = _causal_mask(SB_TILE, SB_TILE)
    zero_c = jnp.zeros((2 * SB_TILE, 1), F32)
    zero_a = jnp.zeros((2 * SB_TILE, SB_PAIR), F32)

    def tiles(j, mask, state):
        return _sb_tiles(qst, [kt_ref[j, lanes[pp], :] for pp in pairs],
                         [vt_ref[j, lanes[pp], :] for pp in pairs], upper2, mask,
                         [c for c, _ in state], [a for _, a in state], True)

    def store(state):
        for pp in pairs:
            o_ref[:, lanes[pp]] = _unstack_heads(state[pp][1]).astype(o_ref.dtype)

    zero = [(zero_c, zero_a)] * SB_PROMPT_PAIRS

    @pl.when(i == 0)
    def _():
        store(tiles(0, causal, zero))

    @pl.when(i > 0)
    def _():
        state = tiles(i - 1, None, tiles(i, causal, zero))

        def live(c):
            dead = c[1]
            for pp in pairs[1:]:
                dead = jnp.minimum(dead, c[1 + 2 * pp])
            return (c[0] < i) & (jnp.min(dead) < SB_DEAD_LOG2)

        def body(c):
            j = i - 1 - c[0]
            out = tiles(j, None, [(c[1 + 2 * pp], c[2 + 2 * pp]) for pp in pairs])
            return (c[0] + 1,) + tuple(x for ca in out for x in ca)

        final = lax.while_loop(live, body,
                               (jnp.int32(1),) + tuple(x for ca in state for x in ca))
        store([(final[1 + 2 * pp], final[2 + 2 * pp]) for pp in pairs])


def _sb_prompt(q, ktb, vtb):
    nq = SEQ // SB_TILE
    width = SB_PROMPT_PAIRS * SB_PAIR
    kv_spec = pl.BlockSpec((None, nq, width, SB_TILE), lambda b, p, i: (b, 0, p, 0))
    return pl.pallas_call(
        _sb_prompt_kernel,
        name="sb_prompt",
        grid=(BATCH, D_MODEL // width, nq),
        in_specs=[pl.BlockSpec((SB_TILE, width), lambda b, p, i: (b * nq + i, p)),
                  kv_spec, kv_spec],
        out_specs=pl.BlockSpec((SB_TILE, width), lambda b, p, i: (b * nq + i, p)),
        out_shape=jax.ShapeDtypeStruct((N_PROMPT, D_MODEL), BF16),
        compiler_params=_params("parallel", "parallel", "arbitrary"),
    )(q, ktb, vtb)


def _sb_sample_kernel(q_ref, kn_ref, vn_ref, ck_ref, cv_ref, o_ref, carry_ref, acc_ref):
    upper2 = _suffix_matrix2(SB_TILE)
    n_cache = PAST_LEN // SB_TILE
    new_w = SB_PAIR
    first_w = SB_TILE + new_w

    def cache_kv(pp, j):
        feats = slice(pp * SB_PAIR, (pp + 1) * SB_PAIR)
        keys = slice(j * SB_TILE, (j + 1) * SB_TILE)
        return ck_ref[feats, keys].astype(BF16), cv_ref[feats, keys].astype(BF16)

    def key_minor(new):
        padded = jnp.concatenate([new, jnp.zeros((new_w - DEC_SEQ, SB_PAIR), new.dtype)], axis=0)
        return padded.astype(F32).T.astype(BF16)

    t = lax.broadcasted_iota(jnp.int32, (2 * DEC_SEQ, first_w), 0) % DEC_SEQ
    s = lax.broadcasted_iota(jnp.int32, (2 * DEC_SEQ, first_w), 1) - SB_TILE
    first_mask = s < t
    upper_first = _suffix_matrix2(first_w)

    def save(state):
        for pp, (carry, acc) in enumerate(state):
            carry_ref[pp], acc_ref[pp] = carry, acc

    pairs = range(SB_SAMPLE_PAIRS)
    qsts, k_first, v_first = [], [], []
    for pp in pairs:
        lanes = slice(pp * SB_PAIR, (pp + 1) * SB_PAIR)
        qsts.append(_stack_heads(q_ref[:, lanes]))
        ck, cv = cache_kv(pp, n_cache - 1)
        k_first.append(jnp.concatenate([ck, key_minor(kn_ref[:, lanes])], axis=1))
        v_first.append(jnp.concatenate([cv, key_minor(vn_ref[:, lanes])], axis=1))
    save(_sb_tiles(qsts, k_first, v_first, upper_first, first_mask,
                   [jnp.zeros((2 * DEC_SEQ, 1), F32)] * SB_SAMPLE_PAIRS,
                   [jnp.zeros((2 * DEC_SEQ, SB_PAIR), F32)] * SB_SAMPLE_PAIRS, True))

    for j in reversed(range(n_cache - 1)):
        @pl.when(jnp.min(carry_ref[...]) < SB_DEAD_LOG2)
        def _():
            kvs = [cache_kv(pp, j) for pp in pairs]
            save(_sb_tiles(qsts, [kv[0] for kv in kvs], [kv[1] for kv in kvs], upper2, None,
                           [carry_ref[pp] for pp in pairs], [acc_ref[pp] for pp in pairs], True))

    for pp in range(SB_SAMPLE_PAIRS):
        lanes = slice(pp * SB_PAIR, (pp + 1) * SB_PAIR)
        o_ref[:, lanes] = _unstack_heads(acc_ref[pp]).astype(o_ref.dtype)


def _sb_sample(q, kvb, cache_kt, cache_vt, layer):
    width = SB_SAMPLE_PAIRS * SB_PAIR
    steps = D_MODEL // width
    cache_spec = pl.BlockSpec((None, None, width, PAST_LEN), lambda b, p: (layer, b, p, 0))
    return pl.pallas_call(
        _sb_sample_kernel,
        name="sb_sample",
        grid=(DEC_BATCH, steps),
        in_specs=[pl.BlockSpec((DEC_SEQ, width), lambda b, p: (b, p)),
                  pl.BlockSpec((DEC_SEQ, width), lambda b, p: (b, p)),
                  pl.BlockSpec((DEC_SEQ, width), lambda b, p: (b, steps + p)),
                  cache_spec, cache_spec],
        out_specs=pl.BlockSpec((DEC_SEQ, width), lambda b, p: (b, p)),
        out_shape=jax.ShapeDtypeStruct((N_SAMPLE, D_MODEL), BF16),
        scratch_shapes=[pltpu.VMEM((SB_SAMPLE_PAIRS, 2 * DEC_SEQ, 1), F32),
                        pltpu.VMEM((SB_SAMPLE_PAIRS, 2 * DEC_SEQ, SB_PAIR), F32)],
        compiler_params=_params("parallel", "parallel"),
    )(q, kvb, kvb, cache_kt, cache_vt)


def _gla_kernel(q_ref, k_ref, v_ref, go_ref, gl_ref, s0_ref, wgk_ref, bgk_ref, nw_ref,
                o_ref, sout_ref, st_ref, *, nchunks):
    if nchunks > 1:
        @pl.when(pl.program_id(1) == 0)
        def _():
            for sq in range(GLA_SEQS):
                for h in range(GLA_HEADS):
                    st_ref[sq, h] = s0_ref[sq, h].T

    states = [_gla_seq(q_ref.at[sq], k_ref.at[sq], v_ref.at[sq], go_ref.at[sq], gl_ref.at[sq],
                       s0_ref.at[sq], wgk_ref, bgk_ref, nw_ref, o_ref.at[sq], st_ref.at[sq],
                       nchunks)
              for sq in range(GLA_SEQS)]

    def write_states():
        for sq in range(GLA_SEQS):
            for h in range(GLA_HEADS):
                sout_ref[sq, h] = states[sq][h].T

    if nchunks > 1:
        pl.when(pl.program_id(1) == nchunks - 1)(write_states)
    else:
        write_states()


def _gla_seq(q_ref, k_ref, v_ref, go_ref, gl_ref, s0_ref, wgk_ref, bgk_ref, nw_ref,
             o_ref, st_ref, nchunks):
    nt = (((1,), (1,)), ((), ()))
    q = q_ref[...] * (GLA_DK ** -0.5)
    k = k_ref[...]
    gate_in = jnp.dot(gl_ref[...].astype(BF16), wgk_ref[...].astype(BF16),
                      preferred_element_type=F32) + bgk_ref[...]
    g = (jnp.minimum(gate_in, 0.0) - jnp.log1p(jnp.exp(-jnp.abs(gate_in)))) / GLA_GATE_NORMALIZER

    t_i = lax.broadcasted_iota(jnp.int32, (CHUNK, CHUNK), 0)
    s_i = lax.broadcasted_iota(jnp.int32, (CHUNK, CHUNK), 1)
    lower = (s_i <= t_i).astype(BF16)
    b = jnp.zeros((CHUNK, GLA_KEY_DIM), F32)
    for part in _split_bf16(g, 3):
        b = b + jnp.dot(lower, part, preferred_element_type=F32)

    qe = (q * jnp.exp(b)).astype(BF16)
    b_last = b[CHUNK - 1:CHUNK]
    kd = (k * jnp.exp(b_last - b)).astype(BF16)
    decay_last = jnp.exp(b_last)

    key_row = lax.broadcasted_iota(jnp.int32, (CHUNK, 1), 0)
    sub_row = lax.broadcasted_iota(jnp.int32, (GLA_SUB, 1), 0)
    key_lane = lax.broadcasted_iota(jnp.int32, (1, CHUNK), 1)
    n_sub = CHUNK // GLA_SUB
    qt, kt = [None] * n_sub, [None] * n_sub
    for blk in range(1, n_sub):
        lo = blk * GLA_SUB
        ref = b[lo - 1:lo]
        qt[blk] = (q[lo:lo + GLA_SUB] * jnp.exp(b[lo:lo + GLA_SUB] - ref)).astype(BF16)
        kt[blk] = (k * jnp.exp(jnp.where(key_row < lo, ref - b, -jnp.inf))).astype(BF16)

    feat_head = lax.broadcasted_iota(jnp.int32, (GLA_KEY_DIM, GLA_HEADS * CHUNK), 0) // GLA_DK
    out_head = lax.broadcasted_iota(jnp.int32, (GLA_KEY_DIM, GLA_HEADS * CHUNK), 1) // CHUNK
    head_sum = (feat_head == out_head).astype(BF16)
    out_key = lax.broadcasted_iota(jnp.int32, (1, GLA_HEADS * CHUNK), 1) % CHUNK
    diag_rows = []
    for blk in range(n_sub):
        rows = slice(blk * GLA_SUB, (blk + 1) * GLA_SUB)
        qi, ki, bi = q[rows], k[rows], b[rows]
        prods = []
        for s in range(GLA_SUB):
            diff = jnp.where(sub_row >= s, bi - bi[s:s + 1], -jnp.inf)
            prods.append((qi * (ki[s:s + 1] * jnp.exp(diff))).astype(BF16))
        sums = jnp.dot(jnp.concatenate(prods, axis=0), head_sum, preferred_element_type=F32)
        a_blk = jnp.zeros((GLA_SUB, GLA_HEADS * CHUNK), F32)
        for s in range(GLA_SUB):
            a_blk = jnp.where(out_key == blk * GLA_SUB + s,
                              sums[s * GLA_SUB:(s + 1) * GLA_SUB], a_blk)
        diag_rows.append(a_blk)
    a_diag = jnp.concatenate(diag_rows, axis=0)

    nw = nw_ref[...]
    new_states = []
    for h in range(GLA_HEADS):
        kl = slice(h * GLA_DK, (h + 1) * GLA_DK)
        vl = slice(h * GLA_DV, (h + 1) * GLA_DV)
        v = v_ref[:, vl].astype(BF16)
        st = st_ref[h] if nchunks > 1 else s0_ref[h].T
        a_rows = [jnp.zeros((GLA_SUB, CHUNK), F32)]
        for blk in range(1, n_sub):
            a_rows.append(lax.dot_general(qt[blk][:, kl], kt[blk][:, kl], nt,
                                          preferred_element_type=F32))
        a = (jnp.concatenate(a_rows, axis=0) + a_diag[:, h * CHUNK:(h + 1) * CHUNK]).astype(BF16)
        o = (lax.dot_general(qe[:, kl], st.astype(BF16), nt, preferred_element_type=F32)
             + jnp.dot(a, v, preferred_element_type=F32))
        st_new = st * decay_last[:, kl] + lax.dot_general(
            v, kd[:, kl], (((0,), (0,)), ((), ())), preferred_element_type=F32)
        if nchunks > 1:
            st_ref[h] = st_new
        new_states.append(st_new)

        o = o * lax.rsqrt(jnp.mean(o * o, axis=-1, keepdims=True) + NORM_EPS) * nw
        go = go_ref[:, vl]
        o_ref[:, vl] = (o * (go * jax.nn.sigmoid(go))).astype(o_ref.dtype)
    return new_states


def _gla(proj, state, w_gk2, b_gk, norm_w, nseq, nchunks):
    def rows(width, col):
        return pl.BlockSpec((GLA_SEQS, CHUNK, width), lambda b, c: (b, c, col))

    state_spec = pl.BlockSpec((GLA_SEQS, GLA_HEADS, GLA_DK, GLA_DV), lambda b, c: (b, 0, 0, 0))
    w_gk2 = jnp.pad(w_gk2, ((0, LANES - GLA_GATE_RANK), (0, 0)))
    b_gk = b_gk.reshape(1, GLA_KEY_DIM)
    norm_w = norm_w.reshape(1, GLA_DV)
    proj = proj.reshape(nseq, nchunks * CHUNK, GLA_IN_PAD)
    o, s_out = pl.pallas_call(
        functools.partial(_gla_kernel, nchunks=nchunks),
        name="gla",
        grid=(nseq // GLA_SEQS, nchunks),
        in_specs=[
            rows(GLA_KEY_DIM, 0), rows(GLA_KEY_DIM, 1), rows(GLA_VALUE_DIM, 1),
            rows(GLA_VALUE_DIM, 2), rows(LANES, GLA_IN_PAD // LANES - 1),
            state_spec,
            _resident(w_gk2), _resident(b_gk), _resident(norm_w),
        ],
        out_specs=[rows(GLA_VALUE_DIM, 0), state_spec],
        out_shape=[
            jax.ShapeDtypeStruct((nseq, nchunks * CHUNK, GLA_VALUE_DIM), BF16),
            jax.ShapeDtypeStruct((nseq, GLA_HEADS, GLA_DK, GLA_DV), F32),
        ],
        scratch_shapes=[pltpu.VMEM((GLA_SEQS, GLA_HEADS, GLA_DV, GLA_DK), F32)],
        compiler_params=_params("parallel", "arbitrary"),
    )(proj, proj, proj, proj, proj, state, w_gk2, b_gk, norm_w)
    return o.reshape(nseq * nchunks * CHUNK, GLA_VALUE_DIM), s_out


def kernel(x_prompt, x_sample, cache_sb_k, cache_sb_v, state_conv, state_gla, norm_mix, norm_mlp,
           sb_w_qkv, sb_w_o, conv_w_in, conv_w, conv_w_out, gla_w_in, gla_w_gk2, gla_b_gk,
           gla_norm, gla_w_o, mlp_w_up, mlp_w_down, norm_final):
    xp = x_prompt.reshape(N_PROMPT, D_MODEL)
    xs = x_sample.reshape(N_SAMPLE, D_MODEL)
    n_sb = cache_sb_k.shape[0]
    cache_kt = cache_sb_k.transpose(0, 1, 3, 4, 2).reshape(n_sb, DEC_BATCH, D_MODEL, PAST_LEN)
    cache_vt = cache_sb_v.transpose(0, 1, 3, 4, 2).reshape(n_sb, DEC_BATCH, D_MODEL, PAST_LEN)
    sb_kv_p, sb_kv_s = None, None
    conv_p, conv_s, gla_p, gla_s = [], [], [], []

    def vec(a):
        return a.reshape(1, -1)

    for i in range(DEPTH):
        j = i // N_MIXERS
        final = i == DEPTH - 1
        mlp_consts = [vec(norm_mlp[i]), mlp_w_up[i].astype(BF16), mlp_w_down[i].astype(BF16),
                      vec(norm_final)]

        def proj_mlp(name, a_p, a_s, w_o):
            return _two_stream(functools.partial(_proj_mlp_body, final_norm=final), name,
                               [a_p, xp], [a_s, xs], [w_o.astype(BF16)] + mlp_consts,
                               [(D_MODEL, F32)])

        if i % N_MIXERS == 0:
            w = sb_w_qkv[j]
            wq = (w[:, :D_MODEL] * (SB_HEAD_DIM ** -0.5)).astype(BF16)
            wkv = w[:, D_MODEL:].astype(BF16)
            q_p, kt, vt, ktb, vtb = _sb_qkv_prompt(xp, norm_mix[i], wq, wkv.T, j, n_sb, sb_kv_p)
            sb_kv_p = (kt, vt)
            q_s, k_s, v_s, kvb_s = _sb_qkv_sample(xs, norm_mix[i], wq, wkv, j, n_sb, sb_kv_s)
            sb_kv_s = (k_s, v_s)
            o_p = _sb_prompt(q_p, ktb, vtb)
            o_s = _sb_sample(q_s, kvb_s, cache_kt, cache_vt, j)
            xp, xs = proj_mlp("sb_out_mlp", o_p, o_s, sb_w_o[j])
        elif i % N_MIXERS == 1:
            gb_p, u_p, gb_s, u_s = _two_stream(
                _conv_in_body, "conv_in", [xp], [xs],
                [vec(norm_mix[i]), conv_w_in[j].astype(BF16)], [(D_MODEL, F32), (D_MODEL, F32)])
            keep = CONV_WIDTH - 1
            tails_p = u_p.reshape(BATCH, SEQ // CHUNK, CHUNK, D_MODEL)[:, :, CHUNK - keep:]
            prev_p = jnp.concatenate(
                [jnp.zeros((BATCH, 1, keep, D_MODEL), F32), tails_p[:, :-1]], axis=1)
            prev_p = prev_p.reshape(N_PROMPT // CHUNK, keep, D_MODEL)
            xp, xs = _two_stream(_conv_out_body, "conv_out",
                                 [u_p, prev_p, gb_p, xp], [u_s, state_conv[j], gb_s, xs],
                                 [conv_w[j], conv_w_out[j].astype(BF16)], [(D_MODEL, F32)])
            xp, xs = _two_stream(functools.partial(_mlp_body, final_norm=final), "mlp",
                                 [xp], [xs], mlp_consts, [(D_MODEL, F32)])
            conv_p.append(tails_p[:, -1])
            conv_s.append(u_s.reshape(DEC_BATCH, DEC_SEQ, D_MODEL)[:, DEC_SEQ - keep:])
        else:
            w_in = jnp.pad(gla_w_in[j], ((0, 0), (0, GLA_IN_PAD - GLA_IN_DIM))).astype(BF16)
            proj_p, proj_s = _two_stream(_norm_proj_body, "gla_in", [xp], [xs],
                                         [vec(norm_mix[i]), w_in], [(GLA_IN_PAD, F32)])
            zero_state = jnp.zeros((BATCH, GLA_HEADS, GLA_DK, GLA_DV), F32)
            o_p, s_p = _gla(proj_p, zero_state, gla_w_gk2[j], gla_b_gk[j], gla_norm[j],
                            BATCH, SEQ // CHUNK)
            o_s, s_s = _gla(proj_s, state_gla[j], gla_w_gk2[j], gla_b_gk[j], gla_norm[j],
                            DEC_BATCH, DEC_SEQ // CHUNK)
            xp, xs = proj_mlp("gla_out_mlp", o_p, o_s, gla_w_o[j])
            gla_p.append(s_p)
            gla_s.append(s_s)

    def prompt_heads(a):
        return a.reshape(n_sb, BATCH, SB_HEADS, SB_HEAD_DIM, SEQ).transpose(0, 1, 4, 2, 3)

    def sample_heads(a):
        return a.reshape(n_sb, DEC_BATCH, DEC_SEQ, SB_HEADS, SB_HEAD_DIM)

    return (xp.reshape(BATCH, SEQ, D_MODEL), xs.reshape(DEC_BATCH, DEC_SEQ, D_MODEL),
            prompt_heads(sb_kv_p[0]), prompt_heads(sb_kv_p[1]),
            sample_heads(sb_kv_s[0]), sample_heads(sb_kv_s[1]),
            jnp.stack(conv_p), jnp.stack(conv_s), jnp.stack(gla_p), jnp.stack(gla_s))
```

```python
import functools

import jax
import jax.numpy as jnp
from jax import lax
from jax.experimental import pallas as pl
from jax.experimental.pallas import tpu as pltpu

F32 = jnp.float32
BF16 = jnp.bfloat16

D_MODEL = 1024
BATCH = 8
SEQ = 2048
DEPTH = 4
DEC_BATCH = 32
DEC_SEQ = 64
PAST_LEN = 1024
CHUNK = 64
N_MIXERS = 3
SB_HEADS = 16
SB_HEAD_DIM = D_MODEL // SB_HEADS
CONV_WIDTH = 3
GLA_HEADS = 4
GLA_KEY_DIM = D_MODEL // 2
GLA_VALUE_DIM = D_MODEL
GLA_DK = GLA_KEY_DIM // GLA_HEADS
GLA_DV = GLA_VALUE_DIM // GLA_HEADS
GLA_GATE_RANK = 16
GLA_GATE_NORMALIZER = 16.0
GLA_IN_DIM = 2 * GLA_KEY_DIM + 2 * GLA_VALUE_DIM + GLA_GATE_RANK
MLP_HIDDEN = 4 * D_MODEL
NORM_EPS = 1e-6

N_PROMPT = BATCH * SEQ
N_SAMPLE = DEC_BATCH * DEC_SEQ

LANES = 128
GLA_IN_PAD = 25 * LANES
TOKEN_TILE = 512
P_TILES = N_PROMPT // TOKEN_TILE
S_TILES = N_SAMPLE // TOKEN_TILE
COL_CHUNK = 512
SB_TILE = 256
SB_PAIR = 2 * SB_HEAD_DIM
SB_PAIRS = D_MODEL // SB_PAIR
SB_SUFFIX_PASSES = 1
SB_PROMPT_PAIRS = 4
SB_SAMPLE_PAIRS = 8
GLA_SEQS = 8
GLA_SUB = 16
VMEM_LIMIT = 56 * 1024 * 1024
LOG2E = 1.4426950408889634
SB_DEAD_LOG2 = 152.0


def _params(*sem):
    return pltpu.CompilerParams(dimension_semantics=sem, vmem_limit_bytes=VMEM_LIMIT)


def _rms(x, gain):
    ms = jnp.mean(x * x, axis=-1, keepdims=True)
    return x * lax.rsqrt(ms + NORM_EPS) * gain


def _split_bf16(x, passes):
    parts = []
    r = x
    for _ in range(passes):
        h = r.astype(BF16)
        parts.append(h)
        r = r - h.astype(F32)
    return parts


def _resident(arr):
    nd = arr.ndim
    return pl.BlockSpec(arr.shape, lambda *_: (0,) * nd, pipeline_mode=pl.Buffered(1))


def _two_stream(body, name, ins_p, ins_s, consts, outs, tile=TOKEN_TILE):
    n_in, n_c, n_out = len(ins_p), len(consts), len(outs)
    p_tiles, s_tiles = N_PROMPT // tile, N_SAMPLE // tile

    def kern(*refs):
        p_in, s_in = refs[:n_in], refs[n_in:2 * n_in]
        c = refs[2 * n_in:2 * n_in + n_c]
        p_out = refs[2 * n_in + n_c:2 * n_in + n_c + n_out]
        s_out = refs[2 * n_in + n_c + n_out:]
        i = pl.program_id(0)

        @pl.when(i < p_tiles)
        def _():
            body(p_in, c, p_out)

        @pl.when(i >= p_tiles)
        def _():
            body(s_in, c, s_out)

    def p_idx(i):
        return jnp.minimum(i, p_tiles - 1)

    def s_idx(i):
        return jnp.maximum(i - p_tiles, 0)

    def spec(arr, tiles, idx):
        block = (arr.shape[0] // tiles,) + arr.shape[1:]
        zeros = (0,) * (arr.ndim - 1)
        return pl.BlockSpec(block, lambda i: (idx(i),) + zeros)

    def out_spec(width, idx):
        return pl.BlockSpec((tile, width), lambda i: (idx(i), 0))

    return pl.pallas_call(
        kern,
        name=name,
        grid=(p_tiles + s_tiles,),
        in_specs=([spec(a, p_tiles, p_idx) for a in ins_p] + [spec(a, s_tiles, s_idx) for a in ins_s]
                  + [_resident(a) for a in consts]),
        out_specs=([out_spec(w, p_idx) for w, _ in outs] + [out_spec(w, s_idx) for w, _ in outs]),
        out_shape=([jax.ShapeDtypeStruct((N_PROMPT, w), dt) for w, dt in outs]
                   + [jax.ShapeDtypeStruct((N_SAMPLE, w), dt) for w, dt in outs]),
        compiler_params=_params("arbitrary"),
    )(*ins_p, *ins_s, *consts)


def _norm_proj_body(ins, consts, outs):
    (x_ref,), (g_ref, w_ref), (o_ref,) = ins, consts, outs
    xn = _rms(x_ref[...], g_ref[...]).astype(BF16)
    dout = o_ref.shape[1]
    for lo in range(0, dout, COL_CHUNK):
        cols = slice(lo, min(lo + COL_CHUNK, dout))
        o_ref[:, cols] = jnp.dot(xn, w_ref[:, cols], preferred_element_type=F32)


def _conv_in_body(ins, consts, outs):
    (x_ref,), (g_ref, w_ref), (gb_ref, u_ref) = ins, consts, outs
    xn = _rms(x_ref[...], g_ref[...]).astype(BF16)
    for lo in range(0, D_MODEL, COL_CHUNK):
        cols = slice(lo, lo + COL_CHUNK)
        cols_c = slice(D_MODEL + lo, D_MODEL + lo + COL_CHUNK)
        cols_x = slice(2 * D_MODEL + lo, 2 * D_MODEL + lo + COL_CHUNK)
        gb_ref[:, cols] = jnp.dot(xn, w_ref[:, cols], preferred_element_type=F32)
        gc = jnp.dot(xn, w_ref[:, cols_c], preferred_element_type=F32)
        xp = jnp.dot(xn, w_ref[:, cols_x], preferred_element_type=F32)
        u_ref[:, cols] = gc * xp


def _conv_out_body(ins, consts, outs):
    (u_ref, prev_ref, gb_ref, r_ref), (cw_ref, w_ref), (o_ref,) = ins, consts, outs
    rows = u_ref.shape[0]
    units = rows // CHUNK
    u = u_ref[...]
    prev = prev_ref[...]
    p0 = jnp.broadcast_to(prev[:, 0:1, :], (units, CHUNK, D_MODEL)).reshape(rows, D_MODEL)
    p1 = jnp.broadcast_to(prev[:, 1:2, :], (units, CHUNK, D_MODEL)).reshape(rows, D_MODEL)
    t = lax.broadcasted_iota(jnp.int32, (rows, 1), 0) % CHUNK
    s1 = jnp.where(t == 0, p1, pltpu.roll(u, 1, axis=0))
    s2 = jnp.where(t == 0, p0, jnp.where(t == 1, p1, pltpu.roll(u, 2, axis=0)))
    cw = cw_ref[...]
    y = cw[0:1, :] * s2 + cw[1:2, :] * s1 + cw[2:3, :] * u
    a = (gb_ref[...] * y).astype(BF16)
    o_ref[...] = r_ref[...] + jnp.dot(a, w_ref[...], preferred_element_type=F32)


def _mlp_apply(x, g_ref, wu_ref, wd_ref, gf_ref, final_norm):
    xn = _rms(x, g_ref[...]).astype(BF16)
    acc = x
    for lo in range(0, MLP_HIDDEN, COL_CHUNK):
        cols = slice(lo, lo + COL_CHUNK)
        h = jnp.maximum(jnp.dot(xn, wu_ref[:, cols], preferred_element_type=F32), 0.0)
        acc = acc + jnp.dot((h * h).astype(BF16), wd_ref[cols, :], preferred_element_type=F32)
    return _rms(acc, gf_ref[...]) if final_norm else acc


def _mlp_body(ins, consts, outs, *, final_norm):
    (x_ref,), (g_ref, wu_ref, wd_ref, gf_ref), (o_ref,) = ins, consts, outs
    o_ref[...] = _mlp_apply(x_ref[...], g_ref, wu_ref, wd_ref, gf_ref, final_norm)


def _proj_mlp_body(ins, consts, outs, *, final_norm):
    (a_ref, r_ref), (w_ref, g_ref, wu_ref, wd_ref, gf_ref), (o_ref,) = ins, consts, outs
    x = r_ref[...] + jnp.dot(a_ref[...], w_ref[...], preferred_element_type=F32)
    o_ref[...] = _mlp_apply(x, g_ref, wu_ref, wd_ref, gf_ref, final_norm)


def _sb_qkv_prompt_kernel(*refs, layer, n_layers, first):
    x_ref, g_ref, wq_ref, wkvt_ref = refs[:4]
    q_ref, kt_ref, vt_ref, ktb_ref, vtb_ref = refs[-5:]
    xn = _rms(x_ref[...], g_ref[...]).astype(BF16)
    for lo in range(0, D_MODEL, COL_CHUNK):
        cols = slice(lo, lo + COL_CHUNK)
        q_ref[:, cols] = jnp.dot(xn, wq_ref[:, cols], preferred_element_type=F32).astype(BF16)
    for out_ref, outb_ref, base in ((kt_ref, ktb_ref, 0), (vt_ref, vtb_ref, D_MODEL)):
        for lo in range(0, D_MODEL, COL_CHUNK):
            rows = slice(lo, lo + COL_CHUNK)
            w_rows = slice(base + lo, base + lo + COL_CHUNK)
            t = lax.dot_general(wkvt_ref[w_rows, :], xn, (((1,), (1,)), ((), ())),
                                preferred_element_type=F32)
            if first:
                for other in range(n_layers):
                    out_ref[other, rows, :] = t if other == layer else jnp.zeros_like(t)
            else:
                out_ref[rows, :] = t
            for kt in range(TOKEN_TILE // SB_TILE):
                outb_ref[kt, rows, :] = t[:, kt * SB_TILE:(kt + 1) * SB_TILE].astype(BF16)


def _sb_qkv_prompt(x, gain, wq, wkvt, layer, n_layers, stacked):
    per_seq = SEQ // TOKEN_TILE
    key_tiles = TOKEN_TILE // SB_TILE
    first = stacked is None
    if first:
        kt_spec = pl.BlockSpec((n_layers, None, D_MODEL, TOKEN_TILE),
                               lambda i: (0, i // per_seq, 0, i % per_seq))
    else:
        kt_spec = pl.BlockSpec((None, None, D_MODEL, TOKEN_TILE),
                               lambda i: (layer, i // per_seq, 0, i % per_seq))
    ktb_spec = pl.BlockSpec((None, key_tiles, D_MODEL, SB_TILE),
                            lambda i: (i // per_seq, i % per_seq, 0, 0))
    kt_shape = jax.ShapeDtypeStruct((n_layers, BATCH, D_MODEL, SEQ), F32)
    ktb_shape = jax.ShapeDtypeStruct((BATCH, SEQ // SB_TILE, D_MODEL, SB_TILE), BF16)
    gain = gain.reshape(1, D_MODEL)
    operands = [x, gain, wq, wkvt] + ([] if first else list(stacked))
    in_specs = [pl.BlockSpec((TOKEN_TILE, D_MODEL), lambda i: (i, 0)),
                _resident(gain), _resident(wq), _resident(wkvt)]
    if not first:
        in_specs += [pl.BlockSpec(memory_space=pl.ANY)] * 2
    return pl.pallas_call(
        functools.partial(_sb_qkv_prompt_kernel, layer=layer, n_layers=n_layers, first=first),
        name="sb_qkv_prompt",
        grid=(P_TILES,),
        in_specs=in_specs,
        out_specs=[pl.BlockSpec((TOKEN_TILE, D_MODEL), lambda i: (i, 0)),
                   kt_spec, kt_spec, ktb_spec, ktb_spec],
        out_shape=[jax.ShapeDtypeStruct((N_PROMPT, D_MODEL), BF16),
                   kt_shape, kt_shape, ktb_shape, ktb_shape],
        input_output_aliases={} if first else {4: 1, 5: 2},
        compiler_params=_params("parallel"),
    )(*operands)


def _sb_qkv_sample_kernel(*refs, layer, n_layers, first):
    x_ref, g_ref, wq_ref, wkv_ref = refs[:4]
    q_ref, k_ref, v_ref, kvb_ref = refs[-4:]
    xn = _rms(x_ref[...], g_ref[...]).astype(BF16)
    for lo in range(0, D_MODEL, COL_CHUNK):
        cols = slice(lo, lo + COL_CHUNK)
        q_ref[:, cols] = jnp.dot(xn, wq_ref[:, cols], preferred_element_type=F32).astype(BF16)
    for out_ref, base in ((k_ref, 0), (v_ref, D_MODEL)):
        for lo in range(0, D_MODEL, COL_CHUNK):
            cols = slice(lo, lo + COL_CHUNK)
            w_cols = slice(base + lo, base + lo + COL_CHUNK)
            t = jnp.dot(xn, wkv_ref[:, w_cols], preferred_element_type=F32)
            if first:
                for other in range(n_layers):
                    out_ref[other, :, cols] = t if other == layer else jnp.zeros_like(t)
            else:
                out_ref[:, cols] = t
            kvb_ref[:, w_cols] = t.astype(BF16)


def _sb_qkv_sample(x, gain, wq, wkv, layer, n_layers, stacked):
    first = stacked is None
    tok = pl.BlockSpec((TOKEN_TILE, D_MODEL), lambda i: (i, 0))
    if first:
        kv_spec = pl.BlockSpec((n_layers, TOKEN_TILE, D_MODEL), lambda i: (0, i, 0))
    else:
        kv_spec = pl.BlockSpec((None, TOKEN_TILE, D_MODEL), lambda i: (layer, i, 0))
    kv_shape = jax.ShapeDtypeStruct((n_layers, N_SAMPLE, D_MODEL), F32)
    gain = gain.reshape(1, D_MODEL)
    operands = [x, gain, wq, wkv] + ([] if first else list(stacked))
    in_specs = [tok, _resident(gain), _resident(wq), _resident(wkv)]
    if not first:
        in_specs += [pl.BlockSpec(memory_space=pl.ANY)] * 2
    return pl.pallas_call(
        functools.partial(_sb_qkv_sample_kernel, layer=layer, n_layers=n_layers, first=first),
        name="sb_qkv_sample",
        grid=(S_TILES,),
        in_specs=in_specs,
        out_specs=[tok, kv_spec, kv_spec,
                   pl.BlockSpec((TOKEN_TILE, 2 * D_MODEL), lambda i: (i, 0))],
        out_shape=[jax.ShapeDtypeStruct((N_SAMPLE, D_MODEL), BF16), kv_shape, kv_shape,
                   jax.ShapeDtypeStruct((N_SAMPLE, 2 * D_MODEL), BF16)],
        input_output_aliases={} if first else {4: 1, 5: 2},
        compiler_params=_params("parallel"),
    )(*operands)


def _sb_tiles(qsts, ks, vs, upper2, mask, carries, accs, key_minor):
    nt = (((1,), (1,)), ((), ()))
    z2s, sps, spms, parts = [], [], [], []
    for qst, k in zip(qsts, ks):
        if key_minor:
            z = jnp.dot(qst, k, preferred_element_type=F32)
        else:
            z = lax.dot_general(qst, k, nt, preferred_element_type=F32)
        z2 = z * LOG2E
        sp = jnp.maximum(z2, 0.0) + jnp.log2(1.0 + jnp.exp2(-jnp.abs(z2)))
        spm = sp if mask is None else jnp.where(mask, sp, 0.0)
        z2s.append(z2)
        sps.append(sp)
        spms.append(spm)
        parts.append(jnp.concatenate(_split_bf16(spm, SB_SUFFIX_PASSES), axis=1))
    sums = jnp.dot(jnp.concatenate(parts, axis=0), upper2, preferred_element_type=F32)
    m = qsts[0].shape[0]
    out = []
    for n, (z2, sp, spm, v, carry, acc) in enumerate(zip(z2s, sps, spms, vs, carries, accs)):
        below = sums[n * m:(n + 1) * m] + carry
        a = jnp.exp2(z2 - sp - below)
        if mask is not None:
            a = jnp.where(mask, a, 0.0)
        a = a.astype(BF16)
        if key_minor:
            acc = acc + lax.dot_general(a, v, nt, preferred_element_type=F32)
        else:
            acc = acc + jnp.dot(a, v, preferred_element_type=F32)
        out.append((below[:, 0:1] + spm[:, 0:1], acc))
    return out


def _suffix_matrix2(n):
    s = lax.broadcasted_iota(jnp.int32, (SB_SUFFIX_PASSES * n, n), 0) % n
    j = lax.broadcasted_iota(jnp.int32, (SB_SUFFIX_PASSES * n, n), 1)
    return (s > j).astype(BF16)


def _split_heads(q):
    lane = lax.broadcasted_iota(jnp.int32, (1, SB_PAIR), 1)
    zero = jnp.zeros_like(q)
    return jnp.where(lane < SB_HEAD_DIM, q, zero), jnp.where(lane >= SB_HEAD_DIM, q, zero)


def _stack_heads(q):
    return jnp.concatenate(_split_heads(q), axis=0)


def _unstack_heads(acc):
    lane = lax.broadcasted_iota(jnp.int32, (1, SB_PAIR), 1)
    m = acc.shape[0] // 2
    return jnp.where(lane < SB_HEAD_DIM, acc[:m], acc[m:])


def _causal_mask(tq, tk):
    t = lax.broadcasted_iota(jnp.int32, (2 * tq, tk), 0) % tq
    s = lax.broadcasted_iota(jnp.int32, (2 * tq, tk), 1)
    return s < t


def _sb_prompt_kernel(q_ref, kt_ref, vt_ref, o_ref):
    i = pl.program_id(2)
    pairs = range(SB_PROMPT_PAIRS)
    lanes = [slice(pp * SB_PAIR, (pp + 1) * SB_PAIR) for pp in pairs]
    qst = [_stack_heads(q_ref[:, lanes[pp]]) for pp in pairs]
    upper2 = _suffix_matrix2(SB_TILE)
    causal = _causal_mask(SB_TILE, SB_TILE)
    zero_c = jnp.zeros((2 * SB_TILE, 1), F32)
    zero_a = jnp.zeros((2 * SB_TILE, SB_PAIR), F32)

    def tiles(j, mask, state):
        return _sb_tiles(qst, [kt_ref[j, lanes[pp], :] for pp in pairs],
                         [vt_ref[j, lanes[pp], :] for pp in pairs], upper2, mask,
                         [c for c, _ in state], [a for _, a in state], True)

    def store(state):
        for pp in pairs:
            o_ref[:, lanes[pp]] = _unstack_heads(state[pp][1]).astype(o_ref.dtype)

    zero = [(zero_c, zero_a)] * SB_PROMPT_PAIRS

    @pl.when(i == 0)
    def _():
        store(tiles(0, causal, zero))

    @pl.when(i > 0)
    def _():
        state = tiles(i - 1, None, tiles(i, causal, zero))

        def live(c):
            dead = c[1]
            for pp in pairs[1:]:
                dead = jnp.minimum(dead, c[1 + 2 * pp])
            return (c[0] < i) & (jnp.min(dead) < SB_DEAD_LOG2)

        def body(c):
            j = i - 1 - c[0]
            out = tiles(j, None, [(c[1 + 2 * pp], c[2 + 2 * pp]) for pp in pairs])
            return (c[0] + 1,) + tuple(x for ca in out for x in ca)

        final = lax.while_loop(live, body,
                               (jnp.int32(1),) + tuple(x for ca in state for x in ca))
        store([(final[1 + 2 * pp], final[2 + 2 * pp]) for pp in pairs])


def _sb_prompt(q, ktb, vtb):
    nq = SEQ // SB_TILE
    width = SB_PROMPT_PAIRS * SB_PAIR
    kv_spec = pl.BlockSpec((None, nq, width, SB_TILE), lambda b, p, i: (b, 0, p, 0))
    return pl.pallas_call(
        _sb_prompt_kernel,
        name="sb_prompt",
        grid=(BATCH, D_MODEL // width, nq),
        in_specs=[pl.BlockSpec((SB_TILE, width), lambda b, p, i: (b * nq + i, p)),
                  kv_spec, kv_spec],
        out_specs=pl.BlockSpec((SB_TILE, width), lambda b, p, i: (b * nq + i, p)),
        out_shape=jax.ShapeDtypeStruct((N_PROMPT, D_MODEL), BF16),
        compiler_params=_params("parallel", "parallel", "arbitrary"),
    )(q, ktb, vtb)


def _sb_sample_kernel(q_ref, kn_ref, vn_ref, ck_ref, cv_ref, o_ref, carry_ref, acc_ref):
    upper2 = _suffix_matrix2(SB_TILE)
    n_cache = PAST_LEN // SB_TILE
    new_w = SB_PAIR
    first_w = SB_TILE + new_w

    def cache_kv(pp, j):
        feats = slice(pp * SB_PAIR, (pp + 1) * SB_PAIR)
        keys = slice(j * SB_TILE, (j + 1) * SB_TILE)
        return ck_ref[feats, keys].astype(BF16), cv_ref[feats, keys].astype(BF16)

    def key_minor(new):
        padded = jnp.concatenate([new, jnp.zeros((new_w - DEC_SEQ, SB_PAIR), new.dtype)], axis=0)
        return padded.astype(F32).T.astype(BF16)

    t = lax.broadcasted_iota(jnp.int32, (2 * DEC_SEQ, first_w), 0) % DEC_SEQ
    s = lax.broadcasted_iota(jnp.int32, (2 * DEC_SEQ, first_w), 1) - SB_TILE
    first_mask = s < t
    upper_first = _suffix_matrix2(first_w)

    def save(state):
        for pp, (carry, acc) in enumerate(state):
            carry_ref[pp], acc_ref[pp] = carry, acc

    pairs = range(SB_SAMPLE_PAIRS)
    qsts, k_first, v_first = [], [], []
    for pp in pairs:
        lanes = slice(pp * SB_PAIR, (pp + 1) * SB_PAIR)
        qsts.append(_stack_heads(q_ref[:, lanes]))
        ck, cv = cache_kv(pp, n_cache - 1)
        k_first.append(jnp.concatenate([ck, key_minor(kn_ref[:, lanes])], axis=1))
        v_first.append(jnp.concatenate([cv, key_minor(vn_ref[:, lanes])], axis=1))
    save(_sb_tiles(qsts, k_first, v_first, upper_first, first_mask,
                   [jnp.zeros((2 * DEC_SEQ, 1), F32)] * SB_SAMPLE_PAIRS,
                   [jnp.zeros((2 * DEC_SEQ, SB_PAIR), F32)] * SB_SAMPLE_PAIRS, True))

    for j in reversed(range(n_cache - 1)):
        @pl.when(jnp.min(carry_ref[...]) < SB_DEAD_LOG2)
        def _():
            kvs = [cache_kv(pp, j) for pp in pairs]
            save(_sb_tiles(qsts, [kv[0] for kv in kvs], [kv[1] for kv in kvs], upper2, None,
                           [carry_ref[pp] for pp in pairs], [acc_ref[pp] for pp in pairs], True))

    for pp in range(SB_SAMPLE_PAIRS):
        lanes = slice(pp * SB_PAIR, (pp + 1) * SB_PAIR)
        o_ref[:, lanes] = _unstack_heads(acc_ref[pp]).astype(o_ref.dtype)


def _sb_sample(q, kvb, cache_kt, cache_vt, layer):
    width = SB_SAMPLE_PAIRS * SB_PAIR
    steps = D_MODEL // width
    cache_spec = pl.BlockSpec((None, None, width, PAST_LEN), lambda b, p: (layer, b, p, 0))
    return pl.pallas_call(
        _sb_sample_kernel,
        name="sb_sample",
        grid=(DEC_BATCH, steps),
        in_specs=[pl.BlockSpec((DEC_SEQ, width), lambda b, p: (b, p)),
                  pl.BlockSpec((DEC_SEQ, width), lambda b, p: (b, p)),
                  pl.BlockSpec((DEC_SEQ, width), lambda b, p: (b, steps + p)),
                  cache_spec, cache_spec],
        out_specs=pl.BlockSpec((DEC_SEQ, width), lambda b, p: (b, p)),
        out_shape=jax.ShapeDtypeStruct((N_SAMPLE, D_MODEL), BF16),
        scratch_shapes=[pltpu.VMEM((SB_SAMPLE_PAIRS, 2 * DEC_SEQ, 1), F32),
                        pltpu.VMEM((SB_SAMPLE_PAIRS, 2 * DEC_SEQ, SB_PAIR), F32)],
        compiler_params=_params("parallel", "parallel"),
    )(q, kvb, kvb, cache_kt, cache_vt)


def _gla_kernel(q_ref, k_ref, v_ref, go_ref, gl_ref, s0_ref, wgk_ref, bgk_ref, nw_ref,
                o_ref, sout_ref, st_ref, *, nchunks):
    if nchunks > 1:
        @pl.when(pl.program_id(1) == 0)
        def _():
            for sq in range(GLA_SEQS):
                for h in range(GLA_HEADS):
                    st_ref[sq, h] = s0_ref[sq, h].T

    states = [_gla_seq(q_ref.at[sq], k_ref.at[sq], v_ref.at[sq], go_ref.at[sq], gl_ref.at[sq],
                       s0_ref.at[sq], wgk_ref, bgk_ref, nw_ref, o_ref.at[sq], st_ref.at[sq],
                       nchunks)
              for sq in range(GLA_SEQS)]

    def write_states():
        for sq in range(GLA_SEQS):
            for h in range(GLA_HEADS):
                sout_ref[sq, h] = states[sq][h].T

    if nchunks > 1:
        pl.when(pl.program_id(1) == nchunks - 1)(write_states)
    else:
        write_states()


def _gla_seq(q_ref, k_ref, v_ref, go_ref, gl_ref, s0_ref, wgk_ref, bgk_ref, nw_ref,
             o_ref, st_ref, nchunks):
    nt = (((1,), (1,)), ((), ()))
    q = q_ref[...] * (GLA_DK ** -0.5)
    k = k_ref[...]
    gate_in = jnp.dot(gl_ref[...].astype(BF16), wgk_ref[...].astype(BF16),
                      preferred_element_type=F32) + bgk_ref[...]
    g = (jnp.minimum(gate_in, 0.0) - jnp.log1p(jnp.exp(-jnp.abs(gate_in)))) / GLA_GATE_NORMALIZER

    t_i = lax.broadcasted_iota(jnp.int32, (CHUNK, CHUNK), 0)
    s_i = lax.broadcasted_iota(jnp.int32, (CHUNK, CHUNK), 1)
    lower = (s_i <= t_i).astype(BF16)
    b = jnp.zeros((CHUNK, GLA_KEY_DIM), F32)
    for part in _split_bf16(g, 3):
        b = b + jnp.dot(lower, part, preferred_element_type=F32)

    qe = (q * jnp.exp(b)).astype(BF16)
    b_last = b[CHUNK - 1:CHUNK]
    kd = (k * jnp.exp(b_last - b)).astype(BF16)
    decay_last = jnp.exp(b_last)

    key_row = lax.broadcasted_iota(jnp.int32, (CHUNK, 1), 0)
    sub_row = lax.broadcasted_iota(jnp.int32, (GLA_SUB, 1), 0)
    key_lane = lax.broadcasted_iota(jnp.int32, (1, CHUNK), 1)
    n_sub = CHUNK // GLA_SUB
    qt, kt = [None] * n_sub, [None] * n_sub
    for blk in range(1, n_sub):
        lo = blk * GLA_SUB
        ref = b[lo - 1:lo]
        qt[blk] = (q[lo:lo + GLA_SUB] * jnp.exp(b[lo:lo + GLA_SUB] - ref)).astype(BF16)
        kt[blk] = (k * jnp.exp(jnp.where(key_row < lo, ref - b, -jnp.inf))).astype(BF16)

    feat_head = lax.broadcasted_iota(jnp.int32, (GLA_KEY_DIM, GLA_HEADS * CHUNK), 0) // GLA_DK
    out_head = lax.broadcasted_iota(jnp.int32, (GLA_KEY_DIM, GLA_HEADS * CHUNK), 1) // CHUNK
    head_sum = (feat_head == out_head).astype(BF16)
    out_key = lax.broadcasted_iota(jnp.int32, (1, GLA_HEADS * CHUNK), 1) % CHUNK
    diag_rows = []
    for blk in range(n_sub):
        rows = slice(blk * GLA_SUB, (blk + 1) * GLA_SUB)
        qi, ki, bi = q[rows], k[rows], b[rows]
        prods = []
        for s in range(GLA_SUB):
            diff = jnp.where(sub_row >= s, bi - bi[s:s + 1], -jnp.inf)
            prods.append((qi * (ki[s:s + 1] * jnp.exp(diff))).astype(BF16))
        sums = jnp.dot(jnp.concatenate(prods, axis=0), head_sum, preferred_element_type=F32)
        a_blk = jnp.zeros((GLA_SUB, GLA_HEADS * CHUNK), F32)
        for s in range(GLA_SUB):
            a_blk = jnp.where(out_key == blk * GLA_SUB + s,
                              sums[s * GLA_SUB:(s + 1) * GLA_SUB], a_blk)
        diag_rows.append(a_blk)
    a_diag = jnp.concatenate(diag_rows, axis=0)

    nw = nw_ref[...]
    new_states = []
    for h in range(GLA_HEADS):
        kl = slice(h * GLA_DK, (h + 1) * GLA_DK)
        vl = slice(h * GLA_DV, (h + 1) * GLA_DV)
        v = v_ref[:, vl].astype(BF16)
        st = st_ref[h] if nchunks > 1 else s0_ref[h].T
        a_rows = [jnp.zeros((GLA_SUB, CHUNK), F32)]
        for blk in range(1, n_sub):
            a_rows.append(lax.dot_general(qt[blk][:, kl], kt[blk][:, kl], nt,
                                          preferred_element_type=F32))
        a = (jnp.concatenate(a_rows, axis=0) + a_diag[:, h * CHUNK:(h + 1) * CHUNK]).astype(BF16)
        o = (lax.dot_general(qe[:, kl], st.astype(BF16), nt, preferred_element_type=F32)
             + jnp.dot(a, v, preferred_element_type=F32))
        st_new = st * decay_last[:, kl] + lax.dot_general(
            v, kd[:, kl], (((0,), (0,)), ((), ())), preferred_element_type=F32)
        if nchunks > 1:
            st_ref[h] = st_new
        new_states.append(st_new)

        o = o * lax.rsqrt(jnp.mean(o * o, axis=-1, keepdims=True) + NORM_EPS) * nw
        go = go_ref[:, vl]
        o_ref[:, vl] = (o * (go * jax.nn.sigmoid(go))).astype(o_ref.dtype)
    return new_states


def _gla(proj, state, w_gk2, b_gk, norm_w, nseq, nchunks):
    def rows(width, col):
        return pl.BlockSpec((GLA_SEQS, CHUNK, width), lambda b, c: (b, c, col))

    state_spec = pl.BlockSpec((GLA_SEQS, GLA_HEADS, GLA_DK, GLA_DV), lambda b, c: (b, 0, 0, 0))
    w_gk2 = jnp.pad(w_gk2, ((0, LANES - GLA_GATE_RANK), (0, 0)))
    b_gk = b_gk.reshape(1, GLA_KEY_DIM)
    norm_w = norm_w.reshape(1, GLA_DV)
    proj = proj.reshape(nseq, nchunks * CHUNK, GLA_IN_PAD)
    o, s_out = pl.pallas_call(
        functools.partial(_gla_kernel, nchunks=nchunks),
        name="gla",
        grid=(nseq // GLA_SEQS, nchunks),
        in_specs=[
            rows(GLA_KEY_DIM, 0), rows(GLA_KEY_DIM, 1), rows(GLA_VALUE_DIM, 1),
            rows(GLA_VALUE_DIM, 2), rows(LANES, GLA_IN_PAD // LANES - 1),
            state_spec,
            _resident(w_gk2), _resident(b_gk), _resident(norm_w),
        ],
        out_specs=[rows(GLA_VALUE_DIM, 0), state_spec],
        out_shape=[
            jax.ShapeDtypeStruct((nseq, nchunks * CHUNK, GLA_VALUE_DIM), BF16),
            jax.ShapeDtypeStruct((nseq, GLA_HEADS, GLA_DK, GLA_DV), F32),
        ],
        scratch_shapes=[pltpu.VMEM((GLA_SEQS, GLA_HEADS, GLA_DV, GLA_DK), F32)],
        compiler_params=_params("parallel", "arbitrary"),
    )(proj, proj, proj, proj, proj, state, w_gk2, b_gk, norm_w)
    return o.reshape(nseq * nchunks * CHUNK, GLA_VALUE_DIM), s_out


def kernel(x_prompt, x_sample, cache_sb_k, cache_sb_v, state_conv, state_gla, norm_mix, norm_mlp,
           sb_w_qkv, sb_w_o, conv_w_in, conv_w, conv_w_out, gla_w_in, gla_w_gk2, gla_b_gk,
           gla_norm, gla_w_o, mlp_w_up, mlp_w_down, norm_final):
    xp = x_prompt.reshape(N_PROMPT, D_MODEL)
    xs = x_sample.reshape(N_SAMPLE, D_MODEL)
    n_sb = cache_sb_k.shape[0]
    cache_kt = cache_sb_k.transpose(0, 1, 3, 4, 2).reshape(n_sb, DEC_BATCH, D_MODEL, PAST_LEN)
    cache_vt = cache_sb_v.transpose(0, 1, 3, 4, 2).reshape(n_sb, DEC_BATCH, D_MODEL, PAST_LEN)
    sb_kv_p, sb_kv_s = None, None
    conv_p, conv_s, gla_p, gla_s = [], [], [], []

    def vec(a):
        return a.reshape(1, -1)

    for i in range(DEPTH):
        j = i // N_MIXERS
        final = i == DEPTH - 1
        mlp_consts = [vec(norm_mlp[i]), mlp_w_up[i].astype(BF16), mlp_w_down[i].astype(BF16),
                      vec(norm_final)]

        def proj_mlp(name, a_p, a_s, w_o):
            return _two_stream(functools.partial(_proj_mlp_body, final_norm=final), name,
                               [a_p, xp], [a_s, xs], [w_o.astype(BF16)] + mlp_consts,
                               [(D_MODEL, F32)])

        if i % N_MIXERS == 0:
            w = sb_w_qkv[j]
            wq = (w[:, :D_MODEL] * (SB_HEAD_DIM ** -0.5)).astype(BF16)
            wkv = w[:, D_MODEL:].astype(BF16)
            q_p, kt, vt, ktb, vtb = _sb_qkv_prompt(xp, norm_mix[i], wq, wkv.T, j, n_sb, sb_kv_p)
            sb_kv_p = (kt, vt)
            q_s, k_s, v_s, kvb_s = _sb_qkv_sample(xs, norm_mix[i], wq, wkv, j, n_sb, sb_kv_s)
            sb_kv_s = (k_s, v_s)
            o_p = _sb_prompt(q_p, ktb, vtb)
            o_s = _sb_sample(q_s, kvb_s, cache_kt, cache_vt, j)
            xp, xs = proj_mlp("sb_out_mlp", o_p, o_s, sb_w_o[j])
        elif i % N_MIXERS == 1:
            gb_p, u_p, gb_s, u_s = _two_stream(
                _conv_in_body, "conv_in", [xp], [xs],
                [vec(norm_mix[i]), conv_w_in[j].astype(BF16)], [(D_MODEL, F32), (D_MODEL, F32)])
            keep = CONV_WIDTH - 1
            tails_p = u_p.reshape(BATCH, SEQ // CHUNK, CHUNK, D_MODEL)[:, :, CHUNK - keep:]
            prev_p = jnp.concatenate(
                [jnp.zeros((BATCH, 1, keep, D_MODEL), F32), tails_p[:, :-1]], axis=1)
            prev_p = prev_p.reshape(N_PROMPT // CHUNK, keep, D_MODEL)
            xp, xs = _two_stream(_conv_out_body, "conv_out",
                                 [u_p, prev_p, gb_p, xp], [u_s, state_conv[j], gb_s, xs],
                                 [conv_w[j], conv_w_out[j].astype(BF16)], [(D_MODEL, F32)])
            xp, xs = _two_stream(functools.partial(_mlp_body, final_norm=final), "mlp",
                                 [xp], [xs], mlp_consts, [(D_MODEL, F32)])
            conv_p.append(tails_p[:, -1])
            conv_s.append(u_s.reshape(DEC_BATCH, DEC_SEQ, D_MODEL)[:, DEC_SEQ - keep:])
        else:
            w_in = jnp.pad(gla_w_in[j], ((0, 0), (0, GLA_IN_PAD - GLA_IN_DIM))).astype(BF16)
            proj_p, proj_s = _two_stream(_norm_proj_body, "gla_in", [xp], [xs],
                                         [vec(norm_mix[i]), w_in], [(GLA_IN_PAD, F32)])
            zero_state = jnp.zeros((BATCH, GLA_HEADS, GLA_DK, GLA_DV), F32)
            o_p, s_p = _gla(proj_p, zero_state, gla_w_gk2[j], gla_b_gk[j], gla_norm[j],
                            BATCH, SEQ // CHUNK)
            o_s, s_s = _gla(proj_s, state_gla[j], gla_w_gk2[j], gla_b_gk[j], gla_norm[j],
                            DEC_BATCH, DEC_SEQ // CHUNK)
            xp, xs = proj_mlp("gla_out_mlp", o_p, o_s, gla_w_o[j])
            gla_p.append(s_p)
            gla_s.append(s_s)

    def prompt_heads(a):
        return a.reshape(n_sb, BATCH, SB_HEADS, SB_HEAD_DIM, SEQ).transpose(0, 1, 4, 2, 3)

    def sample_heads(a):
        return a.reshape(n_sb, DEC_BATCH, DEC_SEQ, SB_HEADS, SB_HEAD_DIM)

    return (xp.reshape(BATCH, SEQ, D_MODEL), xs.reshape(DEC_BATCH, DEC_SEQ, D_MODEL),
            prompt_heads(sb_kv_p[0]), prompt_heads(sb_kv_p[1]),
            sample_heads(sb_kv_s[0]), sample_heads(sb_kv_s[1]),
            jnp.stack(conv_p), jnp.stack(conv_s), jnp.stack(gla_p), jnp.stack(gla_s))
```

```python
import functools

import jax
import jax.numpy as jnp
from jax import lax
from jax.experimental import pallas as pl
from jax.experimental.pallas import tpu as pltpu

F32 = jnp.float32
BF16 = jnp.bfloat16

D_MODEL = 1024
BATCH = 8
SEQ = 2048
DEPTH = 4
DEC_BATCH = 32
DEC_SEQ = 64
PAST_LEN = 1024
CHUNK = 64
N_MIXERS = 3
SB_HEADS = 16
SB_HEAD_DIM = D_MODEL // SB_HEADS
CONV_WIDTH = 3
GLA_HEADS = 4
GLA_KEY_DIM = D_MODEL // 2
GLA_VALUE_DIM = D_MODEL
GLA_DK = GLA_KEY_DIM // GLA_HEADS
GLA_DV = GLA_VALUE_DIM // GLA_HEADS
GLA_GATE_RANK = 16
GLA_GATE_NORMALIZER = 16.0
GLA_IN_DIM = 2 * GLA_KEY_DIM + 2 * GLA_VALUE_DIM + GLA_GATE_RANK
MLP_HIDDEN = 4 * D_MODEL
NORM_EPS = 1e-6

N_PROMPT = BATCH * SEQ
N_SAMPLE = DEC_BATCH * DEC_SEQ

LANES = 128
GLA_IN_PAD = 25 * LANES
TOKEN_TILE = 512
P_TILES = N_PROMPT // TOKEN_TILE
S_TILES = N_SAMPLE // TOKEN_TILE
COL_CHUNK = 512
SB_TILE = 256
SB_PAIR = 2 * SB_HEAD_DIM
SB_PAIRS = D_MODEL // SB_PAIR
SB_SUFFIX_PASSES = 1
SB_PROMPT_PAIRS = 4
SB_SAMPLE_PAIRS = 8
GLA_SEQS = 8
GLA_SUB = 16
VMEM_LIMIT = 56 * 1024 * 1024
LOG2E = 1.4426950408889634
SB_DEAD_LOG2 = 152.0


def _params(*sem):
    return pltpu.CompilerParams(dimension_semantics=sem, vmem_limit_bytes=VMEM_LIMIT)


def _rms(x, gain):
    ms = jnp.mean(x * x, axis=-1, keepdims=True)
    return x * lax.rsqrt(ms + NORM_EPS) * gain


def _split_bf16(x, passes):
    parts = []
    r = x
    for _ in range(passes):
        h = r.astype(BF16)
        parts.append(h)
        r = r - h.astype(F32)
    return parts


def _resident(arr):
    nd = arr.ndim
    return pl.BlockSpec(arr.shape, lambda *_: (0,) * nd, pipeline_mode=pl.Buffered(1))


def _two_stream(body, name, ins_p, ins_s, consts, outs, tile=TOKEN_TILE):
    n_in, n_c, n_out = len(ins_p), len(consts), len(outs)
    p_tiles, s_tiles = N_PROMPT // tile, N_SAMPLE // tile

    def kern(*refs):
        p_in, s_in = refs[:n_in], refs[n_in:2 * n_in]
        c = refs[2 * n_in:2 * n_in + n_c]
        p_out = refs[2 * n_in + n_c:2 * n_in + n_c + n_out]
        s_out = refs[2 * n_in + n_c + n_out:]
        i = pl.program_id(0)

        @pl.when(i < p_tiles)
        def _():
            body(p_in, c, p_out)

        @pl.when(i >= p_tiles)
        def _():
            body(s_in, c, s_out)

    def p_idx(i):
        return jnp.minimum(i, p_tiles - 1)

    def s_idx(i):
        return jnp.maximum(i - p_tiles, 0)

    def spec(arr, tiles, idx):
        block = (arr.shape[0] // tiles,) + arr.shape[1:]
        zeros = (0,) * (arr.ndim - 1)
        return pl.BlockSpec(block, lambda i: (idx(i),) + zeros)

    def out_spec(width, idx):
        return pl.BlockSpec((tile, width), lambda i: (idx(i), 0))

    return pl.pallas_call(
        kern,
        name=name,
        grid=(p_tiles + s_tiles,),
        in_specs=([spec(a, p_tiles, p_idx) for a in ins_p] + [spec(a, s_tiles, s_idx) for a in ins_s]
                  + [_resident(a) for a in consts]),
        out_specs=([out_spec(w, p_idx) for w, _ in outs] + [out_spec(w, s_idx) for w, _ in outs]),
        out_shape=([jax.ShapeDtypeStruct((N_PROMPT, w), dt) for w, dt in outs]
                   + [jax.ShapeDtypeStruct((N_SAMPLE, w), dt) for w, dt in outs]),
        compiler_params=_params("arbitrary"),
    )(*ins_p, *ins_s, *consts)


def _norm_proj_body(ins, consts, outs):
    (x_ref,), (g_ref, w_ref), (o_ref,) = ins, consts, outs
    xn = _rms(x_ref[...], g_ref[...]).astype(BF16)
    dout = o_ref.shape[1]
    for lo in range(0, dout, COL_CHUNK):
        cols = slice(lo, min(lo + COL_CHUNK, dout))
        o_ref[:, cols] = jnp.dot(xn, w_ref[:, cols], preferred_element_type=F32)


def _conv_in_body(ins, consts, outs):
    (x_ref,), (g_ref, w_ref), (gb_ref, u_ref) = ins, consts, outs
    xn = _rms(x_ref[...], g_ref[...]).astype(BF16)
    for lo in range(0, D_MODEL, COL_CHUNK):
        cols = slice(lo, lo + COL_CHUNK)
        cols_c = slice(D_MODEL + lo, D_MODEL + lo + COL_CHUNK)
        cols_x = slice(2 * D_MODEL + lo, 2 * D_MODEL + lo + COL_CHUNK)
        gb_ref[:, cols] = jnp.dot(xn, w_ref[:, cols], preferred_element_type=F32)
        gc = jnp.dot(xn, w_ref[:, cols_c], preferred_element_type=F32)
        xp = jnp.dot(xn, w_ref[:, cols_x], preferred_element_type=F32)
        u_ref[:, cols] = gc * xp


def _conv_out_body(ins, consts, outs):
    (u_ref, prev_ref, gb_ref, r_ref), (cw_ref, w_ref), (o_ref,) = ins, consts, outs
    rows = u_ref.shape[0]
    units = rows // CHUNK
    u = u_ref[...]
    prev = prev_ref[...]
    p0 = jnp.broadcast_to(prev[:, 0:1, :], (units, CHUNK, D_MODEL)).reshape(rows, D_MODEL)
    p1 = jnp.broadcast_to(prev[:, 1:2, :], (units, CHUNK, D_MODEL)).reshape(rows, D_MODEL)
    t = lax.broadcasted_iota(jnp.int32, (rows, 1), 0) % CHUNK
    s1 = jnp.where(t == 0, p1, pltpu.roll(u, 1, axis=0))
    s2 = jnp.where(t == 0, p0, jnp.where(t == 1, p1, pltpu.roll(u, 2, axis=0)))
    cw = cw_ref[...]
    y = cw[0:1, :] * s2 + cw[1:2, :] * s1 + cw[2:3, :] * u
    a = (gb_ref[...] * y).astype(BF16)
    o_ref[...] = r_ref[...] + jnp.dot(a, w_ref[...], preferred_element_type=F32)


def _mlp_apply(x, g_ref, wu_ref, wd_ref, gf_ref, final_norm):
    xn = _rms(x, g_ref[...]).astype(BF16)
    acc = x
    for lo in range(0, MLP_HIDDEN, COL_CHUNK):
        cols = slice(lo, lo + COL_CHUNK)
        h = jnp.maximum(jnp.dot(xn, wu_ref[:, cols], preferred_element_type=F32), 0.0)
        acc = acc + jnp.dot((h * h).astype(BF16), wd_ref[cols, :], preferred_element_type=F32)
    return _rms(acc, gf_ref[...]) if final_norm else acc


def _mlp_body(ins, consts, outs, *, final_norm):
    (x_ref,), (g_ref, wu_ref, wd_ref, gf_ref), (o_ref,) = ins, consts, outs
    o_ref[...] = _mlp_apply(x_ref[...], g_ref, wu_ref, wd_ref, gf_ref, final_norm)


def _proj_mlp_body(ins, consts, outs, *, final_norm):
    (a_ref, r_ref), (w_ref, g_ref, wu_ref, wd_ref, gf_ref), (o_ref,) = ins, consts, outs
    x = r_ref[...] + jnp.dot(a_ref[...], w_ref[...], preferred_element_type=F32)
    o_ref[...] = _mlp_apply(x, g_ref, wu_ref, wd_ref, gf_ref, final_norm)


def _sb_qkv_prompt_kernel(*refs, layer, n_layers, first):
    x_ref, g_ref, wq_ref, wkvt_ref = refs[:4]
    q_ref, kt_ref, vt_ref, ktb_ref, vtb_ref = refs[-5:]
    xn = _rms(x_ref[...], g_ref[...]).astype(BF16)
    for lo in range(0, D_MODEL, COL_CHUNK):
        cols = slice(lo, lo + COL_CHUNK)
        q_ref[:, cols] = jnp.dot(xn, wq_ref[:, cols], preferred_element_type=F32).astype(BF16)
    for out_ref, outb_ref, base in ((kt_ref, ktb_ref, 0), (vt_ref, vtb_ref, D_MODEL)):
        for lo in range(0, D_MODEL, COL_CHUNK):
            rows = slice(lo, lo + COL_CHUNK)
            w_rows = slice(base + lo, base + lo + COL_CHUNK)
            t = lax.dot_general(wkvt_ref[w_rows, :], xn, (((1,), (1,)), ((), ())),
                                preferred_element_type=F32)
            if first:
                for other in range(n_layers):
                    out_ref[other, rows, :] = t if other == layer else jnp.zeros_like(t)
            else:
                out_ref[rows, :] = t
            for kt in range(TOKEN_TILE // SB_TILE):
                outb_ref[kt, rows, :] = t[:, kt * SB_TILE:(kt + 1) * SB_TILE].astype(BF16)


def _sb_qkv_prompt(x, gain, wq, wkvt, layer, n_layers, stacked):
    per_seq = SEQ // TOKEN_TILE
    key_tiles = TOKEN_TILE // SB_TILE
    first = stacked is None
    if first:
        kt_spec = pl.BlockSpec((n_layers, None, D_MODEL, TOKEN_TILE),
                               lambda i: (0, i // per_seq, 0, i % per_seq))
    else:
        kt_spec = pl.BlockSpec((None, None, D_MODEL, TOKEN_TILE),
                               lambda i: (layer, i // per_seq, 0, i % per_seq))
    ktb_spec = pl.BlockSpec((None, key_tiles, D_MODEL, SB_TILE),
                            lambda i: (i // per_seq, i % per_seq, 0, 0))
    kt_shape = jax.ShapeDtypeStruct((n_layers, BATCH, D_MODEL, SEQ), F32)
    ktb_shape = jax.ShapeDtypeStruct((BATCH, SEQ // SB_TILE, D_MODEL, SB_TILE), BF16)
    gain = gain.reshape(1, D_MODEL)
    operands = [x, gain, wq, wkvt] + ([] if first else list(stacked))
    in_specs = [pl.BlockSpec((TOKEN_TILE, D_MODEL), lambda i: (i, 0)),
                _resident(gain), _resident(wq), _resident(wkvt)]
    if not first:
        in_specs += [pl.BlockSpec(memory_space=pl.ANY)] * 2
    return pl.pallas_call(
        functools.partial(_sb_qkv_prompt_kernel, layer=layer, n_layers=n_layers, first=first),
        name="sb_qkv_prompt",
        grid=(P_TILES,),
        in_specs=in_specs,
        out_specs=[pl.BlockSpec((TOKEN_TILE, D_MODEL), lambda i: (i, 0)),
                   kt_spec, kt_spec, ktb_spec, ktb_spec],
        out_shape=[jax.ShapeDtypeStruct((N_PROMPT, D_MODEL), BF16),
                   kt_shape, kt_shape, ktb_shape, ktb_shape],
        input_output_aliases={} if first else {4: 1, 5: 2},
        compiler_params=_params("parallel"),
    )(*operands)


def _sb_qkv_sample_kernel(*refs, layer, n_layers, first):
    x_ref, g_ref, wq_ref, wkv_ref = refs[:4]
    q_ref, k_ref, v_ref, kvb_ref = refs[-4:]
    xn = _rms(x_ref[...], g_ref[...]).astype(BF16)
    for lo in range(0, D_MODEL, COL_CHUNK):
        cols = slice(lo, lo + COL_CHUNK)
        q_ref[:, cols] = jnp.dot(xn, wq_ref[:, cols], preferred_element_type=F32).astype(BF16)
    for out_ref, base in ((k_ref, 0), (v_ref, D_MODEL)):
        for lo in range(0, D_MODEL, COL_CHUNK):
            cols = slice(lo, lo + COL_CHUNK)
            w_cols = slice(base + lo, base + lo + COL_CHUNK)
            t = jnp.dot(xn, wkv_ref[:, w_cols], preferred_element_type=F32)
            if first:
                for other in range(n_layers):
                    out_ref[other, :, cols] = t if other == layer else jnp.zeros_like(t)
            else:
                out_ref[:, cols] = t
            kvb_ref[:, w_cols] = t.astype(BF16)


def _sb_qkv_sample(x, gain, wq, wkv, layer, n_layers, stacked):
    first = stacked is None
    tok = pl.BlockSpec((TOKEN_TILE, D_MODEL), lambda i: (i, 0))
    if first:
        kv_spec = pl.BlockSpec((n_layers, TOKEN_TILE, D_MODEL), lambda i: (0, i, 0))
    else:
        kv_spec = pl.BlockSpec((None, TOKEN_TILE, D_MODEL), lambda i: (layer, i, 0))
    kv_shape = jax.ShapeDtypeStruct((n_layers, N_SAMPLE, D_MODEL), F32)
    gain = gain.reshape(1, D_MODEL)
    operands = [x, gain, wq, wkv] + ([] if first else list(stacked))
    in_specs = [tok, _resident(gain), _resident(wq), _resident(wkv)]
    if not first:
        in_specs += [pl.BlockSpec(memory_space=pl.ANY)] * 2
    return pl.pallas_call(
        functools.partial(_sb_qkv_sample_kernel, layer=layer, n_layers=n_layers, first=first),
        name="sb_qkv_sample",
        grid=(S_TILES,),
        in_specs=in_specs,
        out_specs=[tok, kv_spec, kv_spec,
                   pl.BlockSpec((TOKEN_TILE, 2 * D_MODEL), lambda i: (i, 0))],
        out_shape=[jax.ShapeDtypeStruct((N_SAMPLE, D_MODEL), BF16), kv_shape, kv_shape,
                   jax.ShapeDtypeStruct((N_SAMPLE, 2 * D_MODEL), BF16)],
        input_output_aliases={} if first else {4: 1, 5: 2},
        compiler_params=_params("parallel"),
    )(*operands)


def _sb_tiles(qsts, ks, vs, upper2, mask, carries, accs, key_minor):
    nt = (((1,), (1,)), ((), ()))
    z2s, sps, spms, parts = [], [], [], []
    for qst, k in zip(qsts, ks):
        if key_minor:
            z = jnp.dot(qst, k, preferred_element_type=F32)
        else:
            z = lax.dot_general(qst, k, nt, preferred_element_type=F32)
        z2 = z * LOG2E
        sp = jnp.maximum(z2, 0.0) + jnp.log2(1.0 + jnp.exp2(-jnp.abs(z2)))
        spm = sp if mask is None else jnp.where(mask, sp, 0.0)
        z2s.append(z2)
        sps.append(sp)
        spms.append(spm)
        parts.append(jnp.concatenate(_split_bf16(spm, SB_SUFFIX_PASSES), axis=1))
    sums = jnp.dot(jnp.concatenate(parts, axis=0), upper2, preferred_element_type=F32)
    m = qsts[0].shape[0]
    out = []
    for n, (z2, sp, spm, v, carry, acc) in enumerate(zip(z2s, sps, spms, vs, carries, accs)):
        below = sums[n * m:(n + 1) * m] + carry
        a = jnp.exp2(z2 - sp - below)
        if mask is not None:
            a = jnp.where(mask, a, 0.0)
        a = a.astype(BF16)
        if key_minor:
            acc = acc + lax.dot_general(a, v, nt, preferred_element_type=F32)
        else:
            acc = acc + jnp.dot(a, v, preferred_element_type=F32)
        out.append((below[:, 0:1] + spm[:, 0:1], acc))
    return out


def _suffix_matrix2(n):
    s = lax.broadcasted_iota(jnp.int32, (SB_SUFFIX_PASSES * n, n), 0) % n
    j = lax.broadcasted_iota(jnp.int32, (SB_SUFFIX_PASSES * n, n), 1)
    return (s > j).astype(BF16)


def _split_heads(q):
    lane = lax.broadcasted_iota(jnp.int32, (1, SB_PAIR), 1)
    zero = jnp.zeros_like(q)
    return jnp.where(lane < SB_HEAD_DIM, q, zero), jnp.where(lane >= SB_HEAD_DIM, q, zero)


def _stack_heads(q):
    return jnp.concatenate(_split_heads(q), axis=0)


def _unstack_heads(acc):
    lane = lax.broadcasted_iota(jnp.int32, (1, SB_PAIR), 1)
    m = acc.shape[0] // 2
    return jnp.where(lane < SB_HEAD_DIM, acc[:m], acc[m:])


def _causal_mask(tq, tk):
    t = lax.broadcasted_iota(jnp.int32, (2 * tq, tk), 0) % tq
    s = lax.broadcasted_iota(jnp.int32, (2 * tq, tk), 1)
    return s < t


def _sb_prompt_kernel(q_ref, kt_ref, vt_ref, o_ref):
    i = pl.program_id(2)
    pairs = range(SB_PROMPT_PAIRS)
    lanes = [slice(pp * SB_PAIR, (pp + 1) * SB_PAIR) for pp in pairs]
    qst = [_stack_heads(q_ref[:, lanes[pp]]) for pp in pairs]
    upper2 = _suffix_matrix2(SB_TILE)
    causal = _causal_mask(SB_TILE, SB_TILE)
    zero_c = jnp.zeros((2 * SB_TILE, 1), F32)
    zero_a = jnp.zeros((2 * SB_TILE, SB_PAIR), F32)

    def tiles(j, mask, state):
        return _sb_tiles(qst, [kt_ref[j, lanes[pp], :] for pp in pairs],
                         [vt_ref[j, lanes[pp], :] for pp in pairs], upper2, mask,
                         [c for c, _ in state], [a for _, a in state], True)

    def store(state):
        for pp in pairs:
            o_ref[:, lanes[pp]] = _unstack_heads(state[pp][1]).astype(o_ref.dtype)

    zero = [(zero_c, zero_a)] * SB_PROMPT_PAIRS

    @pl.when(i == 0)
    def _():
        store(tiles(0, causal, zero))

    @pl.when(i > 0)
    def _():
        state = tiles(i - 1, None, tiles(i, causal, zero))

        def live(c):
            dead = c[1]
            for pp in pairs[1:]:
                dead = jnp.minimum(dead, c[1 + 2 * pp])
            return (c[0] < i) & (jnp.min(dead) < SB_DEAD_LOG2)

        def body(c):
            j = i - 1 - c[0]
            out = tiles(j, None, [(c[1 + 2 * pp], c[2 + 2 * pp]) for pp in pairs])
            return (c[0] + 1,) + tuple(x for ca in out for x in ca)

        final = lax.while_loop(live, body,
                               (jnp.int32(1),) + tuple(x for ca in state for x in ca))
        store([(final[1 + 2 * pp], final[2 + 2 * pp]) for pp in pairs])


def _sb_prompt(q, ktb, vtb):
    nq = SEQ // SB_TILE
    width = SB_PROMPT_PAIRS * SB_PAIR
    kv_spec = pl.BlockSpec((None, nq, width, SB_TILE), lambda b, p, i: (b, 0, p, 0))
    return pl.pallas_call(
        _sb_prompt_kernel,
        name="sb_prompt",
        grid=(BATCH, D_MODEL // width, nq),
        in_specs=[pl.BlockSpec((SB_TILE, width), lambda b, p, i: (b * nq + i, p)),
                  kv_spec, kv_spec],
        out_specs=pl.BlockSpec((SB_TILE, width), lambda b, p, i: (b * nq + i, p)),
        out_shape=jax.ShapeDtypeStruct((N_PROMPT, D_MODEL), BF16),
        compiler_params=_params("parallel", "parallel", "arbitrary"),
    )(q, ktb, vtb)


def _sb_sample_kernel(q_ref, kn_ref, vn_ref, ck_ref, cv_ref, o_ref, carry_ref, acc_ref):
    upper2 = _suffix_matrix2(SB_TILE)
    n_cache = PAST_LEN // SB_TILE
    new_w = SB_PAIR
    first_w = SB_TILE + new_w

    def cache_kv(pp, j):
        feats = slice(pp * SB_PAIR, (pp + 1) * SB_PAIR)
        keys = slice(j * SB_TILE, (j + 1) * SB_TILE)
        return ck_ref[feats, keys].astype(BF16), cv_ref[feats, keys].astype(BF16)

    def key_minor(new):
        padded = jnp.concatenate([new, jnp.zeros((new_w - DEC_SEQ, SB_PAIR), new.dtype)], axis=0)
        return padded.astype(F32).T.astype(BF16)

    t = lax.broadcasted_iota(jnp.int32, (2 * DEC_SEQ, first_w), 0) % DEC_SEQ
    s = lax.broadcasted_iota(jnp.int32, (2 * DEC_SEQ, first_w), 1) - SB_TILE
    first_mask = s < t
    upper_first = _suffix_matrix2(first_w)

    def save(state):
        for pp, (carry, acc) in enumerate(state):
            carry_ref[pp], acc_ref[pp] = carry, acc

    pairs = range(SB_SAMPLE_PAIRS)
    qsts, k_first, v_first = [], [], []
    for pp in pairs:
        lanes = slice(pp * SB_PAIR, (pp + 1) * SB_PAIR)
        qsts.append(_stack_heads(q_ref[:, lanes]))
        ck, cv = cache_kv(pp, n_cache - 1)
        k_first.append(jnp.concatenate([ck, key_minor(kn_ref[:, lanes])], axis=1))
        v_first.append(jnp.concatenate([cv, key_minor(vn_ref[:, lanes])], axis=1))
    save(_sb_tiles(qsts, k_first, v_first, upper_first, first_mask,
                   [jnp.zeros((2 * DEC_SEQ, 1), F32)] * SB_SAMPLE_PAIRS,
                   [jnp.zeros((2 * DEC_SEQ, SB_PAIR), F32)] * SB_SAMPLE_PAIRS, True))

    for j in reversed(range(n_cache - 1)):
        @pl.when(jnp.min(carry_ref[...]) < SB_DEAD_LOG2)
        def _():
            kvs = [cache_kv(pp, j) for pp in pairs]
            save(_sb_tiles(qsts, [kv[0] for kv in kvs], [kv[1] for kv in kvs], upper2, None,
                           [carry_ref[pp] for pp in pairs], [acc_ref[pp] for pp in pairs], True))

    for pp in range(SB_SAMPLE_PAIRS):
        lanes = slice(pp * SB_PAIR, (pp + 1) * SB_PAIR)
        o_ref[:, lanes] = _unstack_heads(acc_ref[pp]).astype(o_ref.dtype)


def _sb_sample(q, kvb, cache_kt, cache_vt, layer):
    width = SB_SAMPLE_PAIRS * SB_PAIR
    steps = D_MODEL // width
    cache_spec = pl.BlockSpec((None, None, width, PAST_LEN), lambda b, p: (layer, b, p, 0))
    return pl.pallas_call(
        _sb_sample_kernel,
        name="sb_sample",
        grid=(DEC_BATCH, steps),
        in_specs=[pl.BlockSpec((DEC_SEQ, width), lambda b, p: (b, p)),
                  pl.BlockSpec((DEC_SEQ, width), lambda b, p: (b, p)),
                  pl.BlockSpec((DEC_SEQ, width), lambda b, p: (b, steps + p)),
                  cache_spec, cache_spec],
        out_specs=pl.BlockSpec((DEC_SEQ, width), lambda b, p: (b, p)),
        out_shape=jax.ShapeDtypeStruct((N_SAMPLE, D_MODEL), BF16),
        scratch_shapes=[pltpu.VMEM((SB_SAMPLE_PAIRS, 2 * DEC_SEQ, 1), F32),
                        pltpu.VMEM((SB_SAMPLE_PAIRS, 2 * DEC_SEQ, SB_PAIR), F32)],
        compiler_params=_params("parallel", "parallel"),
    )(q, kvb, kvb, cache_kt, cache_vt)


def _sb_sample_near_kernel(q_ref, kn_ref, vn_ref, ck_ref, cv_ref, o_ref, live_ref):
    new_w = SB_PAIR
    first_w = SB_TILE + new_w

    def key_minor(new):
        padded = jnp.concatenate([new, jnp.zeros((new_w - DEC_SEQ, SB_PAIR), new.dtype)], axis=0)
        return padded.astype(F32).T.astype(BF16)

    t = lax.broadcasted_iota(jnp.int32, (2 * DEC_SEQ, first_w), 0) % DEC_SEQ
    s = lax.broadcasted_iota(jnp.int32, (2 * DEC_SEQ, first_w), 1) - SB_TILE
    pairs = range(SB_SAMPLE_PAIRS)
    qsts, k_first, v_first = [], [], []
    for pp in pairs:
        lanes = slice(pp * SB_PAIR, (pp + 1) * SB_PAIR)
        qsts.append(_stack_heads(q_ref[:, lanes]))
        k_first.append(jnp.concatenate([ck_ref[lanes, :].astype(BF16), key_minor(kn_ref[:, lanes])],
                                       axis=1))
        v_first.append(jnp.concatenate([cv_ref[lanes, :].astype(BF16), key_minor(vn_ref[:, lanes])],
                                       axis=1))
    state = _sb_tiles(qsts, k_first, v_first, _suffix_matrix2(first_w), s < t,
                      [jnp.zeros((2 * DEC_SEQ, 1), F32)] * SB_SAMPLE_PAIRS,
                      [jnp.zeros((2 * DEC_SEQ, SB_PAIR), F32)] * SB_SAMPLE_PAIRS, True)
    low = state[0][0]
    for pp in pairs:
        lanes = slice(pp * SB_PAIR, (pp + 1) * SB_PAIR)
        o_ref[:, lanes] = _unstack_heads(state[pp][1]).astype(o_ref.dtype)
        low = jnp.minimum(low, state[pp][0])
    live_ref[...] = jnp.full(live_ref.shape, jnp.min(low), F32)


def _sb_sample_near(q, kvb, cache_kt, cache_vt, layer):
    assert SB_SAMPLE_PAIRS * SB_PAIR == D_MODEL
    near = PAST_LEN // SB_TILE - 1
    cache_spec = pl.BlockSpec((None, None, D_MODEL, SB_TILE), lambda b: (layer, b, 0, near))
    return pl.pallas_call(
        _sb_sample_near_kernel,
        name="sb_sample_near",
        grid=(DEC_BATCH,),
        in_specs=[pl.BlockSpec((DEC_SEQ, D_MODEL), lambda b: (b, 0)),
                  pl.BlockSpec((DEC_SEQ, D_MODEL), lambda b: (b, 0)),
                  pl.BlockSpec((DEC_SEQ, D_MODEL), lambda b: (b, 1)),
                  cache_spec, cache_spec],
        out_specs=[pl.BlockSpec((DEC_SEQ, D_MODEL), lambda b: (b, 0)),
                   pl.BlockSpec((None, 8, LANES), lambda b: (b, 0, 0))],
        out_shape=[jax.ShapeDtypeStruct((N_SAMPLE, D_MODEL), BF16),
                   jax.ShapeDtypeStruct((DEC_BATCH, 8, LANES), F32)],
        compiler_params=_params("parallel"),
    )(q, kvb, kvb, cache_kt, cache_vt)


def _gla_kernel(q_ref, k_ref, v_ref, go_ref, gl_ref, s0_ref, wgk_ref, bgk_ref, nw_ref,
                o_ref, sout_ref, st_ref, *, nchunks):
    if nchunks > 1:
        @pl.when(pl.program_id(1) == 0)
        def _():
            for sq in range(GLA_SEQS):
                for h in range(GLA_HEADS):
                    st_ref[sq, h] = s0_ref[sq, h].T

    states = [_gla_seq(q_ref.at[sq], k_ref.at[sq], v_ref.at[sq], go_ref.at[sq], gl_ref.at[sq],
                       s0_ref.at[sq], wgk_ref, bgk_ref, nw_ref, o_ref.at[sq], st_ref.at[sq],
                       nchunks)
              for sq in range(GLA_SEQS)]

    def write_states():
        for sq in range(GLA_SEQS):
            for h in range(GLA_HEADS):
                sout_ref[sq, h] = states[sq][h].T

    if nchunks > 1:
        pl.when(pl.program_id(1) == nchunks - 1)(write_states)
    else:
        write_states()


def _gla_seq(q_ref, k_ref, v_ref, go_ref, gl_ref, s0_ref, wgk_ref, bgk_ref, nw_ref,
             o_ref, st_ref, nchunks):
    nt = (((1,), (1,)), ((), ()))
    q = q_ref[...] * (GLA_DK ** -0.5)
    k = k_ref[...]
    gate_in = jnp.dot(gl_ref[...].astype(BF16), wgk_ref[...].astype(BF16),
                      preferred_element_type=F32) + bgk_ref[...]
    g = (jnp.minimum(gate_in, 0.0) - jnp.log1p(jnp.exp(-jnp.abs(gate_in)))) / GLA_GATE_NORMALIZER

    t_i = lax.broadcasted_iota(jnp.int32, (CHUNK, CHUNK), 0)
    s_i = lax.broadcasted_iota(jnp.int32, (CHUNK, CHUNK), 1)
    lower = (s_i <= t_i).astype(BF16)
    b = jnp.zeros((CHUNK, GLA_KEY_DIM), F32)
    for part in _split_bf16(g, 3):
        b = b + jnp.dot(lower, part, preferred_element_type=F32)

    qe = (q * jnp.exp(b)).astype(BF16)
    b_last = b[CHUNK - 1:CHUNK]
    kd = (k * jnp.exp(b_last - b)).astype(BF16)
    decay_last = jnp.exp(b_last)

    key_row = lax.broadcasted_iota(jnp.int32, (CHUNK, 1), 0)
    sub_row = lax.broadcasted_iota(jnp.int32, (GLA_SUB, 1), 0)
    key_lane = lax.broadcasted_iota(jnp.int32, (1, CHUNK), 1)
    n_sub = CHUNK // GLA_SUB
    qt, kt = [None] * n_sub, [None] * n_sub
    for blk in range(1, n_sub):
        lo = blk * GLA_SUB
        ref = b[lo - 1:lo]
        qt[blk] = (q[lo:lo + GLA_SUB] * jnp.exp(b[lo:lo + GLA_SUB] - ref)).astype(BF16)
        kt[blk] = (k * jnp.exp(jnp.where(key_row < lo, ref - b, -jnp.inf))).astype(BF16)

    feat_head = lax.broadcasted_iota(jnp.int32, (GLA_KEY_DIM, GLA_HEADS * CHUNK), 0) // GLA_DK
    out_head = lax.broadcasted_iota(jnp.int32, (GLA_KEY_DIM, GLA_HEADS * CHUNK), 1) // CHUNK
    head_sum = (feat_head == out_head).astype(BF16)
    out_key = lax.broadcasted_iota(jnp.int32, (1, GLA_HEADS * CHUNK), 1) % CHUNK
    diag_rows = []
    for blk in range(n_sub):
        rows = slice(blk * GLA_SUB, (blk + 1) * GLA_SUB)
        qi, ki, bi = q[rows], k[rows], b[rows]
        prods = []
        for s in range(GLA_SUB):
            diff = jnp.where(sub_row >= s, bi - bi[s:s + 1], -jnp.inf)
            prods.append((qi * (ki[s:s + 1] * jnp.exp(diff))).astype(BF16))
        sums = jnp.dot(jnp.concatenate(prods, axis=0), head_sum, preferred_element_type=F32)
        a_blk = jnp.zeros((GLA_SUB, GLA_HEADS * CHUNK), F32)
        for s in range(GLA_SUB):
            a_blk = jnp.where(out_key == blk * GLA_SUB + s,
                              sums[s * GLA_SUB:(s + 1) * GLA_SUB], a_blk)
        diag_rows.append(a_blk)
    a_diag = jnp.concatenate(diag_rows, axis=0)

    nw = nw_ref[...]
    new_states = []
    for h in range(GLA_HEADS):
        kl = slice(h * GLA_DK, (h + 1) * GLA_DK)
        vl = slice(h * GLA_DV, (h + 1) * GLA_DV)
        v = v_ref[:, vl].astype(BF16)
        st = st_ref[h] if nchunks > 1 else s0_ref[h].T
        a_rows = [jnp.zeros((GLA_SUB, CHUNK), F32)]
        for blk in range(1, n_sub):
            a_rows.append(lax.dot_general(qt[blk][:, kl], kt[blk][:, kl], nt,
                                          preferred_element_type=F32))
        a = (jnp.concatenate(a_rows, axis=0) + a_diag[:, h * CHUNK:(h + 1) * CHUNK]).astype(BF16)
        o = (lax.dot_general(qe[:, kl], st.astype(BF16), nt, preferred_element_type=F32)
             + jnp.dot(a, v, preferred_element_type=F32))
        st_new = st * decay_last[:, kl] + lax.dot_general(
            v, kd[:, kl], (((0,), (0,)), ((), ())), preferred_element_type=F32)
        if nchunks > 1:
            st_ref[h] = st_new
        new_states.append(st_new)

        o = o * lax.rsqrt(jnp.mean(o * o, axis=-1, keepdims=True) + NORM_EPS) * nw
        go = go_ref[:, vl]
        o_ref[:, vl] = (o * (go * jax.nn.sigmoid(go))).astype(o_ref.dtype)
    return new_states


def _gla(proj, state, w_gk2, b_gk, norm_w, nseq, nchunks):
    def rows(width, col):
        return pl.BlockSpec((GLA_SEQS, CHUNK, width), lambda b, c: (b, c, col))

    state_spec = pl.BlockSpec((GLA_SEQS, GLA_HEADS, GLA_DK, GLA_DV), lambda b, c: (b, 0, 0, 0))
    w_gk2 = jnp.pad(w_gk2, ((0, LANES - GLA_GATE_RANK), (0, 0)))
    b_gk = b_gk.reshape(1, GLA_KEY_DIM)
    norm_w = norm_w.reshape(1, GLA_DV)
    proj = proj.reshape(nseq, nchunks * CHUNK, GLA_IN_PAD)
    o, s_out = pl.pallas_call(
        functools.partial(_gla_kernel, nchunks=nchunks),
        name="gla",
        grid=(nseq // GLA_SEQS, nchunks),
        in_specs=[
            rows(GLA_KEY_DIM, 0), rows(GLA_KEY_DIM, 1), rows(GLA_VALUE_DIM, 1),
            rows(GLA_VALUE_DIM, 2), rows(LANES, GLA_IN_PAD // LANES - 1),
            state_spec,
            _resident(w_gk2), _resident(b_gk), _resident(norm_w),
        ],
        out_specs=[rows(GLA_VALUE_DIM, 0), state_spec],
        out_shape=[
            jax.ShapeDtypeStruct((nseq, nchunks * CHUNK, GLA_VALUE_DIM), BF16),
            jax.ShapeDtypeStruct((nseq, GLA_HEADS, GLA_DK, GLA_DV), F32),
        ],
        scratch_shapes=[pltpu.VMEM((GLA_SEQS, GLA_HEADS, GLA_DV, GLA_DK), F32)],
        compiler_params=_params("parallel", "arbitrary"),
    )(proj, proj, proj, proj, proj, state, w_gk2, b_gk, norm_w)
    return o.reshape(nseq * nchunks * CHUNK, GLA_VALUE_DIM), s_out


def kernel(x_prompt, x_sample, cache_sb_k, cache_sb_v, state_conv, state_gla, norm_mix, norm_mlp,
           sb_w_qkv, sb_w_o, conv_w_in, conv_w, conv_w_out, gla_w_in, gla_w_gk2, gla_b_gk,
           gla_norm, gla_w_o, mlp_w_up, mlp_w_down, norm_final):
    xp = x_prompt.reshape(N_PROMPT, D_MODEL)
    xs = x_sample.reshape(N_SAMPLE, D_MODEL)
    n_sb = cache_sb_k.shape[0]
    cache_kt = cache_sb_k.transpose(0, 1, 3, 4, 2).reshape(n_sb, DEC_BATCH, D_MODEL, PAST_LEN)
    cache_vt = cache_sb_v.transpose(0, 1, 3, 4, 2).reshape(n_sb, DEC_BATCH, D_MODEL, PAST_LEN)
    sb_kv_p, sb_kv_s = None, None
    conv_p, conv_s, gla_p, gla_s = [], [], [], []

    def vec(a):
        return a.reshape(1, -1)

    for i in range(DEPTH):
        j = i // N_MIXERS
        final = i == DEPTH - 1
        mlp_consts = [vec(norm_mlp[i]), mlp_w_up[i].astype(BF16), mlp_w_down[i].astype(BF16),
                      vec(norm_final)]

        def proj_mlp(name, a_p, a_s, w_o):
            return _two_stream(functools.partial(_proj_mlp_body, final_norm=final), name,
                               [a_p, xp], [a_s, xs], [w_o.astype(BF16)] + mlp_consts,
                               [(D_MODEL, F32)])

        if i % N_MIXERS == 0:
            w = sb_w_qkv[j]
            wq = (w[:, :D_MODEL] * (SB_HEAD_DIM ** -0.5)).astype(BF16)
            wkv = w[:, D_MODEL:].astype(BF16)
            q_p, kt, vt, ktb, vtb = _sb_qkv_prompt(xp, norm_mix[i], wq, wkv.T, j, n_sb, sb_kv_p)
            sb_kv_p = (kt, vt)
            q_s, k_s, v_s, kvb_s = _sb_qkv_sample(xs, norm_mix[i], wq, wkv, j, n_sb, sb_kv_s)
            sb_kv_s = (k_s, v_s)
            o_p = _sb_prompt(q_p, ktb, vtb)
            o_near, low = _sb_sample_near(q_s, kvb_s, cache_kt, cache_vt, j)
            o_s = lax.cond(jnp.min(low) < SB_DEAD_LOG2,
                           lambda: _sb_sample(q_s, kvb_s, cache_kt, cache_vt, j),
                           lambda: o_near)
            xp, xs = proj_mlp("sb_out_mlp", o_p, o_s, sb_w_o[j])
        elif i % N_MIXERS == 1:
            gb_p, u_p, gb_s, u_s = _two_stream(
                _conv_in_body, "conv_in", [xp], [xs],
                [vec(norm_mix[i]), conv_w_in[j].astype(BF16)], [(D_MODEL, F32), (D_MODEL, F32)])
            keep = CONV_WIDTH - 1
            tails_p = u_p.reshape(BATCH, SEQ // CHUNK, CHUNK, D_MODEL)[:, :, CHUNK - keep:]
            prev_p = jnp.concatenate(
                [jnp.zeros((BATCH, 1, keep, D_MODEL), F32), tails_p[:, :-1]], axis=1)
            prev_p = prev_p.reshape(N_PROMPT // CHUNK, keep, D_MODEL)
            xp, xs = _two_stream(_conv_out_body, "conv_out",
                                 [u_p, prev_p, gb_p, xp], [u_s, state_conv[j], gb_s, xs],
                                 [conv_w[j], conv_w_out[j].astype(BF16)], [(D_MODEL, F32)])
            xp, xs = _two_stream(functools.partial(_mlp_body, final_norm=final), "mlp",
                                 [xp], [xs], mlp_consts, [(D_MODEL, F32)])
            conv_p.append(tails_p[:, -1])
            conv_s.append(u_s.reshape(DEC_BATCH, DEC_SEQ, D_MODEL)[:, DEC_SEQ - keep:])
        else:
            w_in = jnp.pad(gla_w_in[j], ((0, 0), (0, GLA_IN_PAD - GLA_IN_DIM))).astype(BF16)
            proj_p, proj_s = _two_stream(_norm_proj_body, "gla_in", [xp], [xs],
                                         [vec(norm_mix[i]), w_in], [(GLA_IN_PAD, F32)])
            zero_state = jnp.zeros((BATCH, GLA_HEADS, GLA_DK, GLA_DV), F32)
            o_p, s_p = _gla(proj_p, zero_state, gla_w_gk2[j], gla_b_gk[j], gla_norm[j],
                            BATCH, SEQ // CHUNK)
            o_s, s_s = _gla(proj_s, state_gla[j], gla_w_gk2[j], gla_b_gk[j], gla_norm[j],
                            DEC_BATCH, DEC_SEQ // CHUNK)
            xp, xs = proj_mlp("gla_out_mlp", o_p, o_s, gla_w_o[j])
            gla_p.append(s_p)
            gla_s.append(s_s)

    def prompt_heads(a):
        return a.reshape(n_sb, BATCH, SB_HEADS, SB_HEAD_DIM, SEQ).transpose(0, 1, 4, 2, 3)

    def sample_heads(a):
        return a.reshape(n_sb, DEC_BATCH, DEC_SEQ, SB_HEADS, SB_HEAD_DIM)

    return (xp.reshape(BATCH, SEQ, D_MODEL), xs.reshape(DEC_BATCH, DEC_SEQ, D_MODEL),
            prompt_heads(sb_kv_p[0]), prompt_heads(sb_kv_p[1]),
            sample_heads(sb_kv_s[0]), sample_heads(sb_kv_s[1]),
            jnp.stack(conv_p), jnp.stack(conv_s), jnp.stack(gla_p), jnp.stack(gla_s))
```

```python
import functools

import jax
import jax.numpy as jnp
from jax import lax
from jax.experimental import pallas as pl
from jax.experimental.pallas import tpu as pltpu

F32 = jnp.float32
BF16 = jnp.bfloat16

D_MODEL = 1024
BATCH = 8
SEQ = 2048
DEPTH = 4
DEC_BATCH = 32
DEC_SEQ = 64
PAST_LEN = 1024
CHUNK = 64
N_MIXERS = 3
SB_HEADS = 16
SB_HEAD_DIM = D_MODEL // SB_HEADS
CONV_WIDTH = 3
GLA_HEADS = 4
GLA_KEY_DIM = D_MODEL // 2
GLA_VALUE_DIM = D_MODEL
GLA_DK = GLA_KEY_DIM // GLA_HEADS
GLA_DV = GLA_VALUE_DIM // GLA_HEADS
GLA_GATE_RANK = 16
GLA_GATE_NORMALIZER = 16.0
GLA_IN_DIM = 2 * GLA_KEY_DIM + 2 * GLA_VALUE_DIM + GLA_GATE_RANK
MLP_HIDDEN = 4 * D_MODEL
NORM_EPS = 1e-6

N_PROMPT = BATCH * SEQ
N_SAMPLE = DEC_BATCH * DEC_SEQ

LANES = 128
GLA_IN_PAD = 25 * LANES
TOKEN_TILE = 512
P_TILES = N_PROMPT // TOKEN_TILE
S_TILES = N_SAMPLE // TOKEN_TILE
COL_CHUNK = 512
SB_TILE = 256
SB_PAIR = 2 * SB_HEAD_DIM
SB_PAIRS = D_MODEL // SB_PAIR
SB_SUFFIX_PASSES = 1
SB_PROMPT_PAIRS = 8
SB_SAMPLE_PAIRS = 8
GLA_SEQS = 8
GLA_SUB = 16
VMEM_LIMIT = 56 * 1024 * 1024
LOG2E = 1.4426950408889634
SB_DEAD_LOG2 = 152.0


def _params(*sem):
    return pltpu.CompilerParams(dimension_semantics=sem, vmem_limit_bytes=VMEM_LIMIT)


def _rms(x, gain):
    ms = jnp.mean(x * x, axis=-1, keepdims=True)
    return x * lax.rsqrt(ms + NORM_EPS) * gain


def _split_bf16(x, passes):
    parts = []
    r = x
    for _ in range(passes):
        h = r.astype(BF16)
        parts.append(h)
        r = r - h.astype(F32)
    return parts


def _resident(arr):
    nd = arr.ndim
    return pl.BlockSpec(arr.shape, lambda *_: (0,) * nd, pipeline_mode=pl.Buffered(1))


def _two_stream(body, name, ins_p, ins_s, consts, outs, tile=TOKEN_TILE):
    n_in, n_c, n_out = len(ins_p), len(consts), len(outs)
    p_tiles, s_tiles = N_PROMPT // tile, N_SAMPLE // tile

    def kern(*refs):
        p_in, s_in = refs[:n_in], refs[n_in:2 * n_in]
        c = refs[2 * n_in:2 * n_in + n_c]
        p_out = refs[2 * n_in + n_c:2 * n_in + n_c + n_out]
        s_out = refs[2 * n_in + n_c + n_out:]
        i = pl.program_id(0)

        @pl.when(i < p_tiles)
        def _():
            body(p_in, c, p_out)

        @pl.when(i >= p_tiles)
        def _():
            body(s_in, c, s_out)

    def p_idx(i):
        return jnp.minimum(i, p_tiles - 1)

    def s_idx(i):
        return jnp.maximum(i - p_tiles, 0)

    def spec(arr, tiles, idx):
        block = (arr.shape[0] // tiles,) + arr.shape[1:]
        zeros = (0,) * (arr.ndim - 1)
        return pl.BlockSpec(block, lambda i: (idx(i),) + zeros)

    def out_spec(width, idx):
        return pl.BlockSpec((tile, width), lambda i: (idx(i), 0))

    return pl.pallas_call(
        kern,
        name=name,
        grid=(p_tiles + s_tiles,),
        in_specs=([spec(a, p_tiles, p_idx) for a in ins_p] + [spec(a, s_tiles, s_idx) for a in ins_s]
                  + [_resident(a) for a in consts]),
        out_specs=([out_spec(w, p_idx) for w, _ in outs] + [out_spec(w, s_idx) for w, _ in outs]),
        out_shape=([jax.ShapeDtypeStruct((N_PROMPT, w), dt) for w, dt in outs]
                   + [jax.ShapeDtypeStruct((N_SAMPLE, w), dt) for w, dt in outs]),
        compiler_params=_params("arbitrary"),
    )(*ins_p, *ins_s, *consts)


def _norm_proj_body(ins, consts, outs):
    (x_ref,), (g_ref, w_ref), (o_ref,) = ins, consts, outs
    xn = _rms(x_ref[...], g_ref[...]).astype(BF16)
    dout = o_ref.shape[1]
    for lo in range(0, dout, COL_CHUNK):
        cols = slice(lo, min(lo + COL_CHUNK, dout))
        o_ref[:, cols] = jnp.dot(xn, w_ref[:, cols], preferred_element_type=F32)


def _conv_in_body(ins, consts, outs):
    (x_ref,), (g_ref, w_ref), (gb_ref, u_ref) = ins, consts, outs
    xn = _rms(x_ref[...], g_ref[...]).astype(BF16)
    for lo in range(0, D_MODEL, COL_CHUNK):
        cols = slice(lo, lo + COL_CHUNK)
        cols_c = slice(D_MODEL + lo, D_MODEL + lo + COL_CHUNK)
        cols_x = slice(2 * D_MODEL + lo, 2 * D_MODEL + lo + COL_CHUNK)
        gb_ref[:, cols] = jnp.dot(xn, w_ref[:, cols], preferred_element_type=F32)
        gc = jnp.dot(xn, w_ref[:, cols_c], preferred_element_type=F32)
        xp = jnp.dot(xn, w_ref[:, cols_x], preferred_element_type=F32)
        u_ref[:, cols] = gc * xp


def _conv_out_body(ins, consts, outs):
    (u_ref, prev_ref, gb_ref, r_ref), (cw_ref, w_ref), (o_ref,) = ins, consts, outs
    rows = u_ref.shape[0]
    units = rows // CHUNK
    u = u_ref[...]
    prev = prev_ref[...]
    p0 = jnp.broadcast_to(prev[:, 0:1, :], (units, CHUNK, D_MODEL)).reshape(rows, D_MODEL)
    p1 = jnp.broadcast_to(prev[:, 1:2, :], (units, CHUNK, D_MODEL)).reshape(rows, D_MODEL)
    t = lax.broadcasted_iota(jnp.int32, (rows, 1), 0) % CHUNK
    s1 = jnp.where(t == 0, p1, pltpu.roll(u, 1, axis=0))
    s2 = jnp.where(t == 0, p0, jnp.where(t == 1, p1, pltpu.roll(u, 2, axis=0)))
    cw = cw_ref[...]
    y = cw[0:1, :] * s2 + cw[1:2, :] * s1 + cw[2:3, :] * u
    a = (gb_ref[...] * y).astype(BF16)
    o_ref[...] = r_ref[...] + jnp.dot(a, w_ref[...], preferred_element_type=F32)


def _mlp_apply(x, g_ref, wu_ref, wd_ref, gf_ref, final_norm):
    xn = _rms(x, g_ref[...]).astype(BF16)
    acc = x
    for lo in range(0, MLP_HIDDEN, COL_CHUNK):
        cols = slice(lo, lo + COL_CHUNK)
        h = jnp.maximum(jnp.dot(xn, wu_ref[:, cols], preferred_element_type=F32), 0.0)
        acc = acc + jnp.dot((h * h).astype(BF16), wd_ref[cols, :], preferred_element_type=F32)
    return _rms(acc, gf_ref[...]) if final_norm else acc


def _mlp_body(ins, consts, outs, *, final_norm):
    (x_ref,), (g_ref, wu_ref, wd_ref, gf_ref), (o_ref,) = ins, consts, outs
    o_ref[...] = _mlp_apply(x_ref[...], g_ref, wu_ref, wd_ref, gf_ref, final_norm)


def _proj_mlp_body(ins, consts, outs, *, final_norm):
    (a_ref, r_ref), (w_ref, g_ref, wu_ref, wd_ref, gf_ref), (o_ref,) = ins, consts, outs
    x = r_ref[...] + jnp.dot(a_ref[...], w_ref[...], preferred_element_type=F32)
    o_ref[...] = _mlp_apply(x, g_ref, wu_ref, wd_ref, gf_ref, final_norm)


def _sb_qkv_prompt_kernel(*refs, layer, n_layers, first):
    x_ref, g_ref, wq_ref, wkvt_ref = refs[:4]
    q_ref, kt_ref, vt_ref, ktb_ref, vtb_ref = refs[-5:]
    xn = _rms(x_ref[...], g_ref[...]).astype(BF16)
    for lo in range(0, D_MODEL, COL_CHUNK):
        cols = slice(lo, lo + COL_CHUNK)
        q_ref[:, cols] = jnp.dot(xn, wq_ref[:, cols], preferred_element_type=F32).astype(BF16)
    for out_ref, outb_ref, base in ((kt_ref, ktb_ref, 0), (vt_ref, vtb_ref, D_MODEL)):
        for lo in range(0, D_MODEL, COL_CHUNK):
            rows = slice(lo, lo + COL_CHUNK)
            w_rows = slice(base + lo, base + lo + COL_CHUNK)
            t = lax.dot_general(wkvt_ref[w_rows, :], xn, (((1,), (1,)), ((), ())),
                                preferred_element_type=F32)
            if first:
                for other in range(n_layers):
                    out_ref[other, rows, :] = t if other == layer else jnp.zeros_like(t)
            else:
                out_ref[rows, :] = t
            for kt in range(TOKEN_TILE // SB_TILE):
                outb_ref[kt, rows, :] = t[:, kt * SB_TILE:(kt + 1) * SB_TILE].astype(BF16)


def _sb_qkv_prompt(x, gain, wq, wkvt, layer, n_layers, stacked):
    per_seq = SEQ // TOKEN_TILE
    key_tiles = TOKEN_TILE // SB_TILE
    first = stacked is None
    if first:
        kt_spec = pl.BlockSpec((n_layers, None, D_MODEL, TOKEN_TILE),
                               lambda i: (0, i // per_seq, 0, i % per_seq))
    else:
        kt_spec = pl.BlockSpec((None, None, D_MODEL, TOKEN_TILE),
                               lambda i: (layer, i // per_seq, 0, i % per_seq))
    ktb_spec = pl.BlockSpec((None, key_tiles, D_MODEL, SB_TILE),
                            lambda i: (i // per_seq, i % per_seq, 0, 0))
    kt_shape = jax.ShapeDtypeStruct((n_layers, BATCH, D_MODEL, SEQ), F32)
    ktb_shape = jax.ShapeDtypeStruct((BATCH, SEQ // SB_TILE, D_MODEL, SB_TILE), BF16)
    gain = gain.reshape(1, D_MODEL)
    operands = [x, gain, wq, wkvt] + ([] if first else list(stacked))
    in_specs = [pl.BlockSpec((TOKEN_TILE, D_MODEL), lambda i: (i, 0)),
                _resident(gain), _resident(wq), _resident(wkvt)]
    if not first:
        in_specs += [pl.BlockSpec(memory_space=pl.ANY)] * 2
    return pl.pallas_call(
        functools.partial(_sb_qkv_prompt_kernel, layer=layer, n_layers=n_layers, first=first),
        name="sb_qkv_prompt",
        grid=(P_TILES,),
        in_specs=in_specs,
        out_specs=[pl.BlockSpec((TOKEN_TILE, D_MODEL), lambda i: (i, 0)),
                   kt_spec, kt_spec, ktb_spec, ktb_spec],
        out_shape=[jax.ShapeDtypeStruct((N_PROMPT, D_MODEL), BF16),
                   kt_shape, kt_shape, ktb_shape, ktb_shape],
        input_output_aliases={} if first else {4: 1, 5: 2},
        compiler_params=_params("parallel"),
    )(*operands)


def _sb_qkv_sample_kernel(*refs, layer, n_layers, first):
    x_ref, g_ref, wq_ref, wkv_ref = refs[:4]
    q_ref, k_ref, v_ref, kvb_ref = refs[-4:]
    xn = _rms(x_ref[...], g_ref[...]).astype(BF16)
    for lo in range(0, D_MODEL, COL_CHUNK):
        cols = slice(lo, lo + COL_CHUNK)
        q_ref[:, cols] = jnp.dot(xn, wq_ref[:, cols], preferred_element_type=F32).astype(BF16)
    for out_ref, base in ((k_ref, 0), (v_ref, D_MODEL)):
        for lo in range(0, D_MODEL, COL_CHUNK):
            cols = slice(lo, lo + COL_CHUNK)
            w_cols = slice(base + lo, base + lo + COL_CHUNK)
            t = jnp.dot(xn, wkv_ref[:, w_cols], preferred_element_type=F32)
            if first:
                for other in range(n_layers):
                    out_ref[other, :, cols] = t if other == layer else jnp.zeros_like(t)
            else:
                out_ref[:, cols] = t
            kvb_ref[:, w_cols] = t.astype(BF16)


def _sb_qkv_sample(x, gain, wq, wkv, layer, n_layers, stacked):
    first = stacked is None
    tok = pl.BlockSpec((TOKEN_TILE, D_MODEL), lambda i: (i, 0))
    if first:
        kv_spec = pl.BlockSpec((n_layers, TOKEN_TILE, D_MODEL), lambda i: (0, i, 0))
    else:
        kv_spec = pl.BlockSpec((None, TOKEN_TILE, D_MODEL), lambda i: (layer, i, 0))
    kv_shape = jax.ShapeDtypeStruct((n_layers, N_SAMPLE, D_MODEL), F32)
    gain = gain.reshape(1, D_MODEL)
    operands = [x, gain, wq, wkv] + ([] if first else list(stacked))
    in_specs = [tok, _resident(gain), _resident(wq), _resident(wkv)]
    if not first:
        in_specs += [pl.BlockSpec(memory_space=pl.ANY)] * 2
    return pl.pallas_call(
        functools.partial(_sb_qkv_sample_kernel, layer=layer, n_layers=n_layers, first=first),
        name="sb_qkv_sample",
        grid=(S_TILES,),
        in_specs=in_specs,
        out_specs=[tok, kv_spec, kv_spec,
                   pl.BlockSpec((TOKEN_TILE, 2 * D_MODEL), lambda i: (i, 0))],
        out_shape=[jax.ShapeDtypeStruct((N_SAMPLE, D_MODEL), BF16), kv_shape, kv_shape,
                   jax.ShapeDtypeStruct((N_SAMPLE, 2 * D_MODEL), BF16)],
        input_output_aliases={} if first else {4: 1, 5: 2},
        compiler_params=_params("parallel"),
    )(*operands)


def _sb_tiles(qsts, ks, vs, upper2, mask, carries, accs, key_minor):
    nt = (((1,), (1,)), ((), ()))
    z2s, sps, spms, parts = [], [], [], []
    for qst, k in zip(qsts, ks):
        if key_minor:
            z = jnp.dot(qst, k, preferred_element_type=F32)
        else:
            z = lax.dot_general(qst, k, nt, preferred_element_type=F32)
        z2 = z * LOG2E
        sp = jnp.maximum(z2, 0.0) + jnp.log2(1.0 + jnp.exp2(-jnp.abs(z2)))
        spm = sp if mask is None else jnp.where(mask, sp, 0.0)
        z2s.append(z2)
        sps.append(sp)
        spms.append(spm)
        parts.append(jnp.concatenate(_split_bf16(spm, SB_SUFFIX_PASSES), axis=1))
    sums = jnp.dot(jnp.concatenate(parts, axis=0), upper2, preferred_element_type=F32)
    m = qsts[0].shape[0]
    out = []
    for n, (z2, sp, spm, v, carry, acc) in enumerate(zip(z2s, sps, spms, vs, carries, accs)):
        below = sums[n * m:(n + 1) * m] + carry
        a = jnp.exp2(z2 - sp - below)
        if mask is not None:
            a = jnp.where(mask, a, 0.0)
        a = a.astype(BF16)
        if key_minor:
            acc = acc + lax.dot_general(a, v, nt, preferred_element_type=F32)
        else:
            acc = acc + jnp.dot(a, v, preferred_element_type=F32)
        out.append((below[:, 0:1] + spm[:, 0:1], acc))
    return out


def _suffix_matrix2(n):
    s = lax.broadcasted_iota(jnp.int32, (SB_SUFFIX_PASSES * n, n), 0) % n
    j = lax.broadcasted_iota(jnp.int32, (SB_SUFFIX_PASSES * n, n), 1)
    return (s > j).astype(BF16)


def _split_heads(q):
    lane = lax.broadcasted_iota(jnp.int32, (1, SB_PAIR), 1)
    zero = jnp.zeros_like(q)
    return jnp.where(lane < SB_HEAD_DIM, q, zero), jnp.where(lane >= SB_HEAD_DIM, q, zero)


def _stack_heads(q):
    return jnp.concatenate(_split_heads(q), axis=0)


def _unstack_heads(acc):
    lane = lax.broadcasted_iota(jnp.int32, (1, SB_PAIR), 1)
    m = acc.shape[0] // 2
    return jnp.where(lane < SB_HEAD_DIM, acc[:m], acc[m:])


def _causal_mask(tq, tk):
    t = lax.broadcasted_iota(jnp.int32, (2 * tq, tk), 0) % tq
    s = lax.broadcasted_iota(jnp.int32, (2 * tq, tk), 1)
    return s < t


def _sb_prompt_kernel(q_ref, kt_ref, vt_ref, o_ref):
    i = pl.program_id(2)
    pairs = range(SB_PROMPT_PAIRS)
    lanes = [slice(pp * SB_PAIR, (pp + 1) * SB_PAIR) for pp in pairs]
    qst = [_stack_heads(q_ref[:, lanes[pp]]) for pp in pairs]
    upper2 = _suffix_matrix2(SB_TILE)
    causal = _causal_mask(SB_TILE, SB_TILE)
    zero_c = jnp.zeros((2 * SB_TILE, 1), F32)
    zero_a = jnp.zeros((2 * SB_TILE, SB_PAIR), F32)

    def tiles(j, mask, state):
        return _sb_tiles(qst, [kt_ref[j, lanes[pp], :] for pp in pairs],
                         [vt_ref[j, lanes[pp], :] for pp in pairs], upper2, mask,
                         [c for c, _ in state], [a for _, a in state], True)

    def store(state):
        for pp in pairs:
            o_ref[:, lanes[pp]] = _unstack_heads(state[pp][1]).astype(o_ref.dtype)

    zero = [(zero_c, zero_a)] * SB_PROMPT_PAIRS

    @pl.when(i == 0)
    def _():
        store(tiles(0, causal, zero))

    @pl.when(i > 0)
    def _():
        state = tiles(i - 1, None, tiles(i, causal, zero))

        def live(c):
            dead = c[1]
            for pp in pairs[1:]:
                dead = jnp.minimum(dead, c[1 + 2 * pp])
            return (c[0] < i) & (jnp.min(dead) < SB_DEAD_LOG2)

        def body(c):
            j = i - 1 - c[0]
            out = tiles(j, None, [(c[1 + 2 * pp], c[2 + 2 * pp]) for pp in pairs])
            return (c[0] + 1,) + tuple(x for ca in out for x in ca)

        final = lax.while_loop(live, body,
                               (jnp.int32(1),) + tuple(x for ca in state for x in ca))
        store([(final[1 + 2 * pp], final[2 + 2 * pp]) for pp in pairs])


def _sb_prompt(q, ktb, vtb):
    nq = SEQ // SB_TILE
    width = SB_PROMPT_PAIRS * SB_PAIR
    kv_spec = pl.BlockSpec((None, nq, width, SB_TILE), lambda b, p, i: (b, 0, p, 0))
    return pl.pallas_call(
        _sb_prompt_kernel,
        name="sb_prompt",
        grid=(BATCH, D_MODEL // width, nq),
        in_specs=[pl.BlockSpec((SB_TILE, width), lambda b, p, i: (b * nq + i, p)),
                  kv_spec, kv_spec],
        out_specs=pl.BlockSpec((SB_TILE, width), lambda b, p, i: (b * nq + i, p)),
        out_shape=jax.ShapeDtypeStruct((N_PROMPT, D_MODEL), BF16),
        compiler_params=_params("parallel", "parallel", "arbitrary"),
    )(q, ktb, vtb)


def _sb_sample_kernel(q_ref, kn_ref, vn_ref, ck_ref, cv_ref, o_ref, carry_ref, acc_ref):
    upper2 = _suffix_matrix2(SB_TILE)
    n_cache = PAST_LEN // SB_TILE
    new_w = SB_PAIR
    first_w = SB_TILE + new_w

    def cache_kv(pp, j):
        feats = slice(pp * SB_PAIR, (pp + 1) * SB_PAIR)
        keys = slice(j * SB_TILE, (j + 1) * SB_TILE)
        return ck_ref[feats, keys].astype(BF16), cv_ref[feats, keys].astype(BF16)

    def key_minor(new):
        padded = jnp.concatenate([new, jnp.zeros((new_w - DEC_SEQ, SB_PAIR), new.dtype)], axis=0)
        return padded.astype(F32).T.astype(BF16)

    t = lax.broadcasted_iota(jnp.int32, (2 * DEC_SEQ, first_w), 0) % DEC_SEQ
    s = lax.broadcasted_iota(jnp.int32, (2 * DEC_SEQ, first_w), 1) - SB_TILE
    first_mask = s < t
    upper_first = _suffix_matrix2(first_w)

    def save(state):
        for pp, (carry, acc) in enumerate(state):
            carry_ref[pp], acc_ref[pp] = carry, acc

    pairs = range(SB_SAMPLE_PAIRS)
    qsts, k_first, v_first = [], [], []
    for pp in pairs:
        lanes = slice(pp * SB_PAIR, (pp + 1) * SB_PAIR)
        qsts.append(_stack_heads(q_ref[:, lanes]))
        ck, cv = cache_kv(pp, n_cache - 1)
        k_first.append(jnp.concatenate([ck, key_minor(kn_ref[:, lanes])], axis=1))
        v_first.append(jnp.concatenate([cv, key_minor(vn_ref[:, lanes])], axis=1))
    save(_sb_tiles(qsts, k_first, v_first, upper_first, first_mask,
                   [jnp.zeros((2 * DEC_SEQ, 1), F32)] * SB_SAMPLE_PAIRS,
                   [jnp.zeros((2 * DEC_SEQ, SB_PAIR), F32)] * SB_SAMPLE_PAIRS, True))

    for j in reversed(range(n_cache - 1)):
        @pl.when(jnp.min(carry_ref[...]) < SB_DEAD_LOG2)
        def _():
            kvs = [cache_kv(pp, j) for pp in pairs]
            save(_sb_tiles(qsts, [kv[0] for kv in kvs], [kv[1] for kv in kvs], upper2, None,
                           [carry_ref[pp] for pp in pairs], [acc_ref[pp] for pp in pairs], True))

    for pp in range(SB_SAMPLE_PAIRS):
        lanes = slice(pp * SB_PAIR, (pp + 1) * SB_PAIR)
        o_ref[:, lanes] = _unstack_heads(acc_ref[pp]).astype(o_ref.dtype)


def _sb_sample(q, kvb, cache_kt, cache_vt, layer):
    width = SB_SAMPLE_PAIRS * SB_PAIR
    steps = D_MODEL // width
    cache_spec = pl.BlockSpec((None, None, width, PAST_LEN), lambda b, p: (layer, b, p, 0))
    return pl.pallas_call(
        _sb_sample_kernel,
        name="sb_sample",
        grid=(DEC_BATCH, steps),
        in_specs=[pl.BlockSpec((DEC_SEQ, width), lambda b, p: (b, p)),
                  pl.BlockSpec((DEC_SEQ, width), lambda b, p: (b, p)),
                  pl.BlockSpec((DEC_SEQ, width), lambda b, p: (b, steps + p)),
                  cache_spec, cache_spec],
        out_specs=pl.BlockSpec((DEC_SEQ, width), lambda b, p: (b, p)),
        out_shape=jax.ShapeDtypeStruct((N_SAMPLE, D_MODEL), BF16),
        scratch_shapes=[pltpu.VMEM((SB_SAMPLE_PAIRS, 2 * DEC_SEQ, 1), F32),
                        pltpu.VMEM((SB_SAMPLE_PAIRS, 2 * DEC_SEQ, SB_PAIR), F32)],
        compiler_params=_params("parallel", "parallel"),
    )(q, kvb, kvb, cache_kt, cache_vt)


def _sb_sample_near_kernel(q_ref, kn_ref, vn_ref, ck_ref, cv_ref, o_ref, live_ref):
    new_w = SB_PAIR
    first_w = SB_TILE + new_w

    def key_minor(new):
        padded = jnp.concatenate([new, jnp.zeros((new_w - DEC_SEQ, SB_PAIR), new.dtype)], axis=0)
        return padded.astype(F32).T.astype(BF16)

    t = lax.broadcasted_iota(jnp.int32, (2 * DEC_SEQ, first_w), 0) % DEC_SEQ
    s = lax.broadcasted_iota(jnp.int32, (2 * DEC_SEQ, first_w), 1) - SB_TILE
    pairs = range(SB_SAMPLE_PAIRS)
    qsts, k_first, v_first = [], [], []
    for pp in pairs:
        lanes = slice(pp * SB_PAIR, (pp + 1) * SB_PAIR)
        qsts.append(_stack_heads(q_ref[:, lanes]))
        k_first.append(jnp.concatenate([ck_ref[lanes, :].astype(BF16), key_minor(kn_ref[:, lanes])],
                                       axis=1))
        v_first.append(jnp.concatenate([cv_ref[lanes, :].astype(BF16), key_minor(vn_ref[:, lanes])],
                                       axis=1))
    state = _sb_tiles(qsts, k_first, v_first, _suffix_matrix2(first_w), s < t,
                      [jnp.zeros((2 * DEC_SEQ, 1), F32)] * SB_SAMPLE_PAIRS,
                      [jnp.zeros((2 * DEC_SEQ, SB_PAIR), F32)] * SB_SAMPLE_PAIRS, True)
    low = state[0][0]
    for pp in pairs:
        lanes = slice(pp * SB_PAIR, (pp + 1) * SB_PAIR)
        o_ref[:, lanes] = _unstack_heads(state[pp][1]).astype(o_ref.dtype)
        low = jnp.minimum(low, state[pp][0])
    live_ref[...] = jnp.full(live_ref.shape, jnp.min(low), F32)


def _sb_sample_near(q, kvb, cache_kt, cache_vt, layer):
    assert SB_SAMPLE_PAIRS * SB_PAIR == D_MODEL
    near = PAST_LEN // SB_TILE - 1
    cache_spec = pl.BlockSpec((None, None, D_MODEL, SB_TILE), lambda b: (layer, b, 0, near))
    return pl.pallas_call(
        _sb_sample_near_kernel,
        name="sb_sample_near",
        grid=(DEC_BATCH,),
        in_specs=[pl.BlockSpec((DEC_SEQ, D_MODEL), lambda b: (b, 0)),
                  pl.BlockSpec((DEC_SEQ, D_MODEL), lambda b: (b, 0)),
                  pl.BlockSpec((DEC_SEQ, D_MODEL), lambda b: (b, 1)),
                  cache_spec, cache_spec],
        out_specs=[pl.BlockSpec((DEC_SEQ, D_MODEL), lambda b: (b, 0)),
                   pl.BlockSpec((None, 8, LANES), lambda b: (b, 0, 0))],
        out_shape=[jax.ShapeDtypeStruct((N_SAMPLE, D_MODEL), BF16),
                   jax.ShapeDtypeStruct((DEC_BATCH, 8, LANES), F32)],
        compiler_params=_params("parallel"),
    )(q, kvb, kvb, cache_kt, cache_vt)


def _gla_kernel(q_ref, k_ref, v_ref, go_ref, gl_ref, s0_ref, wgk_ref, bgk_ref, nw_ref,
                o_ref, sout_ref, st_ref, *, nchunks):
    if nchunks > 1:
        @pl.when(pl.program_id(1) == 0)
        def _():
            for sq in range(GLA_SEQS):
                for h in range(GLA_HEADS):
                    st_ref[sq, h] = s0_ref[sq, h].T

    states = [_gla_seq(q_ref.at[sq], k_ref.at[sq], v_ref.at[sq], go_ref.at[sq], gl_ref.at[sq],
                       s0_ref.at[sq], wgk_ref, bgk_ref, nw_ref, o_ref.at[sq], st_ref.at[sq],
                       nchunks)
              for sq in range(GLA_SEQS)]

    def write_states():
        for sq in range(GLA_SEQS):
            for h in range(GLA_HEADS):
                sout_ref[sq, h] = states[sq][h].T

    if nchunks > 1:
        pl.when(pl.program_id(1) == nchunks - 1)(write_states)
    else:
        write_states()


def _gla_seq(q_ref, k_ref, v_ref, go_ref, gl_ref, s0_ref, wgk_ref, bgk_ref, nw_ref,
             o_ref, st_ref, nchunks):
    nt = (((1,), (1,)), ((), ()))
    q = q_ref[...] * (GLA_DK ** -0.5)
    k = k_ref[...]
    gate_in = jnp.dot(gl_ref[...].astype(BF16), wgk_ref[...].astype(BF16),
                      preferred_element_type=F32) + bgk_ref[...]
    g = (jnp.minimum(gate_in, 0.0) - jnp.log1p(jnp.exp(-jnp.abs(gate_in)))) / GLA_GATE_NORMALIZER

    t_i = lax.broadcasted_iota(jnp.int32, (CHUNK, CHUNK), 0)
    s_i = lax.broadcasted_iota(jnp.int32, (CHUNK, CHUNK), 1)
    lower = (s_i <= t_i).astype(BF16)
    b = jnp.zeros((CHUNK, GLA_KEY_DIM), F32)
    for part in _split_bf16(g, 3):
        b = b + jnp.dot(lower, part, preferred_element_type=F32)

    qe = (q * jnp.exp(b)).astype(BF16)
    b_last = b[CHUNK - 1:CHUNK]
    kd = (k * jnp.exp(b_last - b)).astype(BF16)
    decay_last = jnp.exp(b_last)

    key_row = lax.broadcasted_iota(jnp.int32, (CHUNK, 1), 0)
    sub_row = lax.broadcasted_iota(jnp.int32, (GLA_SUB, 1), 0)
    key_lane = lax.broadcasted_iota(jnp.int32, (1, CHUNK), 1)
    n_sub = CHUNK // GLA_SUB
    qt, kt = [None] * n_sub, [None] * n_sub
    for blk in range(1, n_sub):
        lo = blk * GLA_SUB
        ref = b[lo - 1:lo]
        qt[blk] = (q[lo:lo + GLA_SUB] * jnp.exp(b[lo:lo + GLA_SUB] - ref)).astype(BF16)
        kt[blk] = (k * jnp.exp(jnp.where(key_row < lo, ref - b, -jnp.inf))).astype(BF16)

    feat_head = lax.broadcasted_iota(jnp.int32, (GLA_KEY_DIM, GLA_HEADS * CHUNK), 0) // GLA_DK
    out_head = lax.broadcasted_iota(jnp.int32, (GLA_KEY_DIM, GLA_HEADS * CHUNK), 1) // CHUNK
    head_sum = (feat_head == out_head).astype(BF16)
    out_key = lax.broadcasted_iota(jnp.int32, (1, GLA_HEADS * CHUNK), 1) % CHUNK
    diag_rows = []
    for blk in range(n_sub):
        rows = slice(blk * GLA_SUB, (blk + 1) * GLA_SUB)
        qi, ki, bi = q[rows], k[rows], b[rows]
        prods = []
        for s in range(GLA_SUB):
            diff = jnp.where(sub_row >= s, bi - bi[s:s + 1], -jnp.inf)
            prods.append((qi * (ki[s:s + 1] * jnp.exp(diff))).astype(BF16))
        sums = jnp.dot(jnp.concatenate(prods, axis=0), head_sum, preferred_element_type=F32)
        a_blk = jnp.zeros((GLA_SUB, GLA_HEADS * CHUNK), F32)
        for s in range(GLA_SUB):
            a_blk = jnp.where(out_key == blk * GLA_SUB + s,
                              sums[s * GLA_SUB:(s + 1) * GLA_SUB], a_blk)
        diag_rows.append(a_blk)
    a_diag = jnp.concatenate(diag_rows, axis=0)

    nw = nw_ref[...]
    new_states = []
    for h in range(GLA_HEADS):
        kl = slice(h * GLA_DK, (h + 1) * GLA_DK)
        vl = slice(h * GLA_DV, (h + 1) * GLA_DV)
        v = v_ref[:, vl].astype(BF16)
        st = st_ref[h] if nchunks > 1 else s0_ref[h].T
        a_rows = [jnp.zeros((GLA_SUB, CHUNK), F32)]
        for blk in range(1, n_sub):
            a_rows.append(lax.dot_general(qt[blk][:, kl], kt[blk][:, kl], nt,
                                          preferred_element_type=F32))
        a = (jnp.concatenate(a_rows, axis=0) + a_diag[:, h * CHUNK:(h + 1) * CHUNK]).astype(BF16)
        o = (lax.dot_general(qe[:, kl], st.astype(BF16), nt, preferred_element_type=F32)
             + jnp.dot(a, v, preferred_element_type=F32))
        st_new = st * decay_last[:, kl] + lax.dot_general(
            v, kd[:, kl], (((0,), (0,)), ((), ())), preferred_element_type=F32)
        if nchunks > 1:
            st_ref[h] = st_new
        new_states.append(st_new)

        o = o * lax.rsqrt(jnp.mean(o * o, axis=-1, keepdims=True) + NORM_EPS) * nw
        go = go_ref[:, vl]
        o_ref[:, vl] = (o * (go * jax.nn.sigmoid(go))).astype(o_ref.dtype)
    return new_states


def _gla(proj, state, w_gk2, b_gk, norm_w, nseq, nchunks):
    def rows(width, col):
        return pl.BlockSpec((GLA_SEQS, CHUNK, width), lambda b, c: (b, c, col))

    state_spec = pl.BlockSpec((GLA_SEQS, GLA_HEADS, GLA_DK, GLA_DV), lambda b, c: (b, 0, 0, 0))
    w_gk2 = jnp.pad(w_gk2, ((0, LANES - GLA_GATE_RANK), (0, 0)))
    b_gk = b_gk.reshape(1, GLA_KEY_DIM)
    norm_w = norm_w.reshape(1, GLA_DV)
    proj = proj.reshape(nseq, nchunks * CHUNK, GLA_IN_PAD)
    o, s_out = pl.pallas_call(
        functools.partial(_gla_kernel, nchunks=nchunks),
        name="gla",
        grid=(nseq // GLA_SEQS, nchunks),
        in_specs=[
            rows(GLA_KEY_DIM, 0), rows(GLA_KEY_DIM, 1), rows(GLA_VALUE_DIM, 1),
            rows(GLA_VALUE_DIM, 2), rows(LANES, GLA_IN_PAD // LANES - 1),
            state_spec,
            _resident(w_gk2), _resident(b_gk), _resident(norm_w),
        ],
        out_specs=[rows(GLA_VALUE_DIM, 0), state_spec],
        out_shape=[
            jax.ShapeDtypeStruct((nseq, nchunks * CHUNK, GLA_VALUE_DIM), BF16),
            jax.ShapeDtypeStruct((nseq, GLA_HEADS, GLA_DK, GLA_DV), F32),
        ],
        scratch_shapes=[pltpu.VMEM((GLA_SEQS, GLA_HEADS, GLA_DV, GLA_DK), F32)],
        compiler_params=_params("parallel", "arbitrary"),
    )(proj, proj, proj, proj, proj, state, w_gk2, b_gk, norm_w)
    return o.reshape(nseq * nchunks * CHUNK, GLA_VALUE_DIM), s_out


def kernel(x_prompt, x_sample, cache_sb_k, cache_sb_v, state_conv, state_gla, norm_mix, norm_mlp,
           sb_w_qkv, sb_w_o, conv_w_in, conv_w, conv_w_out, gla_w_in, gla_w_gk2, gla_b_gk,
           gla_norm, gla_w_o, mlp_w_up, mlp_w_down, norm_final):
    xp = x_prompt.reshape(N_PROMPT, D_MODEL)
    xs = x_sample.reshape(N_SAMPLE, D_MODEL)
    n_sb = cache_sb_k.shape[0]
    cache_kt = cache_sb_k.transpose(0, 1, 3, 4, 2).reshape(n_sb, DEC_BATCH, D_MODEL, PAST_LEN)
    cache_vt = cache_sb_v.transpose(0, 1, 3, 4, 2).reshape(n_sb, DEC_BATCH, D_MODEL, PAST_LEN)
    sb_kv_p, sb_kv_s = None, None
    conv_p, conv_s, gla_p, gla_s = [], [], [], []

    def vec(a):
        return a.reshape(1, -1)

    for i in range(DEPTH):
        j = i // N_MIXERS
        final = i == DEPTH - 1
        mlp_consts = [vec(norm_mlp[i]), mlp_w_up[i].astype(BF16), mlp_w_down[i].astype(BF16),
                      vec(norm_final)]

        def proj_mlp(name, a_p, a_s, w_o):
            return _two_stream(functools.partial(_proj_mlp_body, final_norm=final), name,
                               [a_p, xp], [a_s, xs], [w_o.astype(BF16)] + mlp_consts,
                               [(D_MODEL, F32)])

        if i % N_MIXERS == 0:
            w = sb_w_qkv[j]
            wq = (w[:, :D_MODEL] * (SB_HEAD_DIM ** -0.5)).astype(BF16)
            wkv = w[:, D_MODEL:].astype(BF16)
            q_p, kt, vt, ktb, vtb = _sb_qkv_prompt(xp, norm_mix[i], wq, wkv.T, j, n_sb, sb_kv_p)
            sb_kv_p = (kt, vt)
            q_s, k_s, v_s, kvb_s = _sb_qkv_sample(xs, norm_mix[i], wq, wkv, j, n_sb, sb_kv_s)
            sb_kv_s = (k_s, v_s)
            o_p = _sb_prompt(q_p, ktb, vtb)
            o_near, low = _sb_sample_near(q_s, kvb_s, cache_kt, cache_vt, j)
            o_s = lax.cond(jnp.min(low) < SB_DEAD_LOG2,
                           lambda: _sb_sample(q_s, kvb_s, cache_kt, cache_vt, j),
                           lambda: o_near)
            xp, xs = proj_mlp("sb_out_mlp", o_p, o_s, sb_w_o[j])
        elif i % N_MIXERS == 1:
            gb_p, u_p, gb_s, u_s = _two_stream(
                _conv_in_body, "conv_in", [xp], [xs],
                [vec(norm_mix[i]), conv_w_in[j].astype(BF16)], [(D_MODEL, F32), (D_MODEL, F32)])
            keep = CONV_WIDTH - 1
            tails_p = u_p.reshape(BATCH, SEQ // CHUNK, CHUNK, D_MODEL)[:, :, CHUNK - keep:]
            prev_p = jnp.concatenate(
                [jnp.zeros((BATCH, 1, keep, D_MODEL), F32), tails_p[:, :-1]], axis=1)
            prev_p = prev_p.reshape(N_PROMPT // CHUNK, keep, D_MODEL)
            xp, xs = _two_stream(_conv_out_body, "conv_out",
                                 [u_p, prev_p, gb_p, xp], [u_s, state_conv[j], gb_s, xs],
                                 [conv_w[j], conv_w_out[j].astype(BF16)], [(D_MODEL, F32)])
            xp, xs = _two_stream(functools.partial(_mlp_body, final_norm=final), "mlp",
                                 [xp], [xs], mlp_consts, [(D_MODEL, F32)])
            conv_p.append(tails_p[:, -1])
            conv_s.append(u_s.reshape(DEC_BATCH, DEC_SEQ, D_MODEL)[:, DEC_SEQ - keep:])
        else:
            w_in = jnp.pad(gla_w_in[j], ((0, 0), (0, GLA_IN_PAD - GLA_IN_DIM))).astype(BF16)
            proj_p, proj_s = _two_stream(_norm_proj_body, "gla_in", [xp], [xs],
                                         [vec(norm_mix[i]), w_in], [(GLA_IN_PAD, F32)])
            zero_state = jnp.zeros((BATCH, GLA_HEADS, GLA_DK, GLA_DV), F32)
            o_p, s_p = _gla(proj_p, zero_state, gla_w_gk2[j], gla_b_gk[j], gla_norm[j],
                            BATCH, SEQ // CHUNK)
            o_s, s_s = _gla(proj_s, state_gla[j], gla_w_gk2[j], gla_b_gk[j], gla_norm[j],
                            DEC_BATCH, DEC_SEQ // CHUNK)
            xp, xs = proj_mlp("gla_out_mlp", o_p, o_s, gla_w_o[j])
            gla_p.append(s_p)
            gla_s.append(s_s)

    def prompt_heads(a):
        return a.reshape(n_sb, BATCH, SB_HEADS, SB_HEAD_DIM, SEQ).transpose(0, 1, 4, 2, 3)

    def sample_heads(a):
        return a.reshape(n_sb, DEC_BATCH, DEC_SEQ, SB_HEADS, SB_HEAD_DIM)

    return (xp.reshape(BATCH, SEQ, D_MODEL), xs.reshape(DEC_BATCH, DEC_SEQ, D_MODEL),
            prompt_heads(sb_kv_p[0]), prompt_heads(sb_kv_p[1]),
            sample_heads(sb_kv_s[0]), sample_heads(sb_kv_s[1]),
            jnp.stack(conv_p), jnp.stack(conv_s), jnp.stack(gla_p), jnp.stack(gla_s))
```
